```python
import math
import jax, jax.numpy as jnp
from jax import lax
import numpy as np

D_MODEL = 2048
BATCH = 32
SEQ = 256
DEPTH = 2
DEC_BATCH = 4
DEC_SEQ = 1024
PAST_LEN = 512

GRID_W = 64
D_MIX = D_MODEL
N_HEADS_A = 16
HEAD_DIM = 64
D_ATTN = N_HEADS_A * HEAD_DIM
WIN_R = 8
WIN_C = 16
Q_BLOCK = 128
D_LRU = D_MIX // 4
LRU_BLOCK = 64
N_LRU_BLOCKS = D_LRU // LRU_BLOCK
CONV_W = 4
LRU_C = 8.0
D_S5 = D_MIX - D_ATTN - D_LRU
S5_GROUP = 16
N_S5_GROUPS = D_S5 // S5_GROUP
S5_STATE = 64
D_IN = 3 * D_ATTN + 2 * D_LRU + D_S5
IN_SPLITS = [D_ATTN, 2 * D_ATTN, 3 * D_ATTN, 3 * D_ATTN + D_LRU, 3 * D_ATTN + 2 * D_LRU]
N_MOD = 6
D_FF = 5632
N_EXPERTS = 8
TOP_K = 2
D_FF_EXPERT = 2816
N_DENSE = (DEPTH + 1) // 2
N_MOE = DEPTH // 2
EPS = 1e-6

kernel_name = 'hybrid_flow_natten_rglru_s5_step'


def rms_norm(x, g):
    xf = x.astype(jnp.float32)
    xf = xf * lax.rsqrt(jnp.mean(xf * xf, axis=-1, keepdims=True) + EPS)
    return xf.astype(x.dtype) * g


def modulate(x, shift, scale):
    return x * (1 + scale) + shift


def adaln(cvec, w_mod, b_mod):
    m = jax.nn.silu(cvec) @ w_mod + b_mod
    return jnp.split(m, N_MOD, axis=-1)


def to_heads(t, g=None):
    B, L, _ = t.shape
    t = t.reshape(B, L, N_HEADS_A, HEAD_DIM)
    if g is not None:
        t = rms_norm(t, g)
    return t.transpose(0, 2, 1, 3)


def context_attention(q, k, v):
    B, H, L, Dh = q.shape
    nb = L // Q_BLOCK
    scale = HEAD_DIM ** -0.5
    qb = jnp.moveaxis(q.reshape(B, H, nb, Q_BLOCK, Dh), 2, 0)

    def block(qi):
        s = jnp.einsum('bhqd,bhkd->bhqk', qi, k).astype(jnp.float32) * scale
        p = jax.nn.softmax(s, axis=-1).astype(v.dtype)
        return jnp.einsum('bhqk,bhkd->bhqd', p, v)

    o = lax.map(block, qb)
    return jnp.moveaxis(o, 0, 2).reshape(B, H, L, Dh)


def neighbourhood_attention(q, k, v, k_ctx, v_ctx, rpb_l):
    B, H, T, Dh = q.shape
    rows = T // GRID_W
    wr = min(WIN_R, rows)
    scale = HEAD_DIM ** -0.5
    qg = q.reshape(B, H, rows, GRID_W, Dh)
    kg = k.reshape(B, H, rows, GRID_W, Dh)
    vg = v.reshape(B, H, rows, GRID_W, Dh)
    row_ids = jnp.arange(rows)
    row_start = jnp.clip(row_ids - wr // 2, 0, rows - wr)
    col = jnp.arange(GRID_W)
    col_start = jnp.clip(col - WIN_C // 2, 0, GRID_W - WIN_C)
    col_in = (col[None, :] >= col_start[:, None]) & (col[None, :] < col_start[:, None] + WIN_C)
    dc_idx = jnp.clip(col[None, :] - col[:, None] + WIN_C - 1, 0, 2 * WIN_C - 2)

    def row_block(args):
        q_r, r, rs = args
        k_band = lax.dynamic_slice_in_dim(kg, rs, wr, axis=2)
        v_band = lax.dynamic_slice_in_dim(vg, rs, wr, axis=2)
        dr_idx = rs + jnp.arange(wr) - r + WIN_R - 1
        bias = rpb_l[:, dr_idx[None, :, None], dc_idx[:, None, :]].astype(jnp.float32)
        s_loc = jnp.einsum('bhqd,bhjkd->bhqjk', q_r, k_band).astype(jnp.float32) * scale + bias
        s_loc = jnp.where(col_in[:, None, :], s_loc, -jnp.inf).reshape(B, H, GRID_W, wr * GRID_W)
        s_ctx = jnp.einsum('bhqd,bhkd->bhqk', q_r, k_ctx).astype(jnp.float32) * scale
        p = jax.nn.softmax(jnp.concatenate([s_loc, s_ctx], axis=-1), axis=-1).astype(v.dtype)
        p_loc = p[..., :wr * GRID_W].reshape(B, H, GRID_W, wr, GRID_W)
        p_ctx = p[..., wr * GRID_W:]
        return (jnp.einsum('bhqjk,bhjkd->bhqd', p_loc, v_band)
                + jnp.einsum('bhqk,bhkd->bhqd', p_ctx, v_ctx))

    o = lax.map(row_block, (jnp.moveaxis(qg, 2, 0), row_ids, row_start))
    return jnp.moveaxis(o, 0, 2).reshape(B, H, T, Dh)


def centred_depthwise_conv(x, w, b):
    L = x.shape[1]
    left = CONV_W // 2
    xp = jnp.pad(x, ((0, 0), (left, CONV_W - 1 - left), (0, 0)))
    y = b
    for j in range(CONV_W):
        y = y + xp[:, j:j + L] * w[j]
    return y


def _lin_combine(e1, e2):
    a1, b1 = e1
    a2, b2 = e2
    return a1 * a2, a2 * b1 + b2


def linear_scan(a, b, h0, reverse):
    if reverse:
        a, b = jnp.flip(a, 1), jnp.flip(b, 1)
    a_cum, b_cum = lax.associative_scan(_lin_combine, (a, b), axis=1)
    h = a_cum * h0[:, None] + b_cum
    h_last = h[:, -1]
    if reverse:
        h = jnp.flip(h, 1)
    return h, h_last


def block_diag(x, w, bias):
    B, L, _ = x.shape
    y = jnp.einsum('blnc,ncd->blnd', x.reshape(B, L, N_LRU_BLOCKS, LRU_BLOCK), w)
    return y.reshape(B, L, D_LRU) + bias


def rglru_mixer(xb, gb, h0, p):
    xc = centred_depthwise_conv(xb, p['conv_w'], p['conv_b'])
    y = 0.0
    finals = []
    for d in range(2):
        r = jax.nn.sigmoid(block_diag(xc, p['wa'][d], p['ba'][d]))
        i = jax.nn.sigmoid(block_diag(xc, p['wi'][d], p['bi'][d]))
        log_a = -LRU_C * r * jax.nn.softplus(-p['lam'][d])
        a = jnp.exp(log_a)
        b_in = jnp.sqrt(-jnp.expm1(2 * log_a)) * (i * xc)
        h, h_last = linear_scan(a, b_in, h0[:, d], reverse=(d == 1))
        y = y + h
        finals.append(h_last)
    return y * jax.nn.gelu(gb), jnp.stack(finals, axis=1)


def _cplx_combine(e1, e2):
    ar1, ai1, br1, bi1 = e1
    ar2, ai2, br2, bi2 = e2
    return (ar1 * ar2 - ai1 * ai2, ar1 * ai2 + ai1 * ar2,
            ar2 * br1 - ai2 * bi1 + br2, ar2 * bi1 + ai2 * br1 + bi2)


def s5_direction(ug, s0_re, s0_im, lam_re, lam_im, log_dt, b_re, b_im, c_re, c_im, reverse):
    dt = jnp.exp(log_dt)[:, None]
    mag = jnp.exp(lam_re * dt)
    abar_re, abar_im = mag * jnp.cos(lam_im * dt), mag * jnp.sin(lam_im * dt)
    den = lam_re * lam_re + lam_im * lam_im
    nr, ni = abar_re - 1, abar_im
    cr = (nr * lam_re + ni * lam_im) / den
    ci = (ni * lam_re - nr * lam_im) / den
    bb_re = cr[..., None] * b_re - ci[..., None] * b_im
    bb_im = cr[..., None] * b_im + ci[..., None] * b_re
    bu_re = jnp.einsum('blgh,gph->blgp', ug, bb_re)
    bu_im = jnp.einsum('blgh,gph->blgp', ug, bb_im)
    if reverse:
        bu_re, bu_im = jnp.flip(bu_re, 1), jnp.flip(bu_im, 1)
    a_re = jnp.broadcast_to(abar_re, bu_re.shape)
    a_im = jnp.broadcast_to(abar_im, bu_re.shape)
    A_r, A_i, B_r, B_i = lax.associative_scan(_cplx_combine, (a_re, a_im, bu_re, bu_im), axis=1)
    s_re = A_r * s0_re[:, None] - A_i * s0_im[:, None] + B_r
    s_im = A_r * s0_im[:, None] + A_i * s0_re[:, None] + B_i
    last_re, last_im = s_re[:, -1], s_im[:, -1]
    if reverse:
        s_re, s_im = jnp.flip(s_re, 1), jnp.flip(s_im, 1)
    y = jnp.einsum('blgp,ghp->blgh', s_re, c_re) - jnp.einsum('blgp,ghp->blgh', s_im, c_im)
    return y, last_re, last_im


def s5_mixer(u, s0_re, s0_im, p):
    B, L, _ = u.shape
    ug = u.reshape(B, L, N_S5_GROUPS, S5_GROUP)
    y = p['d_skip'] * u
    fin_re, fin_im = [], []
    for d in range(2):
        yd, fr, fi = s5_direction(ug, s0_re[:, d], s0_im[:, d], p['lam_re'][d], p['lam_im'][d],
                                  p['log_dt'][d], p['b_re'][d], p['b_im'][d], p['c_re'][d], p['c_im'][d],
                                  reverse=(d == 1))
        y = y + yd.reshape(B, L, D_S5)
        fin_re.append(fr)
        fin_im.append(fi)
    z = jax.nn.gelu(y)
    out = z * jax.nn.sigmoid(z @ p['w_glu'] + p['b_glu'])
    return out, jnp.stack(fin_re, axis=1), jnp.stack(fin_im, axis=1)


def swiglu(x, w1, w3, w2):
    return (jax.nn.silu(x @ w1) * (x @ w3)) @ w2


def moe_ffn(x, router, w1, w3, w2):
    logits = (x @ router).astype(jnp.float32)
    top_v, top_i = lax.top_k(logits, TOP_K)
    gates = jax.nn.softmax(top_v, axis=-1)
    combine = jnp.sum(jax.nn.one_hot(top_i, N_EXPERTS, dtype=jnp.float32) * gates[..., None], axis=-2)
    combine = combine.astype(x.dtype)
    out = jnp.zeros_like(x)
    for e in range(N_EXPERTS):
        out = out + combine[..., e:e + 1] * swiglu(x, w1[e], w3[e], w2[e])
    return out


def run_layer(x, cvec, p, channel_mixer, kv_ctx=None, lru_h0=None, s5_h0=None):
    B, L, _ = x.shape
    sh1, sc1, g1, sh2, sc2, g2 = adaln(cvec, p['w_mod'], p['b_mod'])
    h = modulate(rms_norm(x, p['g_mix']), sh1, sc1)
    q, k, v, xb, gb, u = jnp.split(h @ p['w_in'], IN_SPLITS, axis=-1)
    q = to_heads(q, p['q_g'])
    k = to_heads(k, p['k_g'])
    v = to_heads(v)
    is_ctx = kv_ctx is None
    if is_ctx:
        o_a = context_attention(q, k, v)
        lru_h0 = jnp.zeros((B, 2, D_LRU), x.dtype)
        zs = jnp.zeros((B, 2, N_S5_GROUPS, S5_STATE), x.dtype)
        s5_h0 = (zs, zs)
    else:
        o_a = neighbourhood_attention(q, k, v, kv_ctx[0], kv_ctx[1], p['rpb'])
    o_b, lru_fin = rglru_mixer(xb, gb, lru_h0, p)
    o_c, s5_fin_re, s5_fin_im = s5_mixer(u, s5_h0[0], s5_h0[1], p)
    o_a = o_a.transpose(0, 2, 1, 3).reshape(B, L, D_ATTN)
    mix = jnp.concatenate([o_a, o_b, o_c], axis=-1) @ p['w_out']
    x = x + g1 * mix
    h2 = modulate(rms_norm(x, p['g_ffn']), sh2, sc2)
    x = x + g2 * channel_mixer(h2)
    if is_ctx:
        return x, (k, v, lru_fin, s5_fin_re, s5_fin_im)
    return x, None


def setup_inputs(seed: int = 0) -> dict:
    key = jax.random.key(seed)
    ks = iter(jax.random.split(key, 48))
    f32 = jnp.float32

    def nrm(shape, s):
        return jax.random.normal(next(ks), shape, f32) * s

    def unif(shape, lo, hi):
        return jax.random.uniform(next(ks), shape, f32, minval=lo, maxval=hi)

    inp = {}
    inp['x_prompt'] = nrm((BATCH, SEQ, D_MODEL), 1.0)
    inp['x_sample'] = nrm((DEC_BATCH, DEC_SEQ, D_MODEL), 1.0)
    inp['c'] = nrm((DEC_BATCH, D_MODEL), 1.0)
    inp['cache_k'] = nrm((DEC_BATCH, DEPTH, N_HEADS_A, PAST_LEN, HEAD_DIM), 1.0)
    inp['cache_v'] = nrm((DEC_BATCH, DEPTH, N_HEADS_A, PAST_LEN, HEAD_DIM), 1.0)
    inp['state_lru'] = nrm((DEC_BATCH, DEPTH, 2, D_LRU), 0.5)
    inp['state_s5_re'] = nrm((DEC_BATCH, DEPTH, 2, N_S5_GROUPS, S5_STATE), 0.5)
    inp['state_s5_im'] = nrm((DEC_BATCH, DEPTH, 2, N_S5_GROUPS, S5_STATE), 0.5)
    inp['c_ctx'] = nrm((D_MODEL,), 1.0)
    inp['norm_mix_g'] = 1.0 + nrm((DEPTH, D_MODEL), 0.05)
    inp['norm_ffn_g'] = 1.0 + nrm((DEPTH, D_MODEL), 0.05)
    inp['w_mod'] = nrm((DEPTH, D_MODEL, N_MOD * D_MODEL), 0.5 * D_MODEL ** -0.5)
    inp['b_mod'] = nrm((DEPTH, N_MOD * D_MODEL), 0.01)
    inp['w_in'] = nrm((DEPTH, D_MODEL, D_IN), D_MODEL ** -0.5)
    inp['w_out'] = nrm((DEPTH, D_MIX, D_MODEL), D_MIX ** -0.5)
    inp['q_norm_g'] = 1.0 + nrm((DEPTH, HEAD_DIM), 0.05)
    inp['k_norm_g'] = 1.0 + nrm((DEPTH, HEAD_DIM), 0.05)
    inp['rpb'] = nrm((DEPTH, N_HEADS_A, 2 * WIN_R - 1, 2 * WIN_C - 1), 0.1)
    inp['lru_conv_w'] = nrm((DEPTH, CONV_W, D_LRU), CONV_W ** -0.5)
    inp['lru_conv_b'] = nrm((DEPTH, D_LRU), 0.01)
    inp['lru_wa'] = nrm((DEPTH, 2, N_LRU_BLOCKS, LRU_BLOCK, LRU_BLOCK), LRU_BLOCK ** -0.5)
    inp['lru_ba'] = nrm((DEPTH, 2, D_LRU), 0.01)
    inp['lru_wi'] = nrm((DEPTH, 2, N_LRU_BLOCKS, LRU_BLOCK, LRU_BLOCK), LRU_BLOCK ** -0.5)
    inp['lru_bi'] = nrm((DEPTH, 2, D_LRU), 0.01)
    a_pow = unif((DEPTH, 2, D_LRU), 0.9, 0.999) ** (1.0 / LRU_C)
    inp['lru_lam'] = jnp.log(a_pow) - jnp.log1p(-a_pow)
    inp['s5_lam_re'] = -0.5 + nrm((DEPTH, 2, N_S5_GROUPS, S5_STATE), 0.01)
    inp['s5_lam_im'] = (math.pi * jnp.arange(S5_STATE, dtype=f32)
                        + nrm((DEPTH, 2, N_S5_GROUPS, S5_STATE), 0.01))
    inp['s5_log_dt'] = unif((DEPTH, 2, N_S5_GROUPS), math.log(1e-3), math.log(1e-1))
    inp['s5_b_re'] = nrm((DEPTH, 2, N_S5_GROUPS, S5_STATE, S5_GROUP), (2 * S5_GROUP) ** -0.5)
    inp['s5_b_im'] = nrm((DEPTH, 2, N_S5_GROUPS, S5_STATE, S5_GROUP), (2 * S5_GROUP) ** -0.5)
    inp['s5_c_re'] = nrm((DEPTH, 2, N_S5_GROUPS, S5_GROUP, S5_STATE), (2 * S5_STATE) ** -0.5)
    inp['s5_c_im'] = nrm((DEPTH, 2, N_S5_GROUPS, S5_GROUP, S5_STATE), (2 * S5_STATE) ** -0.5)
    inp['s5_d'] = nrm((DEPTH, D_S5), 1.0)
    inp['s5_w_glu'] = nrm((DEPTH, D_S5, D_S5), D_S5 ** -0.5)
    inp['s5_b_glu'] = nrm((DEPTH, D_S5), 0.01)
    inp['ffn_w1'] = nrm((N_DENSE, D_MODEL, D_FF), D_MODEL ** -0.5)
    inp['ffn_w3'] = nrm((N_DENSE, D_MODEL, D_FF), D_MODEL ** -0.5)
    inp['ffn_w2'] = nrm((N_DENSE, D_FF, D_MODEL), D_FF ** -0.5)
    inp['moe_router'] = nrm((N_MOE, D_MODEL, N_EXPERTS), D_MODEL ** -0.5)
    inp['moe_w1'] = nrm((N_MOE, N_EXPERTS, D_MODEL, D_FF_EXPERT), D_MODEL ** -0.5)
    inp['moe_w3'] = nrm((N_MOE, N_EXPERTS, D_MODEL, D_FF_EXPERT), D_MODEL ** -0.5)
    inp['moe_w2'] = nrm((N_MOE, N_EXPERTS, D_FF_EXPERT, D_MODEL), D_FF_EXPERT ** -0.5)
    return inp


def reference(x_prompt, x_sample, c, cache_k, cache_v, state_lru, state_s5_re, state_s5_im, c_ctx,
              norm_mix_g, norm_ffn_g, w_mod, b_mod, w_in, w_out, q_norm_g, k_norm_g, rpb,
              lru_conv_w, lru_conv_b, lru_wa, lru_ba, lru_wi, lru_bi, lru_lam,
              s5_lam_re, s5_lam_im, s5_log_dt, s5_b_re, s5_b_im, s5_c_re, s5_c_im, s5_d, s5_w_glu, s5_b_glu,
              ffn_w1, ffn_w3, ffn_w2, moe_router, moe_w1, moe_w3, moe_w2):
    xp, xs = x_prompt, x_sample
    ks, vs, lrus, s5rs, s5is = [], [], [], [], []
    for l in range(DEPTH):
        p = {
            'g_mix': norm_mix_g[l], 'g_ffn': norm_ffn_g[l], 'w_mod': w_mod[l], 'b_mod': b_mod[l],
            'w_in': w_in[l], 'w_out': w_out[l], 'q_g': q_norm_g[l], 'k_g': k_norm_g[l], 'rpb': rpb[l],
            'conv_w': lru_conv_w[l], 'conv_b': lru_conv_b[l], 'wa': lru_wa[l], 'ba': lru_ba[l],
            'wi': lru_wi[l], 'bi': lru_bi[l], 'lam': lru_lam[l],
            'lam_re': s5_lam_re[l], 'lam_im': s5_lam_im[l], 'log_dt': s5_log_dt[l],
            'b_re': s5_b_re[l], 'b_im': s5_b_im[l], 'c_re': s5_c_re[l], 'c_im': s5_c_im[l],
            'd_skip': s5_d[l], 'w_glu': s5_w_glu[l], 'b_glu': s5_b_glu[l],
        }
        j = l // 2
        if l % 2 == 0:
            def channel_mixer(h, j=j):
                return swiglu(h, ffn_w1[j], ffn_w3[j], ffn_w2[j])
        else:
            def channel_mixer(h, j=j):
                return moe_ffn(h, moe_router[j], moe_w1[j], moe_w3[j], moe_w2[j])
        xp, (k_c, v_c, lru_c, s5r_c, s5i_c) = run_layer(xp, c_ctx[None, None, :], p, channel_mixer)
        ks.append(k_c)
        vs.append(v_c)
        lrus.append(lru_c)
        s5rs.append(s5r_c)
        s5is.append(s5i_c)
        xs, _ = run_layer(xs, c[:, None, :], p, channel_mixer,
                          kv_ctx=(cache_k[:, l], cache_v[:, l]),
                          lru_h0=state_lru[:, l],
                          s5_h0=(state_s5_re[:, l], state_s5_im[:, l]))
    new_cache_k = jnp.stack(ks, axis=1)
    new_cache_v = jnp.stack(vs, axis=1)
    new_state_lru = jnp.stack(lrus, axis=1)
    new_state_s5_re = jnp.stack(s5rs, axis=1)
    new_state_s5_im = jnp.stack(s5is, axis=1)
    return (xp, xs, new_cache_k, new_cache_v, new_state_lru, new_state_s5_re, new_state_s5_im)
```

```python
import functools

import numpy as np
import jax
import jax.numpy as jnp
from jax import lax
from jax.experimental import pallas as pl
from jax.experimental.pallas import tpu as pltpu

F32 = jnp.float32
BF16 = jnp.bfloat16

D_MODEL = 2048
N_HEADS = 16
HEAD_DIM = 64
D_ATTN = N_HEADS * HEAD_DIM
GRID_W = 64
WIN_R = 8
WIN_C = 16
D_LRU = 512
LRU_BLOCK = 64
CONV_W = 4
LRU_C = 8.0
D_S5 = 512
S5_GROUP = 16
N_S5_GROUPS = 32
S5_STATE = 64
D_REC = 2 * D_LRU + D_S5
D_IN = 3 * D_ATTN + D_REC
N_MOD = 6
N_EXPERTS = 8
EPS = 1e-6
NEG = -1e30

LANES = 128
SUBLANES = 8
MIB = 1024 * 1024

CH = 128
N_CH = D_LRU // CH
S5_ST = (CH // S5_GROUP) * S5_STATE
T_FRONT = CONV_W // 2
T_BACK = CONV_W - 1 - T_FRONT


def _cparams(sem, vmem_mib):
    return pltpu.CompilerParams(dimension_semantics=sem, vmem_limit_bytes=vmem_mib * MIB)


def _adaln_kernel(c_ref, w_ref, b_ref, o_ref):
    s = jax.nn.silu(c_ref[...]).astype(BF16)
    o_ref[0] = jnp.dot(s, w_ref[0].astype(BF16), preferred_element_type=F32) + b_ref[0]


def _adaln(cvecs, w_mod, b_mod):
    depth, d, n = w_mod.shape
    tn = 1024
    return pl.pallas_call(
        _adaln_kernel,
        grid=(depth, n // tn),
        in_specs=[pl.BlockSpec((SUBLANES, d), lambda l, j: (0, 0)),
                  pl.BlockSpec((1, d, tn), lambda l, j: (l, 0, j)),
                  pl.BlockSpec((1, 1, tn), lambda l, j: (l, 0, j))],
        out_specs=pl.BlockSpec((1, SUBLANES, tn), lambda l, j: (l, 0, j)),
        out_shape=jax.ShapeDtypeStruct((depth, SUBLANES, n), F32),
        compiler_params=_cparams(("parallel", "parallel"), 40),
        name="adaln",
    )(cvecs, w_mod, b_mod.reshape(depth, 1, n))


def _rms_mod(x, g, shift, scale):
    xf = x * lax.rsqrt(jnp.mean(x * x, axis=-1, keepdims=True) + EPS)
    return (xf * g) * (1 + scale) + shift


def _in_proj_kernel(x_ref, g_ref, mod_ref, w_ref, qg_ref, kg_ref, ones_ref, *refs, seq, write_cache):
    if write_cache:
        q_ref, k_ref, v_ref, rec_ref, kc_ref, vc_ref, h_scr = refs
    else:
        q_ref, k_ref, v_ref, rec_ref, h_scr = refs
    j = pl.program_id(1)
    tm, tn = q_ref.shape
    heads = tn // HEAD_DIM

    @pl.when(j == 0)
    def _():
        h_scr[...] = _rms_mod(x_ref[...], g_ref[...], mod_ref[0, 0:1, :], mod_ref[0, 1:2, :]).astype(BF16)

    y = jnp.dot(h_scr[...], w_ref[...], preferred_element_type=F32)

    def to_cache(c_ref, val):
        for b in range(tm // seq):
            for h in range(heads):
                c_ref[b, h, :, :] = val[b * seq:(b + 1) * seq, h * HEAD_DIM:(h + 1) * HEAD_DIM]

    @pl.when(j < 4)
    def _():
        y2 = y * y
        hi = y2.astype(BF16)
        lo = (y2 - hi.astype(F32)).astype(BF16)
        ss = (jnp.dot(hi, ones_ref[...], preferred_element_type=F32)
              + jnp.dot(lo, ones_ref[...], preferred_element_type=F32))
        gain = jnp.where(j < 2, qg_ref[...], kg_ref[...])
        yn = (y * lax.rsqrt(ss * (1.0 / HEAD_DIM) + EPS)) * gain

        @pl.when(j < 2)
        def _():
            q_ref[...] = yn.astype(BF16)

        @pl.when(j >= 2)
        def _():
            k_ref[...] = yn.astype(BF16)
            if write_cache:
                to_cache(kc_ref, yn)

    @pl.when((j >= 4) & (j < 6))
    def _():
        v_ref[...] = y.astype(BF16)
        if write_cache:
            to_cache(vc_ref, y)

    @pl.when(j >= 6)
    def _():
        rec_ref[...] = y


def _in_proj(x2d, g, mod, mod_base, rows_per_mod, w_bf, qg, kg, ones_bd, seq, write_cache):
    r, d = x2d.shape
    tm, tn = 512, 512
    n_i = r // tm
    tiles_per_mod = rows_per_mod // tm
    bpt = tm // seq
    hpt = tn // HEAD_DIM

    def col(lo, n):
        return lambda i, j: (i, jnp.clip(j - lo, 0, n - 1))

    out_shape = [jax.ShapeDtypeStruct((r, D_ATTN), BF16)] * 3 + [jax.ShapeDtypeStruct((r, D_REC), F32)]
    out_specs = [pl.BlockSpec((tm, tn), col(0, 2)), pl.BlockSpec((tm, tn), col(2, 2)),
                 pl.BlockSpec((tm, tn), col(4, 2)), pl.BlockSpec((tm, tn), col(6, 3))]
    if write_cache:
        cshape = jax.ShapeDtypeStruct((r // seq, N_HEADS, seq, HEAD_DIM), F32)
        out_shape += [cshape, cshape]
        out_specs += [pl.BlockSpec((bpt, hpt, seq, HEAD_DIM), lambda i, j: (i, jnp.clip(j - 2, 0, 1), 0, 0)),
                      pl.BlockSpec((bpt, hpt, seq, HEAD_DIM), lambda i, j: (i, jnp.clip(j - 4, 0, 1), 0, 0))]
    return pl.pallas_call(
        functools.partial(_in_proj_kernel, seq=seq, write_cache=write_cache),
        grid=(n_i, D_IN // tn),
        in_specs=[pl.BlockSpec((tm, d), lambda i, j: (i, 0)),
                  pl.BlockSpec((1, d), lambda i, j: (0, 0)),
                  pl.BlockSpec((1, N_MOD, d), lambda i, j: (mod_base + i // tiles_per_mod, 0, 0)),
                  pl.BlockSpec((d, tn), lambda i, j: (0, j)),
                  pl.BlockSpec((1, tn), lambda i, j: (0, 0)),
                  pl.BlockSpec((1, tn), lambda i, j: (0, 0)),
                  pl.BlockSpec((tn, tn), lambda i, j: (0, 0))],
        out_specs=out_specs,
        out_shape=out_shape,
        scratch_shapes=[pltpu.VMEM((tm, d), BF16)],
        compiler_params=_cparams(("parallel", "arbitrary"), 52),
        name="in_proj",
    )(x2d, g, mod, w_bf, qg, kg, ones_bd)


def _softmax_rows(parts):
    m = functools.reduce(jnp.maximum, [jnp.max(s, axis=-1, keepdims=True) for s in parts])
    es = [jnp.exp(s - m) for s in parts]
    den = functools.reduce(jnp.add, [jnp.sum(e, axis=-1, keepdims=True) for e in es])
    inv = 1.0 / den
    return [(e * inv).astype(BF16) for e in es]


def _qk(q, k):
    return lax.dot_general(q, k, (((1,), (1,)), ((), ())), preferred_element_type=F32)


def _ctx_attn_kernel(q_ref, k_ref, v_ref, o_ref):
    scale = HEAD_DIM ** -0.5
    for h in range(N_HEADS):
        sl = slice(h * HEAD_DIM, (h + 1) * HEAD_DIM)
        (p,) = _softmax_rows([_qk(q_ref[:, sl], k_ref[:, sl]) * scale])
        o_ref[:, sl] = jnp.dot(p, v_ref[:, sl], preferred_element_type=F32).astype(BF16)


def _ctx_attn(q, k, v, seq):
    r = q.shape[0]
    spec = pl.BlockSpec((seq, D_ATTN), lambda b: (b, 0))
    return pl.pallas_call(
        _ctx_attn_kernel,
        grid=(r // seq,),
        in_specs=[spec, spec, spec],
        out_specs=spec,
        out_shape=jax.ShapeDtypeStruct((r, D_ATTN), BF16),
        compiler_params=_cparams(("parallel",), 32),
        name="ctx_attn",
    )(q, k, v)


def _nbr_plan(seq):
    rows = seq // GRID_W
    wr = min(WIN_R, rows)
    row_start = np.clip(np.arange(rows) - wr // 2, 0, rows - wr)
    rows_per_blk = 256 // GRID_W
    ranges = []
    for qb in range(rows // rows_per_blk):
        rs = row_start[qb * rows_per_blk:(qb + 1) * rows_per_blk]
        lo = int(rs.min()) * GRID_W // LANES * LANES
        hi = -(-(int(rs.max()) + wr) * GRID_W // LANES) * LANES
        ranges.append((lo, hi))
    return rows, wr, row_start, ranges


def _nbr_attn_kernel(q_ref, k_ref, v_ref, kc_ref, vc_ref, tb_ref, o_ref, bias_scr, *, seq):
    rows, wr, row_start, ranges = _nbr_plan(seq)
    scale = HEAD_DIM ** -0.5
    heads = q_ref.shape[1] // HEAD_DIM
    neg = jnp.full((GRID_W, GRID_W), NEG, F32)

    @pl.when(pl.program_id(1) == 0)
    def _():
        for h in range(heads):
            for qr in range(rows):
                for kp in range(rows // 2):
                    blks = []
                    for kr in (2 * kp, 2 * kp + 1):
                        inside = row_start[qr] <= kr < row_start[qr] + wr
                        blks.append(tb_ref[h, kr - qr + WIN_R - 1] if inside else neg)
                    bias_scr[h, qr * GRID_W:(qr + 1) * GRID_W, kp * LANES:(kp + 1) * LANES] = (
                        jnp.concatenate(blks, axis=1))

    for h in range(heads):
        sl = slice(h * HEAD_DIM, (h + 1) * HEAD_DIM)
        kc = kc_ref[0, 0, h].astype(BF16)
        vc = vc_ref[0, 0, h].astype(BF16)
        for qb, (lo, hi) in enumerate(ranges):
            qs = slice(qb * 256, (qb + 1) * 256)
            q = q_ref[qs, sl]
            s_loc = _qk(q, k_ref[lo:hi, sl]) * scale + bias_scr[h, qs, lo:hi]
            s_ctx = _qk(q, kc) * scale
            p_loc, p_ctx = _softmax_rows([s_loc, s_ctx])
            o = (jnp.dot(p_loc, v_ref[lo:hi, sl], preferred_element_type=F32)
                 + jnp.dot(p_ctx, vc, preferred_element_type=F32))
            o_ref[qs, sl] = o.astype(BF16)


def _nbr_attn(q, k, v, cache_k, cache_v, layer, tb, seq):
    r = q.shape[0]
    past = cache_k.shape[3]
    hp = LANES // HEAD_DIM
    spec = pl.BlockSpec((seq, LANES), lambda g, b: (b, g))
    cspec = pl.BlockSpec((1, 1, hp, past, HEAD_DIM), lambda g, b: (b, layer, g, 0, 0))
    return pl.pallas_call(
        functools.partial(_nbr_attn_kernel, seq=seq),
        grid=(N_HEADS // hp, r // seq),
        in_specs=[spec, spec, spec, cspec, cspec,
                  pl.BlockSpec((hp,) + tb.shape[1:], lambda g, b: (g, 0, 0, 0))],
        out_specs=spec,
        out_shape=jax.ShapeDtypeStruct((r, D_ATTN), BF16),
        scratch_shapes=[pltpu.VMEM((hp, seq, seq), F32)],
        compiler_params=_cparams(("parallel", "arbitrary"), 48),
        name="nbr_attn",
    )(q, k, v, cache_k, cache_v, tb)


def _bias_table(rpb_l):
    col = np.arange(GRID_W)
    col_start = np.clip(col - WIN_C // 2, 0, GRID_W - WIN_C)
    col_in = (col[None, :] >= col_start[:, None]) & (col[None, :] < col_start[:, None] + WIN_C)
    dc_idx = np.clip(col[None, :] - col[:, None] + WIN_C - 1, 0, 2 * WIN_C - 2)
    tb = rpb_l[:, :, dc_idx]
    return jnp.where(col_in[None, None], tb, NEG).astype(F32)


def _expm1(x):
    u = jnp.exp(x)
    um1 = u - 1.0
    near = um1 * x / jnp.where(u == 1.0, 1.0, jnp.log(u))
    return jnp.where(x < -0.5, um1, jnp.where(u == 1.0, x, near))


def _rglru_kernel(xb_ref, gb_ref, cw_ref, cb_ref, w_ref, b_ref, lam_ref, h0_ref, o_ref, fin_ref,
                  a_scr, b_scr, *, seq, tc):
    rows = tc * SUBLANES
    n_chunks = seq // tc
    off = T_FRONT * SUBLANES

    def conv(r0):
        y = cb_ref[...]
        for j in range(CONV_W):
            y = y + xb_ref[0, pl.ds(pl.multiple_of(r0 + j * SUBLANES, SUBLANES), rows), :] * cw_ref[j:j + 1, :]
        return y

    for d in range(2):
        reverse = d == 1
        sp = jax.nn.softplus(-lam_ref[0, :, d * CH:(d + 1) * CH])

        def chunk(ci, h, d=d, reverse=reverse, sp=sp):
            c = (n_chunks - 1 - ci) if reverse else ci
            r0 = pl.multiple_of(c * rows, rows)
            xc = conv(r0)
            gates = (jnp.dot(xc.astype(BF16), w_ref[0, :, 2 * d * CH:2 * (d + 1) * CH], preferred_element_type=F32)
                     + b_ref[0, :, 2 * d * CH:2 * (d + 1) * CH])
            rg = jax.nn.sigmoid(gates[:, :CH])
            ig = jax.nn.sigmoid(gates[:, CH:])
            log_a = -LRU_C * rg * sp
            a_scr[...] = jnp.exp(log_a)
            b_scr[...] = jnp.sqrt(-_expm1(2 * log_a)) * (ig * xc)

            def step(t, h):
                tt = (tc - 1 - t) if reverse else t
                rr = pl.multiple_of(tt * SUBLANES, SUBLANES)
                h = a_scr[pl.ds(rr, SUBLANES), :] * h + b_scr[pl.ds(rr, SUBLANES), :]
                b_scr[pl.ds(rr, SUBLANES), :] = h
                return h

            h = lax.fori_loop(0, tc, step, h, unroll=8)
            if not reverse:
                o_ref[0, pl.ds(r0, rows), :] = b_scr[...]
            else:
                g = gb_ref[0, pl.ds(pl.multiple_of(r0 + off, SUBLANES), rows), :]
                o_ref[0, pl.ds(r0, rows), :] = (o_ref[0, pl.ds(r0, rows), :] + b_scr[...]) * jax.nn.gelu(g)
            return h

        fin_ref[0, d] = lax.fori_loop(0, n_chunks, chunk, h0_ref[0, d])


def _rglru(rec_t, conv_w, conv_b, w_g, b_g, lam_g, h0, seq):
    nbg, rows_p, _ = rec_t.shape
    tc = 128
    return pl.pallas_call(
        functools.partial(_rglru_kernel, seq=seq, tc=tc),
        grid=(nbg, N_CH),
        in_specs=[pl.BlockSpec((1, rows_p, CH), lambda g, c: (g, 0, c)),
                  pl.BlockSpec((1, rows_p, CH), lambda g, c: (g, 0, N_CH + c)),
                  pl.BlockSpec((CONV_W, CH), lambda g, c: (0, c)),
                  pl.BlockSpec((1, CH), lambda g, c: (0, c)),
                  pl.BlockSpec((1, CH, 4 * CH), lambda g, c: (c, 0, 0)),
                  pl.BlockSpec((1, 1, 4 * CH), lambda g, c: (c, 0, 0)),
                  pl.BlockSpec((1, 1, 2 * CH), lambda g, c: (c, 0, 0)),
                  pl.BlockSpec((1, 2, SUBLANES, CH), lambda g, c: (g, 0, 0, c))],
        out_specs=[pl.BlockSpec((1, seq * SUBLANES, CH), lambda g, c: (g, 0, c)),
                   pl.BlockSpec((1, 2, SUBLANES, CH), lambda g, c: (g, 0, 0, c))],
        out_shape=[jax.ShapeDtypeStruct((nbg, seq * SUBLANES, D_LRU), F32),
                   jax.ShapeDtypeStruct((nbg, 2, SUBLANES, D_LRU), F32)],
        scratch_shapes=[pltpu.VMEM((tc * SUBLANES, CH), F32), pltpu.VMEM((tc * SUBLANES, CH), F32)],
        compiler_params=_cparams(("parallel", "parallel"), 48),
        name="rglru",
    )(rec_t, rec_t, conv_w, conv_b, w_g, b_g, lam_g, h0)


def _rglru_params(wa, ba, wi, bi, lam):
    bpc = CH // LRU_BLOCK
    eye = jnp.eye(bpc, dtype=F32)

    def dense(w):
        w = w.reshape(N_CH, bpc, LRU_BLOCK, LRU_BLOCK)
        return (w[:, :, :, None, :] * eye[None, :, None, :, None]).reshape(N_CH, CH, CH)

    w_g = jnp.concatenate([dense(wa[0]), dense(wi[0]), dense(wa[1]), dense(wi[1])], axis=-1).astype(BF16)
    b_g = jnp.concatenate([v.reshape(N_CH, 1, CH) for v in (ba[0], bi[0], ba[1], bi[1])], axis=-1)
    lam_g = jnp.concatenate([lam[0].reshape(N_CH, 1, CH), lam[1].reshape(N_CH, 1, CH)], axis=-1)
    return w_g, b_g, lam_g


def _s5_kernel(u_ref, bb_ref, cc_ref, ar_ref, ai_ref, d_ref, s0r_ref, s0i_ref, y_ref, fr_ref, fi_ref,
               st_scr, *, seq, tc):
    rows = tc * SUBLANES
    n_chunks = seq // tc
    off = T_FRONT * SUBLANES

    for d in range(2):
        reverse = d == 1
        a_re = ar_ref[d, 0]
        a_im = ai_ref[d, 0]

        def chunk(ci, carry, d=d, reverse=reverse, a_re=a_re, a_im=a_im):
            c = (n_chunks - 1 - ci) if reverse else ci
            r0 = pl.multiple_of(c * rows, rows)
            u = u_ref[0, pl.ds(pl.multiple_of(r0 + off, SUBLANES), rows), :]
            st_scr[...] = jnp.dot(u.astype(BF16), bb_ref[d, 0], preferred_element_type=F32)

            def step(t, carry):
                s_re, s_im = carry
                tt = (tc - 1 - t) if reverse else t
                rr = pl.ds(pl.multiple_of(tt * SUBLANES, SUBLANES), SUBLANES)
                n_re = a_re * s_re - a_im * s_im + st_scr[rr, :S5_ST]
                n_im = a_re * s_im + a_im * s_re + st_scr[rr, S5_ST:]
                st_scr[rr, :S5_ST] = n_re
                st_scr[rr, S5_ST:] = n_im
                return n_re, n_im

            carry = lax.fori_loop(0, tc, step, carry, unroll=4)
            y = jnp.dot(st_scr[...].astype(BF16), cc_ref[d, 0], preferred_element_type=F32)
            if not reverse:
                y_ref[0, pl.ds(r0, rows), :] = d_ref[...] * u + y
            else:
                y_ref[0, pl.ds(r0, rows), :] = y_ref[0, pl.ds(r0, rows), :] + y
            return carry

        f_re, f_im = lax.fori_loop(0, n_chunks, chunk, (s0r_ref[0, d], s0i_ref[0, d]))
        fr_ref[0, d] = f_re
        fi_ref[0, d] = f_im


def _s5(rec_t, bb, cc, a_re, a_im, d_skip, s0_re, s0_im, seq):
    nbg, rows_p, _ = rec_t.shape
    tc = 64
    n_st = N_CH * S5_ST
    sspec = pl.BlockSpec((1, 2, SUBLANES, S5_ST), lambda g, c: (g, 0, 0, c))
    aspec = pl.BlockSpec((2, 1, SUBLANES, S5_ST), lambda g, c: (0, c, 0, 0))
    return pl.pallas_call(
        functools.partial(_s5_kernel, seq=seq, tc=tc),
        grid=(nbg, N_CH),
        in_specs=[pl.BlockSpec((1, rows_p, CH), lambda g, c: (g, 0, 2 * N_CH + c)),
                  pl.BlockSpec((2, 1, CH, 2 * S5_ST), lambda g, c: (0, c, 0, 0)),
                  pl.BlockSpec((2, 1, 2 * S5_ST, CH), lambda g, c: (0, c, 0, 0)),
                  aspec, aspec,
                  pl.BlockSpec((1, CH), lambda g, c: (0, c)),
                  sspec, sspec],
        out_specs=[pl.BlockSpec((1, seq * SUBLANES, CH), lambda g, c: (g, 0, c)), sspec, sspec],
        out_shape=[jax.ShapeDtypeStruct((nbg, seq * SUBLANES, D_S5), F32),
                   jax.ShapeDtypeStruct((nbg, 2, SUBLANES, n_st), F32),
                   jax.ShapeDtypeStruct((nbg, 2, SUBLANES, n_st), F32)],
        scratch_shapes=[pltpu.VMEM((tc * SUBLANES, 2 * S5_ST), F32)],
        compiler_params=_cparams(("parallel", "parallel"), 48),
        name="s5",
    )(rec_t, bb, cc, a_re, a_im, d_skip, s0_re, s0_im)


def _s5_params(lam_re, lam_im, log_dt, b_re, b_im, c_re, c_im):
    dt = jnp.exp(log_dt)[..., None]
    mag = jnp.exp(lam_re * dt)
    abar_re, abar_im = mag * jnp.cos(lam_im * dt), mag * jnp.sin(lam_im * dt)
    den = lam_re * lam_re + lam_im * lam_im
    nr, ni = abar_re - 1, abar_im
    cr = (nr * lam_re + ni * lam_im) / den
    ci = (ni * lam_re - nr * lam_im) / den
    bb_re = cr[..., None] * b_re - ci[..., None] * b_im
    bb_im = cr[..., None] * b_im + ci[..., None] * b_re
    gpc = CH // S5_GROUP
    eye = jnp.eye(gpc, dtype=F32)

    def in_map(b):
        b = b.reshape(2, N_CH, gpc, S5_STATE, S5_GROUP).transpose(0, 1, 2, 4, 3)
        return (b[:, :, :, :, None, :] * eye[None, None, :, None, :, None]).reshape(2, N_CH, CH, S5_ST)

    def out_map(c):
        c = c.reshape(2, N_CH, gpc, S5_GROUP, S5_STATE).transpose(0, 1, 2, 4, 3)
        return (c[:, :, :, :, None, :] * eye[None, None, :, None, :, None]).reshape(2, N_CH, S5_ST, CH)

    bb = jnp.concatenate([in_map(bb_re), in_map(bb_im)], axis=-1).astype(BF16)
    cc = jnp.concatenate([out_map(c_re), -out_map(c_im)], axis=-2).astype(BF16)

    def bcast(a):
        return jnp.broadcast_to(a.reshape(2, N_CH, 1, S5_ST), (2, N_CH, SUBLANES, S5_ST))

    return bb, cc, bcast(abar_re), bcast(abar_im)


def _out_proj_kernel(x_ref, oa_ref, ob_ref, y_ref, wg_ref, bg_ref, wo_ref, mod_ref, g_ref, *refs, route):
    if route:
        r_ref, x1_ref, h2_ref, comb_ref = refs
    else:
        x1_ref, h2_ref = refs
    z = jax.nn.gelu(y_ref[...])
    gl = jnp.dot(z.astype(BF16), wg_ref[...], preferred_element_type=F32) + bg_ref[...]
    oc = z * jax.nn.sigmoid(gl)
    mix = (jnp.dot(oa_ref[...], wo_ref[:D_ATTN, :], preferred_element_type=F32)
           + jnp.dot(ob_ref[...].astype(BF16), wo_ref[D_ATTN:D_ATTN + D_LRU, :], preferred_element_type=F32)
           + jnp.dot(oc.astype(BF16), wo_ref[D_ATTN + D_LRU:, :], preferred_element_type=F32))
    x1 = x_ref[...] + mod_ref[0, 2:3, :] * mix
    x1_ref[...] = x1
    h2 = _rms_mod(x1, g_ref[...], mod_ref[0, 3:4, :], mod_ref[0, 4:5, :])
    h2_ref[...] = h2.astype(BF16)
    if route:
        logits = jnp.dot(h2, r_ref[...], preferred_element_type=F32, precision=lax.Precision.HIGHEST)
        lane = lax.broadcasted_iota(jnp.int32, logits.shape, 1)
        lg = jnp.where(lane < N_EXPERTS, logits, -jnp.inf)
        m1 = jnp.max(lg, axis=-1, keepdims=True)
        i1 = jnp.min(jnp.where(lg == m1, lane, LANES), axis=-1, keepdims=True)
        lg2 = jnp.where(lane == i1, -jnp.inf, lg)
        m2 = jnp.max(lg2, axis=-1, keepdims=True)
        i2 = jnp.min(jnp.where(lg2 == m2, lane, LANES), axis=-1, keepdims=True)
        e2 = jnp.exp(m2 - m1)
        den = 1.0 + e2
        comb_ref[...] = jnp.where(lane == i1, 1.0 / den, 0.0) + jnp.where(lane == i2, e2 / den, 0.0)


def _out_proj(x2d, oa, ob, y, w_glu_bf, b_glu, w_out_bf, mod, mod_base, rows_per_mod, g, router_pad):
    r, d = x2d.shape
    tm = 512
    tiles_per_mod = rows_per_mod // tm
    route = router_pad is not None
    row = lambda n: pl.BlockSpec((tm, n), lambda i: (i, 0))
    full = lambda a: pl.BlockSpec(a.shape, lambda i: (0,) * a.ndim, pipeline_mode=pl.Buffered(1))
    args = [x2d, oa, ob, y, w_glu_bf, b_glu, w_out_bf, mod, g]
    in_specs = [row(d), row(D_ATTN), row(D_LRU), row(D_S5), full(w_glu_bf), full(b_glu), full(w_out_bf),
                pl.BlockSpec((1, N_MOD, d), lambda i: (mod_base + i // tiles_per_mod, 0, 0)), full(g)]
    out_shape = [jax.ShapeDtypeStruct((r, d), F32), jax.ShapeDtypeStruct((r, d), BF16)]
    out_specs = [row(d), row(d)]
    if route:
        args.append(router_pad)
        in_specs.append(full(router_pad))
        out_shape.append(jax.ShapeDtypeStruct((r, LANES), F32))
        out_specs.append(row(LANES))
    return pl.pallas_call(
        functools.partial(_out_proj_kernel, route=route),
        grid=(r // tm,),
        in_specs=in_specs,
        out_specs=out_specs,
        out_shape=out_shape,
        compiler_params=_cparams(("parallel",), 52),
        name="out_proj",
    )(*args)


def _ffn_kernel(h_ref, x_ref, mod_ref, w1_ref, w3_ref, w2_ref, *refs, tiles_per_expert, n_experts):
    if n_experts > 1:
        comb_ref, o_ref, acc_ref = refs
    else:
        (o_ref,) = refs
        acc_ref = o_ref
    f = pl.program_id(1)
    n_f = pl.num_programs(1)
    h = h_ref[...]
    a = jnp.dot(h, w1_ref[0], preferred_element_type=F32)
    b = jnp.dot(h, w3_ref[0], preferred_element_type=F32)
    part = jnp.dot((jax.nn.silu(a) * b).astype(BF16), w2_ref[0], preferred_element_type=F32)
    first = (f % tiles_per_expert) == 0

    @pl.when(first)
    def _():
        acc_ref[...] = part

    @pl.when(jnp.logical_not(first))
    def _():
        acc_ref[...] += part

    if n_experts > 1:
        @pl.when((f % tiles_per_expert) == tiles_per_expert - 1)
        def _():
            e = f // tiles_per_expert
            lane = lax.broadcasted_iota(jnp.int32, comb_ref.shape, 1)
            w = jnp.sum(jnp.where(lane == e, comb_ref[...], 0.0), axis=-1, keepdims=True)

            @pl.when(e == 0)
            def _():
                o_ref[...] = w * acc_ref[...]

            @pl.when(e > 0)
            def _():
                o_ref[...] += w * acc_ref[...]

    @pl.when(f == n_f - 1)
    def _():
        o_ref[...] = x_ref[...] + mod_ref[0, 5:6, :] * o_ref[...]


def _ffn(h2, x1, mod, mod_base, rows_per_mod, w1, w3, w2, comb):
    r, d = x1.shape
    n_experts, _, f_e = w1.shape
    tm = 1024
    tf = 512 if n_experts == 1 else 256
    tpe = f_e // tf
    tiles_per_mod = rows_per_mod // tm
    args = [h2, x1, mod, w1, w3, w2]
    in_specs = [pl.BlockSpec((tm, d), lambda i, f: (i, 0)),
                pl.BlockSpec((tm, d), lambda i, f: (i, 0), pipeline_mode=pl.Buffered(1)),
                pl.BlockSpec((1, N_MOD, d), lambda i, f: (mod_base + i // tiles_per_mod, 0, 0)),
                pl.BlockSpec((1, d, tf), lambda i, f: (f // tpe, 0, f % tpe)),
                pl.BlockSpec((1, d, tf), lambda i, f: (f // tpe, 0, f % tpe)),
                pl.BlockSpec((1, tf, d), lambda i, f: (f // tpe, f % tpe, 0))]
    scratch = []
    if n_experts > 1:
        args.append(comb)
        in_specs.append(pl.BlockSpec((tm, LANES), lambda i, f: (i, 0)))
        scratch.append(pltpu.VMEM((tm, d), F32))
    return pl.pallas_call(
        functools.partial(_ffn_kernel, tiles_per_expert=tpe, n_experts=n_experts),
        grid=(r // tm, n_experts * tpe),
        in_specs=in_specs,
        out_specs=pl.BlockSpec((tm, d), lambda i, f: (i, 0)),
        out_shape=jax.ShapeDtypeStruct((r, d), F32),
        scratch_shapes=scratch,
        compiler_params=_cparams(("parallel", "arbitrary"), 60),
        name="ffn" if n_experts == 1 else "moe",
    )(*args)


def _time_major(rec, batch, seq):
    c = rec.shape[-1]
    nbg = -(-batch // SUBLANES)
    x = rec.reshape(batch, seq, c)
    x = jnp.pad(x, ((0, nbg * SUBLANES - batch), (T_FRONT, T_BACK), (0, 0)))
    x = x.reshape(nbg, SUBLANES, seq + CONV_W - 1, c).transpose(0, 2, 1, 3)
    return x.reshape(nbg, (seq + CONV_W - 1) * SUBLANES, c)


def _batch_major(y_t, batch, seq):
    nbg = y_t.shape[0]
    c = y_t.shape[-1]
    y = y_t.reshape(nbg, seq, SUBLANES, c).transpose(0, 2, 1, 3).reshape(nbg * SUBLANES, seq, c)
    return y[:batch].reshape(batch * seq, c)


def _group_states(s, batch):
    nbg = -(-batch // SUBLANES)
    s = jnp.pad(s, ((0, nbg * SUBLANES - batch), (0, 0), (0, 0)))
    return s.reshape(nbg, SUBLANES, 2, -1).transpose(0, 2, 1, 3)


def _ungroup_states(s, batch):
    nbg = s.shape[0]
    return s.transpose(0, 2, 1, 3).reshape(nbg * SUBLANES, 2, -1)[:batch]


def _run_layer(x2d, batch, seq, mod, mod_base, rows_per_mod, p, ffn_w, kv_ctx, lru_h0, s5_h0):
    is_ctx = kv_ctx is None
    outs = _in_proj(x2d, p['g_mix'], mod, mod_base, rows_per_mod, p['w_in'], p['q_g'], p['k_g'], p['ones_bd'],
                    seq, is_ctx)
    q, k, v, rec = outs[:4]
    if is_ctx:
        oa = _ctx_attn(q, k, v, seq)
        lru_h0 = jnp.zeros((batch, 2, D_LRU), F32)
        s5_h0 = (jnp.zeros((batch, 2, N_S5_GROUPS * S5_STATE), F32),) * 2
    else:
        oa = _nbr_attn(q, k, v, kv_ctx[0], kv_ctx[1], kv_ctx[2], p['tb'], seq)
    rec_t = _time_major(rec, batch, seq)
    ob_t, lru_fin = _rglru(rec_t, p['conv_w'], p['conv_b'], p['lru_w'], p['lru_b'], p['lru_lam'],
                           _group_states(lru_h0, batch), seq)
    y_t, fin_re, fin_im = _s5(rec_t, p['s5_bb'], p['s5_cc'], p['s5_are'], p['s5_aim'], p['d_skip'],
                              _group_states(s5_h0[0], batch), _group_states(s5_h0[1], batch), seq)
    ob = _batch_major(ob_t, batch, seq)
    y = _batch_major(y_t, batch, seq)
    res = _out_proj(x2d, oa, ob, y, p['w_glu'], p['b_glu'], p['w_out'], mod, mod_base, rows_per_mod,
                    p['g_ffn'], p.get('router'))
    x1, h2 = res[:2]
    comb = res[2] if len(res) > 2 else None
    x2 = _ffn(h2, x1, mod, mod_base, rows_per_mod, ffn_w[0], ffn_w[1], ffn_w[2], comb)
    if not is_ctx:
        return x2, None
    state = (outs[4], outs[5], _ungroup_states(lru_fin, batch),
             _ungroup_states(fin_re, batch).reshape(batch, 2, N_S5_GROUPS, S5_STATE),
             _ungroup_states(fin_im, batch).reshape(batch, 2, N_S5_GROUPS, S5_STATE))
    return x2, state


def kernel(x_prompt, x_sample, c, cache_k, cache_v, state_lru, state_s5_re, state_s5_im, c_ctx, norm_mix_g, norm_ffn_g, w_mod, b_mod, w_in, w_out, q_norm_g, k_norm_g, rpb, lru_conv_w, lru_conv_b, lru_wa, lru_ba, lru_wi, lru_bi, lru_lam, s5_lam_re, s5_lam_im, s5_log_dt, s5_b_re, s5_b_im, s5_c_re, s5_c_im, s5_d, s5_w_glu, s5_b_glu, ffn_w1, ffn_w3, ffn_w2, moe_router, moe_w1, moe_w3, moe_w2):
    batch, seq, d = x_prompt.shape
    dec_batch, dec_seq, _ = x_sample.shape
    depth = w_in.shape[0]
    assert dec_batch + 1 <= SUBLANES

    cvecs = jnp.concatenate([c_ctx[None], c, jnp.zeros((SUBLANES - 1 - dec_batch, d), F32)], axis=0)
    mods = _adaln(cvecs, w_mod, b_mod).reshape(depth, SUBLANES, N_MOD, d)

    heads_per_tile = 512 // HEAD_DIM
    ones_bd = jnp.asarray(np.kron(np.eye(heads_per_tile), np.ones((HEAD_DIM, HEAD_DIM))), BF16)

    xp = x_prompt.reshape(batch * seq, d)
    xs = x_sample.reshape(dec_batch * dec_seq, d)
    ks, vs, lrus, s5rs, s5is = [], [], [], [], []
    for l in range(depth):
        lru_w, lru_b, lru_lam_g = _rglru_params(lru_wa[l], lru_ba[l], lru_wi[l], lru_bi[l], lru_lam[l])
        s5_bb, s5_cc, s5_are, s5_aim = _s5_params(s5_lam_re[l], s5_lam_im[l], s5_log_dt[l], s5_b_re[l],
                                                  s5_b_im[l], s5_c_re[l], s5_c_im[l])
        p = {
            'g_mix': norm_mix_g[l][None], 'g_ffn': norm_ffn_g[l][None],
            'w_in': w_in[l].astype(BF16), 'w_out': w_out[l].astype(BF16),
            'q_g': jnp.tile(q_norm_g[l], heads_per_tile)[None], 'k_g': jnp.tile(k_norm_g[l], heads_per_tile)[None],
            'ones_bd': ones_bd, 'tb': _bias_table(rpb[l]),
            'conv_w': lru_conv_w[l], 'conv_b': lru_conv_b[l][None],
            'lru_w': lru_w, 'lru_b': lru_b, 'lru_lam': lru_lam_g,
            's5_bb': s5_bb, 's5_cc': s5_cc, 's5_are': s5_are, 's5_aim': s5_aim,
            'd_skip': s5_d[l][None], 'w_glu': s5_w_glu[l].astype(BF16), 'b_glu': s5_b_glu[l][None],
        }
        j = l // 2
        if l % 2 == 0:
            ffn_w = (ffn_w1[j][None].astype(BF16), ffn_w3[j][None].astype(BF16), ffn_w2[j][None].astype(BF16))
        else:
            ffn_w = (moe_w1[j].astype(BF16), moe_w3[j].astype(BF16), moe_w2[j].astype(BF16))
            p['router'] = jnp.pad(moe_router[j], ((0, 0), (0, LANES - N_EXPERTS)))
        xp, (k_c, v_c, lru_c, s5r_c, s5i_c) = _run_layer(
            xp, batch, seq, mods[l], 0, batch * seq, p, ffn_w, None, None, None)
        ks.append(k_c)
        vs.append(v_c)
        lrus.append(lru_c)
        s5rs.append(s5r_c)
        s5is.append(s5i_c)
        xs, _ = _run_layer(
            xs, dec_batch, dec_seq, mods[l], 1, dec_seq, p, ffn_w, (cache_k, cache_v, l), state_lru[:, l],
            (state_s5_re[:, l].reshape(dec_batch, 2, -1), state_s5_im[:, l].reshape(dec_batch, 2, -1)))
    return (xp.reshape(batch, seq, d), xs.reshape(dec_batch, dec_seq, d),
            jnp.stack(ks, axis=1), jnp.stack(vs, axis=1), jnp.stack(lrus, axis=1),
            jnp.stack(s5rs, axis=1), jnp.stack(s5is, axis=1))
```

```python
import functools

import numpy as np
import jax
import jax.numpy as jnp
from jax import lax
from jax.experimental import pallas as pl
from jax.experimental.pallas import tpu as pltpu

F32 = jnp.float32
BF16 = jnp.bfloat16

D_MODEL = 2048
N_HEADS = 16
HEAD_DIM = 64
D_ATTN = N_HEADS * HEAD_DIM
GRID_W = 64
WIN_R = 8
WIN_C = 16
D_LRU = 512
LRU_BLOCK = 64
CONV_W = 4
LRU_C = 8.0
D_S5 = 512
S5_GROUP = 16
N_S5_GROUPS = 32
S5_STATE = 64
D_REC = 2 * D_LRU + D_S5
D_IN = 3 * D_ATTN + D_REC
N_MOD = 6
N_EXPERTS = 8
EPS = 1e-6
NEG = -1e30

LANES = 128
SUBLANES = 8
MIB = 1024 * 1024

CH = 128
N_CH = D_LRU // CH
S5_ST = (CH // S5_GROUP) * S5_STATE
T_FRONT = CONV_W // 2
T_BACK = CONV_W - 1 - T_FRONT


def _cparams(sem, vmem_mib):
    return pltpu.CompilerParams(dimension_semantics=sem, vmem_limit_bytes=vmem_mib * MIB)


def _adaln_kernel(c_ref, w_ref, b_ref, o_ref):
    s = jax.nn.silu(c_ref[...]).astype(BF16)
    o_ref[0] = jnp.dot(s, w_ref[0].astype(BF16), preferred_element_type=F32) + b_ref[0]


def _adaln(cvecs, w_mod, b_mod):
    depth, d, n = w_mod.shape
    tn = 1024
    return pl.pallas_call(
        _adaln_kernel,
        grid=(depth, n // tn),
        in_specs=[pl.BlockSpec((SUBLANES, d), lambda l, j: (0, 0)),
                  pl.BlockSpec((1, d, tn), lambda l, j: (l, 0, j)),
                  pl.BlockSpec((1, 1, tn), lambda l, j: (l, 0, j))],
        out_specs=pl.BlockSpec((1, SUBLANES, tn), lambda l, j: (l, 0, j)),
        out_shape=jax.ShapeDtypeStruct((depth, SUBLANES, n), F32),
        compiler_params=_cparams(("parallel", "parallel"), 40),
        name="adaln",
    )(cvecs, w_mod, b_mod.reshape(depth, 1, n))


def _rms_mod(x, g, shift, scale):
    xf = x * lax.rsqrt(jnp.mean(x * x, axis=-1, keepdims=True) + EPS)
    return (xf * g) * (1 + scale) + shift


def _in_proj_kernel(x_ref, g_ref, mod_ref, w_ref, qg_ref, kg_ref, ones_ref, *refs, seq, write_cache):
    if write_cache:
        q_ref, k_ref, v_ref, rec_ref, kc_ref, vc_ref, h_scr = refs
    else:
        q_ref, k_ref, v_ref, rec_ref, h_scr = refs
    j = pl.program_id(1)
    tm, tn = q_ref.shape
    heads = tn // HEAD_DIM

    @pl.when(j == 0)
    def _():
        h_scr[...] = _rms_mod(x_ref[...], g_ref[...], mod_ref[0, 0:1, :], mod_ref[0, 1:2, :]).astype(BF16)

    y = jnp.dot(h_scr[...], w_ref[...], preferred_element_type=F32)

    def to_cache(c_ref, val):
        for b in range(tm // seq):
            for h in range(heads):
                c_ref[b, h, :, :] = val[b * seq:(b + 1) * seq, h * HEAD_DIM:(h + 1) * HEAD_DIM]

    @pl.when(j < 4)
    def _():
        y2 = y * y
        hi = y2.astype(BF16)
        lo = (y2 - hi.astype(F32)).astype(BF16)
        ss = (jnp.dot(hi, ones_ref[...], preferred_element_type=F32)
              + jnp.dot(lo, ones_ref[...], preferred_element_type=F32))
        gain = jnp.where(j < 2, qg_ref[...], kg_ref[...])
        yn = (y * lax.rsqrt(ss * (1.0 / HEAD_DIM) + EPS)) * gain

        @pl.when(j < 2)
        def _():
            q_ref[...] = yn.astype(BF16)

        @pl.when(j >= 2)
        def _():
            k_ref[...] = yn.astype(BF16)
            if write_cache:
                to_cache(kc_ref, yn)

    @pl.when((j >= 4) & (j < 6))
    def _():
        v_ref[...] = y.astype(BF16)
        if write_cache:
            to_cache(vc_ref, y)

    @pl.when(j >= 6)
    def _():
        rec_ref[...] = y


def _in_proj(x2d, g, mod, mod_base, rows_per_mod, w_bf, qg, kg, ones_bd, seq, write_cache):
    r, d = x2d.shape
    tm, tn = 512, 512
    n_i = r // tm
    tiles_per_mod = rows_per_mod // tm
    bpt = tm // seq
    hpt = tn // HEAD_DIM

    def col(lo, n):
        return lambda i, j: (i, jnp.clip(j - lo, 0, n - 1))

    out_shape = [jax.ShapeDtypeStruct((r, D_ATTN), BF16)] * 3 + [jax.ShapeDtypeStruct((r, D_REC), F32)]
    out_specs = [pl.BlockSpec((tm, tn), col(0, 2)), pl.BlockSpec((tm, tn), col(2, 2)),
                 pl.BlockSpec((tm, tn), col(4, 2)), pl.BlockSpec((tm, tn), col(6, 3))]
    if write_cache:
        cshape = jax.ShapeDtypeStruct((r // seq, N_HEADS, seq, HEAD_DIM), F32)
        out_shape += [cshape, cshape]
        out_specs += [pl.BlockSpec((bpt, hpt, seq, HEAD_DIM), lambda i, j: (i, jnp.clip(j - 2, 0, 1), 0, 0)),
                      pl.BlockSpec((bpt, hpt, seq, HEAD_DIM), lambda i, j: (i, jnp.clip(j - 4, 0, 1), 0, 0))]
    return pl.pallas_call(
        functools.partial(_in_proj_kernel, seq=seq, write_cache=write_cache),
        grid=(n_i, D_IN // tn),
        in_specs=[pl.BlockSpec((tm, d), lambda i, j: (i, 0)),
                  pl.BlockSpec((1, d), lambda i, j: (0, 0)),
                  pl.BlockSpec((1, N_MOD, d), lambda i, j: (mod_base + i // tiles_per_mod, 0, 0)),
                  pl.BlockSpec((d, tn), lambda i, j: (0, j)),
                  pl.BlockSpec((1, tn), lambda i, j: (0, 0)),
                  pl.BlockSpec((1, tn), lambda i, j: (0, 0)),
                  pl.BlockSpec((tn, tn), lambda i, j: (0, 0))],
        out_specs=out_specs,
        out_shape=out_shape,
        scratch_shapes=[pltpu.VMEM((tm, d), BF16)],
        compiler_params=_cparams(("parallel", "arbitrary"), 52),
        name="in_proj",
    )(x2d, g, mod, w_bf, qg, kg, ones_bd)


def _softmax_rows(parts):
    m = functools.reduce(jnp.maximum, [jnp.max(s, axis=-1, keepdims=True) for s in parts])
    es = [jnp.exp(s - m) for s in parts]
    den = functools.reduce(jnp.add, [jnp.sum(e, axis=-1, keepdims=True) for e in es])
    inv = 1.0 / den
    return [(e * inv).astype(BF16) for e in es]


def _qk(q, k):
    return lax.dot_general(q, k, (((1,), (1,)), ((), ())), preferred_element_type=F32)


def _ctx_attn_kernel(q_ref, k_ref, v_ref, o_ref):
    scale = HEAD_DIM ** -0.5
    for h in range(N_HEADS):
        sl = slice(h * HEAD_DIM, (h + 1) * HEAD_DIM)
        (p,) = _softmax_rows([_qk(q_ref[:, sl], k_ref[:, sl]) * scale])
        o_ref[:, sl] = jnp.dot(p, v_ref[:, sl], preferred_element_type=F32).astype(BF16)


def _ctx_attn(q, k, v, seq):
    r = q.shape[0]
    spec = pl.BlockSpec((seq, D_ATTN), lambda b: (b, 0))
    return pl.pallas_call(
        _ctx_attn_kernel,
        grid=(r // seq,),
        in_specs=[spec, spec, spec],
        out_specs=spec,
        out_shape=jax.ShapeDtypeStruct((r, D_ATTN), BF16),
        compiler_params=_cparams(("parallel",), 32),
        name="ctx_attn",
    )(q, k, v)


def _nbr_plan(seq):
    rows = seq // GRID_W
    wr = min(WIN_R, rows)
    row_start = np.clip(np.arange(rows) - wr // 2, 0, rows - wr)
    rows_per_blk = 256 // GRID_W
    ranges = []
    for qb in range(rows // rows_per_blk):
        rs = row_start[qb * rows_per_blk:(qb + 1) * rows_per_blk]
        lo = int(rs.min()) * GRID_W // LANES * LANES
        hi = -(-(int(rs.max()) + wr) * GRID_W // LANES) * LANES
        ranges.append((lo, hi))
    return rows, wr, row_start, ranges


def _nbr_attn_kernel(q_ref, k_ref, v_ref, kc_ref, vc_ref, tb_ref, o_ref, bias_scr, *, seq):
    rows, wr, row_start, ranges = _nbr_plan(seq)
    scale = HEAD_DIM ** -0.5
    heads = q_ref.shape[1] // HEAD_DIM
    neg = jnp.full((GRID_W, GRID_W), NEG, F32)

    @pl.when(pl.program_id(1) == 0)
    def _():
        for h in range(heads):
            for qr in range(rows):
                for kp in range(rows // 2):
                    blks = []
                    for kr in (2 * kp, 2 * kp + 1):
                        inside = row_start[qr] <= kr < row_start[qr] + wr
                        blks.append(tb_ref[h, kr - qr + WIN_R - 1] if inside else neg)
                    bias_scr[h, qr * GRID_W:(qr + 1) * GRID_W, kp * LANES:(kp + 1) * LANES] = (
                        jnp.concatenate(blks, axis=1))

    for h in range(heads):
        sl = slice(h * HEAD_DIM, (h + 1) * HEAD_DIM)
        kc = kc_ref[0, 0, h].astype(BF16)
        vc = vc_ref[0, 0, h].astype(BF16)
        for qb, (lo, hi) in enumerate(ranges):
            qs = slice(qb * 256, (qb + 1) * 256)
            q = q_ref[qs, sl]
            s_loc = _qk(q, k_ref[lo:hi, sl]) * scale + bias_scr[h, qs, lo:hi]
            s_ctx = _qk(q, kc) * scale
            p_loc, p_ctx = _softmax_rows([s_loc, s_ctx])
            o = (jnp.dot(p_loc, v_ref[lo:hi, sl], preferred_element_type=F32)
                 + jnp.dot(p_ctx, vc, preferred_element_type=F32))
            o_ref[qs, sl] = o.astype(BF16)


def _nbr_attn(q, k, v, cache_k, cache_v, layer, tb, seq):
    r = q.shape[0]
    past = cache_k.shape[3]
    hp = LANES // HEAD_DIM
    spec = pl.BlockSpec((seq, LANES), lambda g, b: (b, g))
    cspec = pl.BlockSpec((1, 1, hp, past, HEAD_DIM), lambda g, b: (b, layer, g, 0, 0))
    return pl.pallas_call(
        functools.partial(_nbr_attn_kernel, seq=seq),
        grid=(N_HEADS // hp, r // seq),
        in_specs=[spec, spec, spec, cspec, cspec,
                  pl.BlockSpec((hp,) + tb.shape[1:], lambda g, b: (g, 0, 0, 0))],
        out_specs=spec,
        out_shape=jax.ShapeDtypeStruct((r, D_ATTN), BF16),
        scratch_shapes=[pltpu.VMEM((hp, seq, seq), F32)],
        compiler_params=_cparams(("parallel", "arbitrary"), 48),
        name="nbr_attn",
    )(q, k, v, cache_k, cache_v, tb)


def _bias_table(rpb_l):
    col = np.arange(GRID_W)
    col_start = np.clip(col - WIN_C // 2, 0, GRID_W - WIN_C)
    col_in = (col[None, :] >= col_start[:, None]) & (col[None, :] < col_start[:, None] + WIN_C)
    dc_idx = np.clip(col[None, :] - col[:, None] + WIN_C - 1, 0, 2 * WIN_C - 2)
    tb = rpb_l[:, :, dc_idx]
    return jnp.where(col_in[None, None], tb, NEG).astype(F32)


def _expm1(x):
    u = jnp.exp(x)
    um1 = u - 1.0
    near = um1 * x / jnp.where(u == 1.0, 1.0, jnp.log(u))
    return jnp.where(x < -0.5, um1, jnp.where(u == 1.0, x, near))


def _rglru_kernel(xb_ref, gb_ref, cw_ref, cb_ref, w_ref, b_ref, lam_ref, h0_ref, o_ref, fin_ref,
                  a_scr, b_scr, *, seq, tc):
    rows = tc * SUBLANES
    n_chunks = seq // tc
    off = T_FRONT * SUBLANES

    def conv(r0):
        y = cb_ref[...]
        for j in range(CONV_W):
            y = y + xb_ref[0, pl.ds(pl.multiple_of(r0 + j * SUBLANES, SUBLANES), rows), :] * cw_ref[j:j + 1, :]
        return y

    for d in range(2):
        reverse = d == 1
        sp = jax.nn.softplus(-lam_ref[0, :, d * CH:(d + 1) * CH])

        def chunk(ci, h, d=d, reverse=reverse, sp=sp):
            c = (n_chunks - 1 - ci) if reverse else ci
            r0 = pl.multiple_of(c * rows, rows)
            xc = conv(r0)
            gates = (jnp.dot(xc.astype(BF16), w_ref[0, :, 2 * d * CH:2 * (d + 1) * CH], preferred_element_type=F32)
                     + b_ref[0, :, 2 * d * CH:2 * (d + 1) * CH])
            rg = jax.nn.sigmoid(gates[:, :CH])
            ig = jax.nn.sigmoid(gates[:, CH:])
            log_a = -LRU_C * rg * sp
            a_scr[...] = jnp.exp(log_a)
            b_scr[...] = jnp.sqrt(-_expm1(2 * log_a)) * (ig * xc)

            def step(t, h):
                tt = (tc - 1 - t) if reverse else t
                rr = pl.multiple_of(tt * SUBLANES, SUBLANES)
                h = a_scr[pl.ds(rr, SUBLANES), :] * h + b_scr[pl.ds(rr, SUBLANES), :]
                b_scr[pl.ds(rr, SUBLANES), :] = h
                return h

            h = lax.fori_loop(0, tc, step, h, unroll=8)
            if not reverse:
                o_ref[0, pl.ds(r0, rows), :] = b_scr[...]
            else:
                g = gb_ref[0, pl.ds(pl.multiple_of(r0 + off, SUBLANES), rows), :]
                o_ref[0, pl.ds(r0, rows), :] = (o_ref[0, pl.ds(r0, rows), :] + b_scr[...]) * jax.nn.gelu(g)
            return h

        fin_ref[0, d] = lax.fori_loop(0, n_chunks, chunk, h0_ref[0, d])


def _rglru(rec_t, conv_w, conv_b, w_g, b_g, lam_g, h0, seq):
    nbg, rows_p, _ = rec_t.shape
    tc = 128
    return pl.pallas_call(
        functools.partial(_rglru_kernel, seq=seq, tc=tc),
        grid=(nbg, N_CH),
        in_specs=[pl.BlockSpec((1, rows_p, CH), lambda g, c: (g, 0, c)),
                  pl.BlockSpec((1, rows_p, CH), lambda g, c: (g, 0, N_CH + c)),
                  pl.BlockSpec((CONV_W, CH), lambda g, c: (0, c)),
                  pl.BlockSpec((1, CH), lambda g, c: (0, c)),
                  pl.BlockSpec((1, CH, 4 * CH), lambda g, c: (c, 0, 0)),
                  pl.BlockSpec((1, 1, 4 * CH), lambda g, c: (c, 0, 0)),
                  pl.BlockSpec((1, 1, 2 * CH), lambda g, c: (c, 0, 0)),
                  pl.BlockSpec((1, 2, SUBLANES, CH), lambda g, c: (g, 0, 0, c))],
        out_specs=[pl.BlockSpec((1, seq * SUBLANES, CH), lambda g, c: (g, 0, c)),
                   pl.BlockSpec((1, 2, SUBLANES, CH), lambda g, c: (g, 0, 0, c))],
        out_shape=[jax.ShapeDtypeStruct((nbg, seq * SUBLANES, D_LRU), F32),
                   jax.ShapeDtypeStruct((nbg, 2, SUBLANES, D_LRU), F32)],
        scratch_shapes=[pltpu.VMEM((tc * SUBLANES, CH), F32), pltpu.VMEM((tc * SUBLANES, CH), F32)],
        compiler_params=_cparams(("parallel", "parallel"), 48),
        name="rglru",
    )(rec_t, rec_t, conv_w, conv_b, w_g, b_g, lam_g, h0)


def _rglru_params(wa, ba, wi, bi, lam):
    bpc = CH // LRU_BLOCK
    eye = jnp.eye(bpc, dtype=F32)

    def dense(w):
        w = w.reshape(N_CH, bpc, LRU_BLOCK, LRU_BLOCK)
        return (w[:, :, :, None, :] * eye[None, :, None, :, None]).reshape(N_CH, CH, CH)

    w_g = jnp.concatenate([dense(wa[0]), dense(wi[0]), dense(wa[1]), dense(wi[1])], axis=-1).astype(BF16)
    b_g = jnp.concatenate([v.reshape(N_CH, 1, CH) for v in (ba[0], bi[0], ba[1], bi[1])], axis=-1)
    lam_g = jnp.concatenate([lam[0].reshape(N_CH, 1, CH), lam[1].reshape(N_CH, 1, CH)], axis=-1)
    return w_g, b_g, lam_g


def _s5_kernel(u_ref, bb_ref, cc_ref, ar_ref, ai_ref, d_ref, s0r_ref, s0i_ref, y_ref, fr_ref, fi_ref,
               st_scr, *, seq, tc):
    rows = tc * SUBLANES
    n_chunks = seq // tc
    off = T_FRONT * SUBLANES

    for d in range(2):
        reverse = d == 1
        a_re = ar_ref[d, 0]
        a_im = ai_ref[d, 0]

        def chunk(ci, carry, d=d, reverse=reverse, a_re=a_re, a_im=a_im):
            c = (n_chunks - 1 - ci) if reverse else ci
            r0 = pl.multiple_of(c * rows, rows)
            u = u_ref[0, pl.ds(pl.multiple_of(r0 + off, SUBLANES), rows), :]
            st_scr[...] = jnp.dot(u.astype(BF16), bb_ref[d, 0], preferred_element_type=F32)

            def step(t, carry):
                s_re, s_im = carry
                tt = (tc - 1 - t) if reverse else t
                rr = pl.ds(pl.multiple_of(tt * SUBLANES, SUBLANES), SUBLANES)
                n_re = a_re * s_re - a_im * s_im + st_scr[rr, :S5_ST]
                n_im = a_re * s_im + a_im * s_re + st_scr[rr, S5_ST:]
                st_scr[rr, :S5_ST] = n_re
                st_scr[rr, S5_ST:] = n_im
                return n_re, n_im

            carry = lax.fori_loop(0, tc, step, carry, unroll=4)
            y = jnp.dot(st_scr[...].astype(BF16), cc_ref[d, 0], preferred_element_type=F32)
            if not reverse:
                y_ref[0, pl.ds(r0, rows), :] = d_ref[...] * u + y
            else:
                y_ref[0, pl.ds(r0, rows), :] = y_ref[0, pl.ds(r0, rows), :] + y
            return carry

        f_re, f_im = lax.fori_loop(0, n_chunks, chunk, (s0r_ref[0, d], s0i_ref[0, d]))
        fr_ref[0, d] = f_re
        fi_ref[0, d] = f_im


def _s5(rec_t, bb, cc, a_re, a_im, d_skip, s0_re, s0_im, seq):
    nbg, rows_p, _ = rec_t.shape
    tc = 64
    n_st = N_CH * S5_ST
    sspec = pl.BlockSpec((1, 2, SUBLANES, S5_ST), lambda g, c: (g, 0, 0, c))
    aspec = pl.BlockSpec((2, 1, SUBLANES, S5_ST), lambda g, c: (0, c, 0, 0))
    return pl.pallas_call(
        functools.partial(_s5_kernel, seq=seq, tc=tc),
        grid=(nbg, N_CH),
        in_specs=[pl.BlockSpec((1, rows_p, CH), lambda g, c: (g, 0, 2 * N_CH + c)),
                  pl.BlockSpec((2, 1, CH, 2 * S5_ST), lambda g, c: (0, c, 0, 0)),
                  pl.BlockSpec((2, 1, 2 * S5_ST, CH), lambda g, c: (0, c, 0, 0)),
                  aspec, aspec,
                  pl.BlockSpec((1, CH), lambda g, c: (0, c)),
                  sspec, sspec],
        out_specs=[pl.BlockSpec((1, seq * SUBLANES, CH), lambda g, c: (g, 0, c)), sspec, sspec],
        out_shape=[jax.ShapeDtypeStruct((nbg, seq * SUBLANES, D_S5), F32),
                   jax.ShapeDtypeStruct((nbg, 2, SUBLANES, n_st), F32),
                   jax.ShapeDtypeStruct((nbg, 2, SUBLANES, n_st), F32)],
        scratch_shapes=[pltpu.VMEM((tc * SUBLANES, 2 * S5_ST), F32)],
        compiler_params=_cparams(("parallel", "parallel"), 48),
        name="s5",
    )(rec_t, bb, cc, a_re, a_im, d_skip, s0_re, s0_im)


def _s5_params(lam_re, lam_im, log_dt, b_re, b_im, c_re, c_im):
    dt = jnp.exp(log_dt)[..., None]
    mag = jnp.exp(lam_re * dt)
    abar_re, abar_im = mag * jnp.cos(lam_im * dt), mag * jnp.sin(lam_im * dt)
    den = lam_re * lam_re + lam_im * lam_im
    nr, ni = abar_re - 1, abar_im
    cr = (nr * lam_re + ni * lam_im) / den
    ci = (ni * lam_re - nr * lam_im) / den
    bb_re = cr[..., None] * b_re - ci[..., None] * b_im
    bb_im = cr[..., None] * b_im + ci[..., None] * b_re
    gpc = CH // S5_GROUP
    eye = jnp.eye(gpc, dtype=F32)

    def in_map(b):
        b = b.reshape(2, N_CH, gpc, S5_STATE, S5_GROUP).transpose(0, 1, 2, 4, 3)
        return (b[:, :, :, :, None, :] * eye[None, None, :, None, :, None]).reshape(2, N_CH, CH, S5_ST)

    def out_map(c):
        c = c.reshape(2, N_CH, gpc, S5_GROUP, S5_STATE).transpose(0, 1, 2, 4, 3)
        return (c[:, :, :, :, None, :] * eye[None, None, :, None, :, None]).reshape(2, N_CH, S5_ST, CH)

    bb = jnp.concatenate([in_map(bb_re), in_map(bb_im)], axis=-1).astype(BF16)
    cc = jnp.concatenate([out_map(c_re), -out_map(c_im)], axis=-2).astype(BF16)

    def bcast(a):
        return jnp.broadcast_to(a.reshape(2, N_CH, 1, S5_ST), (2, N_CH, SUBLANES, S5_ST))

    return bb, cc, bcast(abar_re), bcast(abar_im)


def _out_proj_kernel(x_ref, oa_ref, ob_ref, y_ref, wg_ref, bg_ref, wo_ref, mod_ref, g_ref, *refs, route):
    if route:
        r_ref, x1_ref, h2_ref, route_ref = refs
    else:
        x1_ref, h2_ref = refs
    z = jax.nn.gelu(y_ref[...])
    gl = jnp.dot(z.astype(BF16), wg_ref[...], preferred_element_type=F32) + bg_ref[...]
    oc = z * jax.nn.sigmoid(gl)
    mix = (jnp.dot(oa_ref[...], wo_ref[:D_ATTN, :], preferred_element_type=F32)
           + jnp.dot(ob_ref[...].astype(BF16), wo_ref[D_ATTN:D_ATTN + D_LRU, :], preferred_element_type=F32)
           + jnp.dot(oc.astype(BF16), wo_ref[D_ATTN + D_LRU:, :], preferred_element_type=F32))
    x1 = x_ref[...] + mod_ref[0, 2:3, :] * mix
    x1_ref[...] = x1
    h2 = _rms_mod(x1, g_ref[...], mod_ref[0, 3:4, :], mod_ref[0, 4:5, :])
    h2_ref[...] = h2.astype(h2_ref.dtype)
    if route:
        logits = jnp.dot(h2, r_ref[...], preferred_element_type=F32, precision=lax.Precision.HIGHEST)
        lane = lax.broadcasted_iota(jnp.int32, logits.shape, 1)
        lg = jnp.where(lane < N_EXPERTS, logits, -jnp.inf)
        m1 = jnp.max(lg, axis=-1, keepdims=True)
        i1 = jnp.min(jnp.where(lg == m1, lane, LANES), axis=-1, keepdims=True)
        lg2 = jnp.where(lane == i1, -jnp.inf, lg)
        m2 = jnp.max(lg2, axis=-1, keepdims=True)
        i2 = jnp.min(jnp.where(lg2 == m2, lane, LANES), axis=-1, keepdims=True)
        e2 = jnp.exp(m2 - m1)
        den = 1.0 + e2
        route_ref[...] = (jnp.where(lane == 0, i1.astype(F32), 0.0) + jnp.where(lane == 1, i2.astype(F32), 0.0)
                          + jnp.where(lane == 2, 1.0 / den, 0.0) + jnp.where(lane == 3, e2 / den, 0.0))


def _out_proj(x2d, oa, ob, y, w_glu_bf, b_glu, w_out_bf, mod, mod_base, rows_per_mod, g, router_pad):
    r, d = x2d.shape
    tm = 512
    tiles_per_mod = rows_per_mod // tm
    route = router_pad is not None
    row = lambda n: pl.BlockSpec((tm, n), lambda i: (i, 0))
    full = lambda a: pl.BlockSpec(a.shape, lambda i: (0,) * a.ndim, pipeline_mode=pl.Buffered(1))
    args = [x2d, oa, ob, y, w_glu_bf, b_glu, w_out_bf, mod, g]
    in_specs = [row(d), row(D_ATTN), row(D_LRU), row(D_S5), full(w_glu_bf), full(b_glu), full(w_out_bf),
                pl.BlockSpec((1, N_MOD, d), lambda i: (mod_base + i // tiles_per_mod, 0, 0)), full(g)]
    out_shape = [jax.ShapeDtypeStruct((r, d), F32), jax.ShapeDtypeStruct((r, d), F32 if route else BF16)]
    out_specs = [row(d), row(d)]
    if route:
        args.append(router_pad)
        in_specs.append(full(router_pad))
        out_shape.append(jax.ShapeDtypeStruct((r, LANES), F32))
        out_specs.append(row(LANES))
    return pl.pallas_call(
        functools.partial(_out_proj_kernel, route=route),
        grid=(r // tm,),
        in_specs=in_specs,
        out_specs=out_specs,
        out_shape=out_shape,
        compiler_params=_cparams(("parallel",), 56),
        name="out_proj",
    )(*args)


def _swiglu_part(h, w1, w3, w2):
    a = jnp.dot(h, w1, preferred_element_type=F32)
    b = jnp.dot(h, w3, preferred_element_type=F32)
    return jnp.dot((jax.nn.silu(a) * b).astype(BF16), w2, preferred_element_type=F32)


def _ffn_kernel(h_ref, x_ref, mod_ref, w1_ref, w3_ref, w2_ref, o_ref):
    f = pl.program_id(1)
    part = _swiglu_part(h_ref[...], w1_ref[...], w3_ref[...], w2_ref[...])

    @pl.when(f == 0)
    def _():
        o_ref[...] = part

    @pl.when(f > 0)
    def _():
        o_ref[...] += part

    @pl.when(f == pl.num_programs(1) - 1)
    def _():
        o_ref[...] = x_ref[...] + mod_ref[0, 5:6, :] * o_ref[...]


def _ffn(h2, x1, mod, mod_base, rows_per_mod, w1, w3, w2):
    r, d = x1.shape
    tm, tf = 1024, 512
    tiles_per_mod = rows_per_mod // tm
    return pl.pallas_call(
        _ffn_kernel,
        grid=(r // tm, w1.shape[1] // tf),
        in_specs=[pl.BlockSpec((tm, d), lambda i, f: (i, 0)),
                  pl.BlockSpec((tm, d), lambda i, f: (i, 0), pipeline_mode=pl.Buffered(1)),
                  pl.BlockSpec((1, N_MOD, d), lambda i, f: (mod_base + i // tiles_per_mod, 0, 0)),
                  pl.BlockSpec((d, tf), lambda i, f: (0, f)),
                  pl.BlockSpec((d, tf), lambda i, f: (0, f)),
                  pl.BlockSpec((tf, d), lambda i, f: (f, 0))],
        out_specs=pl.BlockSpec((tm, d), lambda i, f: (i, 0)),
        out_shape=jax.ShapeDtypeStruct((r, d), F32),
        compiler_params=_cparams(("parallel", "arbitrary"), 60),
        name="ffn",
    )(h2, x1, mod, w1, w3, w2)


MOE_TM = 512
MOE_TF = 256
MOE_TT = 256


def _route_plan(route):
    t = route.shape[0]
    n_pairs = 2 * t
    n_tiles = n_pairs // MOE_TM + N_EXPERTS
    experts = route[:, :2].astype(jnp.int32).reshape(n_pairs)
    gates = route[:, 2:4].reshape(n_pairs)
    onehot = (experts[:, None] == jnp.arange(N_EXPERTS, dtype=jnp.int32)[None]).astype(jnp.int32)
    csum = jnp.cumsum(onehot, axis=0)
    rank = jnp.sum(onehot * csum, axis=1) - 1
    counts = csum[-1]
    padded = (counts + MOE_TM - 1) // MOE_TM * MOE_TM
    ends = jnp.cumsum(padded)
    pos = jnp.sum(onehot * (ends - padded)[None], axis=1) + rank
    src = jnp.zeros((n_tiles * MOE_TM,), jnp.int32).at[pos].set(jnp.arange(n_pairs, dtype=jnp.int32) // 2)
    gate_rows = jnp.zeros((n_tiles * MOE_TM,), F32).at[pos].set(gates)
    tile_start = jnp.arange(n_tiles, dtype=jnp.int32) * MOE_TM
    tile_valid = (tile_start < ends[-1]).astype(jnp.int32)
    tile_expert = jnp.sum((tile_start[:, None] >= ends[None]).astype(jnp.int32), axis=1)
    last_expert = jnp.max(jnp.where(counts > 0, jnp.arange(N_EXPERTS, dtype=jnp.int32), 0))
    tile_expert = jnp.where(tile_valid == 1, tile_expert, last_expert)
    return pos, src, gate_rows[:, None], tile_expert, tile_valid


def _gather_rows(idx_of_row, n_rows, src_hbm, dst, sem):
    def row(r, carry):
        pltpu.make_async_copy(src_hbm.at[pl.ds(idx_of_row(r), 1)], dst.at[pl.ds(r, 1)], sem).start()
        return carry
    lax.fori_loop(0, n_rows, row, 0, unroll=8)


def _wait_rows(n_rows, src_hbm, dst, sem):
    pltpu.make_async_copy(src_hbm.at[pl.ds(0, n_rows)], dst, sem).wait()


def _moe_experts_kernel(src_ref, texp_ref, valid_ref, h_hbm, w1_ref, w3_ref, w2_ref, g_ref, o_ref,
                        xbuf, xbf, sem):
    i = pl.program_id(0)
    f = pl.program_id(1)
    slot = i % 2

    def gather(tile, slot):
        _gather_rows(lambda r: src_ref[tile * MOE_TM + r], MOE_TM, h_hbm, xbuf.at[slot], sem.at[slot])

    @pl.when(f == 0)
    def _():
        @pl.when(i == 0)
        def _():
            gather(0, 0)

        _wait_rows(MOE_TM, h_hbm, xbuf.at[slot], sem.at[slot])

        @pl.when(i + 1 < pl.num_programs(0))
        def _():
            gather(i + 1, 1 - slot)

        xbf[...] = xbuf[slot].astype(BF16)

    valid = valid_ref[i] == 1

    @pl.when(valid)
    def _():
        part = _swiglu_part(xbf[...], w1_ref[0], w3_ref[0], w2_ref[0])

        @pl.when(f == 0)
        def _():
            o_ref[...] = part

        @pl.when(f > 0)
        def _():
            o_ref[...] += part

        @pl.when(f == pl.num_programs(1) - 1)
        def _():
            o_ref[...] = g_ref[...] * o_ref[...]

    @pl.when(jnp.logical_not(valid) & (f == 0))
    def _():
        o_ref[...] = jnp.zeros_like(o_ref)


def _moe_experts(h2, src, gate_rows, tile_expert, tile_valid, w1, w3, w2):
    t, d = h2.shape
    n_tiles = tile_expert.shape[0]
    f_e = w1.shape[2]
    n_f = f_e // MOE_TF

    def wmap(i, f, src, texp, valid):
        return (texp[i], 0, jnp.where(valid[i] == 1, f, n_f - 1))

    def w2map(i, f, src, texp, valid):
        return (texp[i], jnp.where(valid[i] == 1, f, n_f - 1), 0)

    grid_spec = pltpu.PrefetchScalarGridSpec(
        num_scalar_prefetch=3,
        grid=(n_tiles, n_f),
        in_specs=[pl.BlockSpec(memory_space=pl.ANY),
                  pl.BlockSpec((1, d, MOE_TF), wmap),
                  pl.BlockSpec((1, d, MOE_TF), wmap),
                  pl.BlockSpec((1, MOE_TF, d), w2map),
                  pl.BlockSpec((MOE_TM, 1), lambda i, f, *_: (i, 0))],
        out_specs=pl.BlockSpec((MOE_TM, d), lambda i, f, *_: (i, 0)),
        scratch_shapes=[pltpu.VMEM((2, MOE_TM, d), F32), pltpu.VMEM((MOE_TM, d), BF16),
                        pltpu.SemaphoreType.DMA((2,))])
    return pl.pallas_call(
        _moe_experts_kernel,
        grid_spec=grid_spec,
        out_shape=jax.ShapeDtypeStruct((n_tiles * MOE_TM, d), F32),
        compiler_params=_cparams(("arbitrary", "arbitrary"), 48),
        name="moe_experts",
    )(src, tile_expert, tile_valid, h2, w1, w3, w2, gate_rows)


def _moe_combine_kernel(pos_ref, x_ref, mod_ref, y_hbm, o_ref, buf, sem, *, tok_base):
    i = pl.program_id(0)
    slot = i % 2

    def gather(tile, slot):
        for k in range(2):
            _gather_rows(lambda r, k=k: pos_ref[2 * (tok_base + tile * MOE_TT + r) + k], MOE_TT, y_hbm,
                         buf.at[slot, k], sem.at[slot])

    @pl.when(i == 0)
    def _():
        gather(0, 0)

    for k in range(2):
        _wait_rows(MOE_TT, y_hbm, buf.at[slot, k], sem.at[slot])

    @pl.when(i + 1 < pl.num_programs(0))
    def _():
        gather(i + 1, 1 - slot)

    o_ref[...] = x_ref[...] + mod_ref[0, 5:6, :] * (buf[slot, 0] + buf[slot, 1])


def _moe_combine(pos, x1, mod, mod_base, rows_per_mod, y_rows, tok_base):
    r, d = x1.shape
    tiles_per_mod = rows_per_mod // MOE_TT
    grid_spec = pltpu.PrefetchScalarGridSpec(
        num_scalar_prefetch=1,
        grid=(r // MOE_TT,),
        in_specs=[pl.BlockSpec((MOE_TT, d), lambda i, *_: (i, 0)),
                  pl.BlockSpec((1, N_MOD, d), lambda i, *_: (mod_base + i // tiles_per_mod, 0, 0)),
                  pl.BlockSpec(memory_space=pl.ANY)],
        out_specs=pl.BlockSpec((MOE_TT, d), lambda i, *_: (i, 0)),
        scratch_shapes=[pltpu.VMEM((2, 2, MOE_TT, d), F32), pltpu.SemaphoreType.DMA((2,))])
    return pl.pallas_call(
        functools.partial(_moe_combine_kernel, tok_base=tok_base),
        grid_spec=grid_spec,
        out_shape=jax.ShapeDtypeStruct((r, d), F32),
        compiler_params=_cparams(("arbitrary",), 32),
        name="moe_combine",
    )(pos, x1, mod, y_rows)


def _time_major(rec, batch, seq):
    c = rec.shape[-1]
    nbg = -(-batch // SUBLANES)
    x = rec.reshape(batch, seq, c)
    x = jnp.pad(x, ((0, nbg * SUBLANES - batch), (T_FRONT, T_BACK), (0, 0)))
    x = x.reshape(nbg, SUBLANES, seq + CONV_W - 1, c).transpose(0, 2, 1, 3)
    return x.reshape(nbg, (seq + CONV_W - 1) * SUBLANES, c)


def _batch_major(y_t, batch, seq):
    nbg = y_t.shape[0]
    c = y_t.shape[-1]
    y = y_t.reshape(nbg, seq, SUBLANES, c).transpose(0, 2, 1, 3).reshape(nbg * SUBLANES, seq, c)
    return y[:batch].reshape(batch * seq, c)


def _group_states(s, batch):
    nbg = -(-batch // SUBLANES)
    s = jnp.pad(s, ((0, nbg * SUBLANES - batch), (0, 0), (0, 0)))
    return s.reshape(nbg, SUBLANES, 2, -1).transpose(0, 2, 1, 3)


def _ungroup_states(s, batch):
    nbg = s.shape[0]
    return s.transpose(0, 2, 1, 3).reshape(nbg * SUBLANES, 2, -1)[:batch]


def _token_mixer(x2d, batch, seq, mod, mod_base, rows_per_mod, p, kv_ctx, lru_h0, s5_h0):
    is_ctx = kv_ctx is None
    outs = _in_proj(x2d, p['g_mix'], mod, mod_base, rows_per_mod, p['w_in'], p['q_g'], p['k_g'], p['ones_bd'],
                    seq, is_ctx)
    q, k, v, rec = outs[:4]
    if is_ctx:
        oa = _ctx_attn(q, k, v, seq)
        lru_h0 = jnp.zeros((batch, 2, D_LRU), F32)
        s5_h0 = (jnp.zeros((batch, 2, N_S5_GROUPS * S5_STATE), F32),) * 2
    else:
        oa = _nbr_attn(q, k, v, kv_ctx[0], kv_ctx[1], kv_ctx[2], p['tb'], seq)
    rec_t = _time_major(rec, batch, seq)
    ob_t, lru_fin = _rglru(rec_t, p['conv_w'], p['conv_b'], p['lru_w'], p['lru_b'], p['lru_lam'],
                           _group_states(lru_h0, batch), seq)
    y_t, fin_re, fin_im = _s5(rec_t, p['s5_bb'], p['s5_cc'], p['s5_are'], p['s5_aim'], p['d_skip'],
                              _group_states(s5_h0[0], batch), _group_states(s5_h0[1], batch), seq)
    ob = _batch_major(ob_t, batch, seq)
    y = _batch_major(y_t, batch, seq)
    res = _out_proj(x2d, oa, ob, y, p['w_glu'], p['b_glu'], p['w_out'], mod, mod_base, rows_per_mod,
                    p['g_ffn'], p.get('router'))
    if not is_ctx:
        return res, None
    state = (outs[4], outs[5], _ungroup_states(lru_fin, batch),
             _ungroup_states(fin_re, batch).reshape(batch, 2, N_S5_GROUPS, S5_STATE),
             _ungroup_states(fin_im, batch).reshape(batch, 2, N_S5_GROUPS, S5_STATE))
    return res, state


def kernel(x_prompt, x_sample, c, cache_k, cache_v, state_lru, state_s5_re, state_s5_im, c_ctx, norm_mix_g, norm_ffn_g, w_mod, b_mod, w_in, w_out, q_norm_g, k_norm_g, rpb, lru_conv_w, lru_conv_b, lru_wa, lru_ba, lru_wi, lru_bi, lru_lam, s5_lam_re, s5_lam_im, s5_log_dt, s5_b_re, s5_b_im, s5_c_re, s5_c_im, s5_d, s5_w_glu, s5_b_glu, ffn_w1, ffn_w3, ffn_w2, moe_router, moe_w1, moe_w3, moe_w2):
    batch, seq, d = x_prompt.shape
    dec_batch, dec_seq, _ = x_sample.shape
    depth = w_in.shape[0]
    assert dec_batch + 1 <= SUBLANES

    cvecs = jnp.concatenate([c_ctx[None], c, jnp.zeros((SUBLANES - 1 - dec_batch, d), F32)], axis=0)
    mods = _adaln(cvecs, w_mod, b_mod).reshape(depth, SUBLANES, N_MOD, d)

    heads_per_tile = 512 // HEAD_DIM
    ones_bd = jnp.asarray(np.kron(np.eye(heads_per_tile), np.ones((HEAD_DIM, HEAD_DIM))), BF16)

    xp = x_prompt.reshape(batch * seq, d)
    xs = x_sample.reshape(dec_batch * dec_seq, d)
    ks, vs, lrus, s5rs, s5is = [], [], [], [], []
    for l in range(depth):
        lru_w, lru_b, lru_lam_g = _rglru_params(lru_wa[l], lru_ba[l], lru_wi[l], lru_bi[l], lru_lam[l])
        s5_bb, s5_cc, s5_are, s5_aim = _s5_params(s5_lam_re[l], s5_lam_im[l], s5_log_dt[l], s5_b_re[l],
                                                  s5_b_im[l], s5_c_re[l], s5_c_im[l])
        p = {
            'g_mix': norm_mix_g[l][None], 'g_ffn': norm_ffn_g[l][None],
            'w_in': w_in[l].astype(BF16), 'w_out': w_out[l].astype(BF16),
            'q_g': jnp.tile(q_norm_g[l], heads_per_tile)[None], 'k_g': jnp.tile(k_norm_g[l], heads_per_tile)[None],
            'ones_bd': ones_bd, 'tb': _bias_table(rpb[l]),
            'conv_w': lru_conv_w[l], 'conv_b': lru_conv_b[l][None],
            'lru_w': lru_w, 'lru_b': lru_b, 'lru_lam': lru_lam_g,
            's5_bb': s5_bb, 's5_cc': s5_cc, 's5_are': s5_are, 's5_aim': s5_aim,
            'd_skip': s5_d[l][None], 'w_glu': s5_w_glu[l].astype(BF16), 'b_glu': s5_b_glu[l][None],
        }
        j = l // 2
        dense = l % 2 == 0
        if not dense:
            p['router'] = jnp.pad(moe_router[j], ((0, 0), (0, LANES - N_EXPERTS)))
        res_p, (k_c, v_c, lru_c, s5r_c, s5i_c) = _token_mixer(
            xp, batch, seq, mods[l], 0, batch * seq, p, None, None, None)
        ks.append(k_c)
        vs.append(v_c)
        lrus.append(lru_c)
        s5rs.append(s5r_c)
        s5is.append(s5i_c)
        res_s, _ = _token_mixer(
            xs, dec_batch, dec_seq, mods[l], 1, dec_seq, p, (cache_k, cache_v, l), state_lru[:, l],
            (state_s5_re[:, l].reshape(dec_batch, 2, -1), state_s5_im[:, l].reshape(dec_batch, 2, -1)))
        if dense:
            w = (ffn_w1[j].astype(BF16), ffn_w3[j].astype(BF16), ffn_w2[j].astype(BF16))
            xp = _ffn(res_p[1], res_p[0], mods[l], 0, batch * seq, *w)
            xs = _ffn(res_s[1], res_s[0], mods[l], 1, dec_seq, *w)
        else:
            h2 = jnp.concatenate([res_p[1], res_s[1]], axis=0)
            pos, src, gate_rows, tile_expert, tile_valid = _route_plan(jnp.concatenate([res_p[2], res_s[2]], axis=0))
            y_rows = _moe_experts(h2, src, gate_rows, tile_expert, tile_valid,
                                  moe_w1[j].astype(BF16), moe_w3[j].astype(BF16), moe_w2[j].astype(BF16))
            xp = _moe_combine(pos, res_p[0], mods[l], 0, batch * seq, y_rows, 0)
            xs = _moe_combine(pos, res_s[0], mods[l], 1, dec_seq, y_rows, batch * seq)
    return (xp.reshape(batch, seq, d), xs.reshape(dec_batch, dec_seq, d),
            jnp.stack(ks, axis=1), jnp.stack(vs, axis=1), jnp.stack(lrus, axis=1),
            jnp.stack(s5rs, axis=1), jnp.stack(s5is, axis=1))
```

```python
import functools

import numpy as np
import jax
import jax.numpy as jnp
from jax import lax
from jax.experimental import pallas as pl
from jax.experimental.pallas import tpu as pltpu

F32 = jnp.float32
BF16 = jnp.bfloat16

D_MODEL = 2048
N_HEADS = 16
HEAD_DIM = 64
D_ATTN = N_HEADS * HEAD_DIM
GRID_W = 64
WIN_R = 8
WIN_C = 16
D_LRU = 512
LRU_BLOCK = 64
CONV_W = 4
LRU_C = 8.0
D_S5 = 512
S5_GROUP = 16
N_S5_GROUPS = 32
S5_STATE = 64
D_REC = 2 * D_LRU + D_S5
D_IN = 3 * D_ATTN + D_REC
N_MOD = 6
N_EXPERTS = 8
EPS = 1e-6
NEG = -1e30

LANES = 128
SUBLANES = 8
MIB = 1024 * 1024

CH = 128
N_CH = D_LRU // CH
S5_ST = (CH // S5_GROUP) * S5_STATE
SEQ_PER_GROUP = SUBLANES // 2
T_PAD = CONV_W // 2


def _cparams(sem, vmem_mib):
    return pltpu.CompilerParams(dimension_semantics=sem, vmem_limit_bytes=vmem_mib * MIB)


def _adaln_kernel(c_ref, w_ref, b_ref, o_ref):
    s = jax.nn.silu(c_ref[...]).astype(BF16)
    o_ref[0] = jnp.dot(s, w_ref[0].astype(BF16), preferred_element_type=F32) + b_ref[0]


def _adaln(cvecs, w_mod, b_mod):
    depth, d, n = w_mod.shape
    tn = 1024
    return pl.pallas_call(
        _adaln_kernel,
        grid=(depth, n // tn),
        in_specs=[pl.BlockSpec((SUBLANES, d), lambda l, j: (0, 0)),
                  pl.BlockSpec((1, d, tn), lambda l, j: (l, 0, j)),
                  pl.BlockSpec((1, 1, tn), lambda l, j: (l, 0, j))],
        out_specs=pl.BlockSpec((1, SUBLANES, tn), lambda l, j: (l, 0, j)),
        out_shape=jax.ShapeDtypeStruct((depth, SUBLANES, n), F32),
        compiler_params=_cparams(("parallel", "parallel"), 40),
        name="adaln",
    )(cvecs, w_mod, b_mod.reshape(depth, 1, n))


def _rms_mod(x, g, shift, scale):
    xf = x * lax.rsqrt(jnp.mean(x * x, axis=-1, keepdims=True) + EPS)
    return (xf * g) * (1 + scale) + shift


def _in_proj_kernel(x_ref, g_ref, mod_ref, w_ref, qg_ref, kg_ref, ones_ref, *refs, seq, write_cache):
    if write_cache:
        q_ref, k_ref, v_ref, rec_ref, kc_ref, vc_ref, h_scr = refs
    else:
        q_ref, k_ref, v_ref, rec_ref, h_scr = refs
    j = pl.program_id(1)
    tm, tn = q_ref.shape
    heads = tn // HEAD_DIM

    @pl.when(j == 0)
    def _():
        h_scr[...] = _rms_mod(x_ref[...], g_ref[...], mod_ref[0, 0:1, :], mod_ref[0, 1:2, :]).astype(BF16)

    y = jnp.dot(h_scr[...], w_ref[...], preferred_element_type=F32)

    def to_cache(c_ref, val):
        for b in range(tm // seq):
            for h in range(heads):
                c_ref[b, h, :, :] = val[b * seq:(b + 1) * seq, h * HEAD_DIM:(h + 1) * HEAD_DIM]

    @pl.when(j < 4)
    def _():
        y2 = y * y
        hi = y2.astype(BF16)
        lo = (y2 - hi.astype(F32)).astype(BF16)
        ss = (jnp.dot(hi, ones_ref[...], preferred_element_type=F32)
              + jnp.dot(lo, ones_ref[...], preferred_element_type=F32))
        gain = jnp.where(j < 2, qg_ref[...], kg_ref[...])
        yn = (y * lax.rsqrt(ss * (1.0 / HEAD_DIM) + EPS)) * gain

        @pl.when(j < 2)
        def _():
            q_ref[...] = yn.astype(BF16)

        @pl.when(j >= 2)
        def _():
            k_ref[...] = yn.astype(BF16)
            if write_cache:
                to_cache(kc_ref, yn)

    @pl.when((j >= 4) & (j < 6))
    def _():
        v_ref[...] = y.astype(BF16)
        if write_cache:
            to_cache(vc_ref, y)

    @pl.when(j >= 6)
    def _():
        rec_ref[...] = y


def _in_proj(x2d, g, mod, mod_base, rows_per_mod, w_bf, qg, kg, ones_bd, seq, write_cache):
    r, d = x2d.shape
    tm, tn = 512, 512
    n_i = r // tm
    tiles_per_mod = rows_per_mod // tm
    bpt = tm // seq
    hpt = tn // HEAD_DIM

    def col(lo, n):
        return lambda i, j: (i, jnp.clip(j - lo, 0, n - 1))

    out_shape = [jax.ShapeDtypeStruct((r, D_ATTN), BF16)] * 3 + [jax.ShapeDtypeStruct((r, D_REC), F32)]
    out_specs = [pl.BlockSpec((tm, tn), col(0, 2)), pl.BlockSpec((tm, tn), col(2, 2)),
                 pl.BlockSpec((tm, tn), col(4, 2)), pl.BlockSpec((tm, tn), col(6, 3))]
    if write_cache:
        cshape = jax.ShapeDtypeStruct((r // seq, N_HEADS, seq, HEAD_DIM), F32)
        out_shape += [cshape, cshape]
        out_specs += [pl.BlockSpec((bpt, hpt, seq, HEAD_DIM), lambda i, j: (i, jnp.clip(j - 2, 0, 1), 0, 0)),
                      pl.BlockSpec((bpt, hpt, seq, HEAD_DIM), lambda i, j: (i, jnp.clip(j - 4, 0, 1), 0, 0))]
    return pl.pallas_call(
        functools.partial(_in_proj_kernel, seq=seq, write_cache=write_cache),
        grid=(n_i, D_IN // tn),
        in_specs=[pl.BlockSpec((tm, d), lambda i, j: (i, 0)),
                  pl.BlockSpec((1, d), lambda i, j: (0, 0)),
                  pl.BlockSpec((1, N_MOD, d), lambda i, j: (mod_base + i // tiles_per_mod, 0, 0)),
                  pl.BlockSpec((d, tn), lambda i, j: (0, j)),
                  pl.BlockSpec((1, tn), lambda i, j: (0, 0)),
                  pl.BlockSpec((1, tn), lambda i, j: (0, 0)),
                  pl.BlockSpec((tn, tn), lambda i, j: (0, 0))],
        out_specs=out_specs,
        out_shape=out_shape,
        scratch_shapes=[pltpu.VMEM((tm, d), BF16)],
        compiler_params=_cparams(("parallel", "arbitrary"), 52),
        name="in_proj",
    )(x2d, g, mod, w_bf, qg, kg, ones_bd)


def _softmax_rows(parts):
    m = functools.reduce(jnp.maximum, [jnp.max(s, axis=-1, keepdims=True) for s in parts])
    es = [jnp.exp(s - m) for s in parts]
    den = functools.reduce(jnp.add, [jnp.sum(e, axis=-1, keepdims=True) for e in es])
    inv = 1.0 / den
    return [(e * inv).astype(BF16) for e in es]


def _qk(q, k):
    return lax.dot_general(q, k, (((1,), (1,)), ((), ())), preferred_element_type=F32)


def _ctx_attn_kernel(q_ref, k_ref, v_ref, o_ref):
    scale = HEAD_DIM ** -0.5
    for h in range(N_HEADS):
        sl = slice(h * HEAD_DIM, (h + 1) * HEAD_DIM)
        (p,) = _softmax_rows([_qk(q_ref[:, sl], k_ref[:, sl]) * scale])
        o_ref[:, sl] = jnp.dot(p, v_ref[:, sl], preferred_element_type=F32).astype(BF16)


def _ctx_attn(q, k, v, seq):
    r = q.shape[0]
    spec = pl.BlockSpec((seq, D_ATTN), lambda b: (b, 0))
    return pl.pallas_call(
        _ctx_attn_kernel,
        grid=(r // seq,),
        in_specs=[spec, spec, spec],
        out_specs=spec,
        out_shape=jax.ShapeDtypeStruct((r, D_ATTN), BF16),
        compiler_params=_cparams(("parallel",), 32),
        name="ctx_attn",
    )(q, k, v)


def _nbr_plan(seq):
    rows = seq // GRID_W
    wr = min(WIN_R, rows)
    row_start = np.clip(np.arange(rows) - wr // 2, 0, rows - wr)
    rows_per_blk = 256 // GRID_W
    ranges = []
    for qb in range(rows // rows_per_blk):
        rs = row_start[qb * rows_per_blk:(qb + 1) * rows_per_blk]
        lo = int(rs.min()) * GRID_W // LANES * LANES
        hi = -(-(int(rs.max()) + wr) * GRID_W // LANES) * LANES
        ranges.append((lo, hi))
    return rows, wr, row_start, ranges


def _nbr_attn_kernel(q_ref, k_ref, v_ref, kc_ref, vc_ref, tb_ref, o_ref, bias_scr, *, seq):
    rows, wr, row_start, ranges = _nbr_plan(seq)
    scale = HEAD_DIM ** -0.5
    heads = q_ref.shape[1] // HEAD_DIM
    neg = jnp.full((GRID_W, GRID_W), NEG, F32)

    @pl.when(pl.program_id(1) == 0)
    def _():
        for h in range(heads):
            for qr in range(rows):
                for kp in range(rows // 2):
                    blks = []
                    for kr in (2 * kp, 2 * kp + 1):
                        inside = row_start[qr] <= kr < row_start[qr] + wr
                        blks.append(tb_ref[h, kr - qr + WIN_R - 1] if inside else neg)
                    bias_scr[h, qr * GRID_W:(qr + 1) * GRID_W, kp * LANES:(kp + 1) * LANES] = (
                        jnp.concatenate(blks, axis=1))

    for h in range(heads):
        sl = slice(h * HEAD_DIM, (h + 1) * HEAD_DIM)
        kc = kc_ref[0, 0, h].astype(BF16)
        vc = vc_ref[0, 0, h].astype(BF16)
        for qb, (lo, hi) in enumerate(ranges):
            qs = slice(qb * 256, (qb + 1) * 256)
            q = q_ref[qs, sl]
            s_loc = _qk(q, k_ref[lo:hi, sl]) * scale + bias_scr[h, qs, lo:hi]
            s_ctx = _qk(q, kc) * scale
            p_loc, p_ctx = _softmax_rows([s_loc, s_ctx])
            o = (jnp.dot(p_loc, v_ref[lo:hi, sl], preferred_element_type=F32)
                 + jnp.dot(p_ctx, vc, preferred_element_type=F32))
            o_ref[qs, sl] = o.astype(BF16)


def _nbr_attn(q, k, v, cache_k, cache_v, layer, tb, seq):
    r = q.shape[0]
    past = cache_k.shape[3]
    hp = LANES // HEAD_DIM
    spec = pl.BlockSpec((seq, LANES), lambda g, b: (b, g))
    cspec = pl.BlockSpec((1, 1, hp, past, HEAD_DIM), lambda g, b: (b, layer, g, 0, 0))
    return pl.pallas_call(
        functools.partial(_nbr_attn_kernel, seq=seq),
        grid=(N_HEADS // hp, r // seq),
        in_specs=[spec, spec, spec, cspec, cspec,
                  pl.BlockSpec((hp,) + tb.shape[1:], lambda g, b: (g, 0, 0, 0))],
        out_specs=spec,
        out_shape=jax.ShapeDtypeStruct((r, D_ATTN), BF16),
        scratch_shapes=[pltpu.VMEM((hp, seq, seq), F32)],
        compiler_params=_cparams(("parallel", "arbitrary"), 48),
        name="nbr_attn",
    )(q, k, v, cache_k, cache_v, tb)


def _bias_table(rpb_l):
    col = np.arange(GRID_W)
    col_start = np.clip(col - WIN_C // 2, 0, GRID_W - WIN_C)
    col_in = (col[None, :] >= col_start[:, None]) & (col[None, :] < col_start[:, None] + WIN_C)
    dc_idx = np.clip(col[None, :] - col[:, None] + WIN_C - 1, 0, 2 * WIN_C - 2)
    tb = rpb_l[:, :, dc_idx]
    return jnp.where(col_in[None, None], tb, NEG).astype(F32)


def _expm1(x):
    u = jnp.exp(x)
    um1 = u - 1.0
    near = um1 * x / jnp.where(u == 1.0, 1.0, jnp.log(u))
    return jnp.where(x < -0.5, um1, jnp.where(u == 1.0, x, near))


def _is_fwd(n_rows, n_cols):
    return (lax.broadcasted_iota(jnp.int32, (n_rows, n_cols), 0) % SUBLANES) < SEQ_PER_GROUP


def _by_direction(x, fwd):
    zero = jnp.zeros_like(x)
    return jnp.concatenate([jnp.where(fwd, x, zero), jnp.where(fwd, zero, x)], axis=1)


def _fold_directions(pk_ref, t, seq):
    a = pk_ref[pl.ds(pl.multiple_of(t * SUBLANES, SUBLANES), SUBLANES), :]
    b = pk_ref[pl.ds(pl.multiple_of((seq - 1 - t) * SUBLANES, SUBLANES), SUBLANES), :]
    return a + pltpu.roll(b, SEQ_PER_GROUP, axis=0)


def _rglru_kernel(xb_ref, gb_ref, cw_ref, cb_ref, w_ref, b_ref, lam_ref, h0_ref, o_ref, fin_ref,
                  a_scr, h_scr, *, seq, tc):
    rows = tc * SUBLANES
    n_chunks = seq // tc
    off = T_PAD * SUBLANES
    sp2 = jax.nn.softplus(-lam_ref[0])

    def chunk(c, h):
        r0 = pl.multiple_of(c * rows, rows)
        fwd = _is_fwd(rows, CH)
        sp = jnp.where(fwd, sp2[:, :CH], sp2[:, CH:])
        bias = jnp.where(_is_fwd(rows, 2 * CH), b_ref[0, :, :2 * CH], b_ref[0, :, 2 * CH:])
        x = [xb_ref[0, pl.ds(pl.multiple_of(r0 + off + k * SUBLANES, SUBLANES), rows), :] for k in range(-2, 3)]
        xc_f = cb_ref[...]
        xc_b = cb_ref[...]
        for j in range(CONV_W):
            xc_f = xc_f + x[j] * cw_ref[j:j + 1, :]
            xc_b = xc_b + x[CONV_W - j] * cw_ref[j:j + 1, :]
        xc = jnp.where(fwd, xc_f, xc_b)
        gates = jnp.dot(_by_direction(xc, fwd).astype(BF16), w_ref[0], preferred_element_type=F32) + bias
        rg = jax.nn.sigmoid(gates[:, :CH])
        ig = jax.nn.sigmoid(gates[:, CH:])
        log_a = -LRU_C * rg * sp
        a_scr[...] = jnp.exp(log_a)
        h_scr[pl.ds(r0, rows), :] = jnp.sqrt(-_expm1(2 * log_a)) * (ig * xc)

        def step(t, h):
            ra = pl.ds(pl.multiple_of(t * SUBLANES, SUBLANES), SUBLANES)
            rh = pl.ds(pl.multiple_of(r0 + t * SUBLANES, SUBLANES), SUBLANES)
            h = a_scr[ra, :] * h + h_scr[rh, :]
            h_scr[rh, :] = h
            return h

        return lax.fori_loop(0, tc, step, h, unroll=8)

    fin_ref[0] = lax.fori_loop(0, n_chunks, chunk, h0_ref[0])

    def gate(t, carry):
        ro = pl.ds(pl.multiple_of(t * SUBLANES, SUBLANES), SUBLANES)
        g = gb_ref[0, pl.ds(pl.multiple_of(off + t * SUBLANES, SUBLANES), SUBLANES), :]
        o_ref[0, ro, :] = _fold_directions(h_scr, t, seq) * jax.nn.gelu(g)
        return carry

    lax.fori_loop(0, seq, gate, 0, unroll=8)


def _rglru(rec_t, conv_w, conv_b, w_g, b_g, lam_g, h0, seq):
    ng, rows_p, _ = rec_t.shape
    tc = 128
    return pl.pallas_call(
        functools.partial(_rglru_kernel, seq=seq, tc=tc),
        grid=(ng, N_CH),
        in_specs=[pl.BlockSpec((1, rows_p, CH), lambda g, c: (g, 0, c)),
                  pl.BlockSpec((1, rows_p, CH), lambda g, c: (g, 0, N_CH + c)),
                  pl.BlockSpec((CONV_W, CH), lambda g, c: (0, c)),
                  pl.BlockSpec((1, CH), lambda g, c: (0, c)),
                  pl.BlockSpec((1, 2 * CH, 2 * CH), lambda g, c: (c, 0, 0)),
                  pl.BlockSpec((1, 1, 4 * CH), lambda g, c: (c, 0, 0)),
                  pl.BlockSpec((1, 1, 2 * CH), lambda g, c: (c, 0, 0)),
                  pl.BlockSpec((1, SUBLANES, CH), lambda g, c: (g, 0, c))],
        out_specs=[pl.BlockSpec((1, seq * SUBLANES, CH), lambda g, c: (g, 0, c)),
                   pl.BlockSpec((1, SUBLANES, CH), lambda g, c: (g, 0, c))],
        out_shape=[jax.ShapeDtypeStruct((ng, seq * SUBLANES, D_LRU), F32),
                   jax.ShapeDtypeStruct((ng, SUBLANES, D_LRU), F32)],
        scratch_shapes=[pltpu.VMEM((tc * SUBLANES, CH), F32), pltpu.VMEM((seq * SUBLANES, CH), F32)],
        compiler_params=_cparams(("parallel", "parallel"), 48),
        name="rglru",
    )(rec_t, rec_t, conv_w, conv_b, w_g, b_g, lam_g, h0)


def _rglru_params(wa, ba, wi, bi, lam):
    bpc = CH // LRU_BLOCK
    eye = jnp.eye(bpc, dtype=F32)

    def dense(w):
        w = w.reshape(N_CH, bpc, LRU_BLOCK, LRU_BLOCK)
        return (w[:, :, :, None, :] * eye[None, :, None, :, None]).reshape(N_CH, CH, CH)

    w_g = jnp.concatenate([jnp.concatenate([dense(wa[d]), dense(wi[d])], axis=-1) for d in range(2)],
                          axis=-2).astype(BF16)
    b_g = jnp.concatenate([v.reshape(N_CH, 1, CH) for v in (ba[0], bi[0], ba[1], bi[1])], axis=-1)
    lam_g = jnp.concatenate([lam[0].reshape(N_CH, 1, CH), lam[1].reshape(N_CH, 1, CH)], axis=-1)
    return w_g, b_g, lam_g


def _s5_kernel(u_ref, bb_ref, cc_ref, ar_ref, ai_ref, d_ref, s0r_ref, s0i_ref, y_ref, fr_ref, fi_ref,
               st_scr, y_scr, *, seq, tc):
    rows = tc * SUBLANES
    n_chunks = seq // tc
    off = T_PAD * SUBLANES
    a_re = ar_ref[0]
    a_im = ai_ref[0]

    def chunk(c, carry):
        r0 = pl.multiple_of(c * rows, rows)
        fwd = _is_fwd(rows, CH)
        u = u_ref[0, pl.ds(pl.multiple_of(r0 + off, SUBLANES), rows), :]
        st_scr[...] = jnp.dot(_by_direction(u, fwd).astype(BF16), bb_ref[0], preferred_element_type=F32)

        def step(t, carry):
            s_re, s_im = carry
            rr = pl.ds(pl.multiple_of(t * SUBLANES, SUBLANES), SUBLANES)
            n_re = a_re * s_re - a_im * s_im + st_scr[rr, :S5_ST]
            n_im = a_re * s_im + a_im * s_re + st_scr[rr, S5_ST:]
            st_scr[rr, :S5_ST] = n_re
            st_scr[rr, S5_ST:] = n_im
            return n_re, n_im

        carry = lax.fori_loop(0, tc, step, carry, unroll=4)
        y2 = jnp.dot(st_scr[...].astype(BF16), cc_ref[0], preferred_element_type=F32)
        y_scr[pl.ds(r0, rows), :] = jnp.where(fwd, y2[:, :CH], y2[:, CH:])
        return carry

    f_re, f_im = lax.fori_loop(0, n_chunks, chunk, (s0r_ref[0], s0i_ref[0]))
    fr_ref[0] = f_re
    fi_ref[0] = f_im

    def fold(t, carry):
        ro = pl.ds(pl.multiple_of(t * SUBLANES, SUBLANES), SUBLANES)
        u = u_ref[0, pl.ds(pl.multiple_of(off + t * SUBLANES, SUBLANES), SUBLANES), :]
        y_ref[0, ro, :] = d_ref[...] * u + _fold_directions(y_scr, t, seq)
        return carry

    lax.fori_loop(0, seq, fold, 0, unroll=8)


def _s5(rec_t, bb, cc, a_re, a_im, d_skip, s0_re, s0_im, seq):
    ng, rows_p, _ = rec_t.shape
    tc = 64
    n_st = N_CH * S5_ST
    sspec = pl.BlockSpec((1, SUBLANES, S5_ST), lambda g, c: (g, 0, c))
    aspec = pl.BlockSpec((1, SUBLANES, S5_ST), lambda g, c: (c, 0, 0))
    return pl.pallas_call(
        functools.partial(_s5_kernel, seq=seq, tc=tc),
        grid=(ng, N_CH),
        in_specs=[pl.BlockSpec((1, rows_p, CH), lambda g, c: (g, 0, 2 * N_CH + c)),
                  pl.BlockSpec((1, 2 * CH, 2 * S5_ST), lambda g, c: (c, 0, 0)),
                  pl.BlockSpec((1, 2 * S5_ST, 2 * CH), lambda g, c: (c, 0, 0)),
                  aspec, aspec,
                  pl.BlockSpec((1, CH), lambda g, c: (0, c)),
                  sspec, sspec],
        out_specs=[pl.BlockSpec((1, seq * SUBLANES, CH), lambda g, c: (g, 0, c)), sspec, sspec],
        out_shape=[jax.ShapeDtypeStruct((ng, seq * SUBLANES, D_S5), F32),
                   jax.ShapeDtypeStruct((ng, SUBLANES, n_st), F32),
                   jax.ShapeDtypeStruct((ng, SUBLANES, n_st), F32)],
        scratch_shapes=[pltpu.VMEM((tc * SUBLANES, 2 * S5_ST), F32), pltpu.VMEM((seq * SUBLANES, CH), F32)],
        compiler_params=_cparams(("parallel", "parallel"), 48),
        name="s5",
    )(rec_t, bb, cc, a_re, a_im, d_skip, s0_re, s0_im)


def _s5_params(lam_re, lam_im, log_dt, b_re, b_im, c_re, c_im):
    dt = jnp.exp(log_dt)[..., None]
    mag = jnp.exp(lam_re * dt)
    abar_re, abar_im = mag * jnp.cos(lam_im * dt), mag * jnp.sin(lam_im * dt)
    den = lam_re * lam_re + lam_im * lam_im
    nr, ni = abar_re - 1, abar_im
    cr = (nr * lam_re + ni * lam_im) / den
    ci = (ni * lam_re - nr * lam_im) / den
    bb_re = cr[..., None] * b_re - ci[..., None] * b_im
    bb_im = cr[..., None] * b_im + ci[..., None] * b_re
    gpc = CH // S5_GROUP
    eye = jnp.eye(gpc, dtype=F32)

    def in_map(b):
        b = b.reshape(2, N_CH, gpc, S5_STATE, S5_GROUP).transpose(0, 1, 2, 4, 3)
        return (b[:, :, :, :, None, :] * eye[None, None, :, None, :, None]).reshape(2, N_CH, CH, S5_ST)

    def out_map(c):
        c = c.reshape(2, N_CH, gpc, S5_GROUP, S5_STATE).transpose(0, 1, 2, 4, 3)
        return (c[:, :, :, :, None, :] * eye[None, None, :, None, :, None]).reshape(2, N_CH, S5_ST, CH)

    bb = jnp.concatenate([in_map(bb_re), in_map(bb_im)], axis=-1).astype(BF16)
    cc = jnp.concatenate([out_map(c_re), -out_map(c_im)], axis=-2).astype(BF16)
    bb = jnp.concatenate([bb[0], bb[1]], axis=-2)
    cc = jnp.concatenate([cc[0], cc[1]], axis=-1)

    def per_row(a):
        a = a.reshape(2, N_CH, 1, S5_ST)
        return jnp.concatenate([jnp.broadcast_to(a[d], (N_CH, SEQ_PER_GROUP, S5_ST)) for d in range(2)], axis=1)

    return bb, cc, per_row(abar_re), per_row(abar_im)


def _out_proj_kernel(x_ref, oa_ref, ob_ref, y_ref, wg_ref, bg_ref, wo_ref, mod_ref, g_ref, *refs, route):
    if route:
        r_ref, x1_ref, h2_ref, route_ref = refs
    else:
        x1_ref, h2_ref = refs
    z = jax.nn.gelu(y_ref[...])
    gl = jnp.dot(z.astype(BF16), wg_ref[...], preferred_element_type=F32) + bg_ref[...]
    oc = z * jax.nn.sigmoid(gl)
    mix = (jnp.dot(oa_ref[...], wo_ref[:D_ATTN, :], preferred_element_type=F32)
           + jnp.dot(ob_ref[...].astype(BF16), wo_ref[D_ATTN:D_ATTN + D_LRU, :], preferred_element_type=F32)
           + jnp.dot(oc.astype(BF16), wo_ref[D_ATTN + D_LRU:, :], preferred_element_type=F32))
    x1 = x_ref[...] + mod_ref[0, 2:3, :] * mix
    x1_ref[...] = x1
    h2 = _rms_mod(x1, g_ref[...], mod_ref[0, 3:4, :], mod_ref[0, 4:5, :])
    h2_ref[...] = h2.astype(h2_ref.dtype)
    if route:
        logits = jnp.dot(h2, r_ref[...], preferred_element_type=F32, precision=lax.Precision.HIGHEST)
        lane = lax.broadcasted_iota(jnp.int32, logits.shape, 1)
        lg = jnp.where(lane < N_EXPERTS, logits, -jnp.inf)
        m1 = jnp.max(lg, axis=-1, keepdims=True)
        i1 = jnp.min(jnp.where(lg == m1, lane, LANES), axis=-1, keepdims=True)
        lg2 = jnp.where(lane == i1, -jnp.inf, lg)
        m2 = jnp.max(lg2, axis=-1, keepdims=True)
        i2 = jnp.min(jnp.where(lg2 == m2, lane, LANES), axis=-1, keepdims=True)
        e2 = jnp.exp(m2 - m1)
        den = 1.0 + e2
        route_ref[...] = (jnp.where(lane == 0, i1.astype(F32), 0.0) + jnp.where(lane == 1, i2.astype(F32), 0.0)
                          + jnp.where(lane == 2, 1.0 / den, 0.0) + jnp.where(lane == 3, e2 / den, 0.0))


def _out_proj(x2d, oa, ob, y, w_glu_bf, b_glu, w_out_bf, mod, mod_base, rows_per_mod, g, router_pad):
    r, d = x2d.shape
    tm = 512
    tiles_per_mod = rows_per_mod // tm
    route = router_pad is not None
    row = lambda n: pl.BlockSpec((tm, n), lambda i: (i, 0))
    full = lambda a: pl.BlockSpec(a.shape, lambda i: (0,) * a.ndim, pipeline_mode=pl.Buffered(1))
    args = [x2d, oa, ob, y, w_glu_bf, b_glu, w_out_bf, mod, g]
    in_specs = [row(d), row(D_ATTN), row(D_LRU), row(D_S5), full(w_glu_bf), full(b_glu), full(w_out_bf),
                pl.BlockSpec((1, N_MOD, d), lambda i: (mod_base + i // tiles_per_mod, 0, 0)), full(g)]
    out_shape = [jax.ShapeDtypeStruct((r, d), F32), jax.ShapeDtypeStruct((r, d), F32 if route else BF16)]
    out_specs = [row(d), row(d)]
    if route:
        args.append(router_pad)
        in_specs.append(full(router_pad))
        out_shape.append(jax.ShapeDtypeStruct((r, LANES), F32))
        out_specs.append(row(LANES))
    return pl.pallas_call(
        functools.partial(_out_proj_kernel, route=route),
        grid=(r // tm,),
        in_specs=in_specs,
        out_specs=out_specs,
        out_shape=out_shape,
        compiler_params=_cparams(("parallel",), 56),
        name="out_proj",
    )(*args)


def _swiglu_part(h, w1, w3, w2):
    a = jnp.dot(h, w1, preferred_element_type=F32)
    b = jnp.dot(h, w3, preferred_element_type=F32)
    return jnp.dot((jax.nn.silu(a) * b).astype(BF16), w2, preferred_element_type=F32)


def _ffn_kernel(h_ref, x_ref, mod_ref, w1_ref, w3_ref, w2_ref, o_ref):
    f = pl.program_id(1)
    part = _swiglu_part(h_ref[...], w1_ref[...], w3_ref[...], w2_ref[...])

    @pl.when(f == 0)
    def _():
        o_ref[...] = part

    @pl.when(f > 0)
    def _():
        o_ref[...] += part

    @pl.when(f == pl.num_programs(1) - 1)
    def _():
        o_ref[...] = x_ref[...] + mod_ref[0, 5:6, :] * o_ref[...]


def _ffn(h2, x1, mod, mod_base, rows_per_mod, w1, w3, w2):
    r, d = x1.shape
    tm, tf = 1024, 512
    tiles_per_mod = rows_per_mod // tm
    return pl.pallas_call(
        _ffn_kernel,
        grid=(r // tm, w1.shape[1] // tf),
        in_specs=[pl.BlockSpec((tm, d), lambda i, f: (i, 0)),
                  pl.BlockSpec((tm, d), lambda i, f: (i, 0), pipeline_mode=pl.Buffered(1)),
                  pl.BlockSpec((1, N_MOD, d), lambda i, f: (mod_base + i // tiles_per_mod, 0, 0)),
                  pl.BlockSpec((d, tf), lambda i, f: (0, f)),
                  pl.BlockSpec((d, tf), lambda i, f: (0, f)),
                  pl.BlockSpec((tf, d), lambda i, f: (f, 0))],
        out_specs=pl.BlockSpec((tm, d), lambda i, f: (i, 0)),
        out_shape=jax.ShapeDtypeStruct((r, d), F32),
        compiler_params=_cparams(("parallel", "arbitrary"), 60),
        name="ffn",
    )(h2, x1, mod, w1, w3, w2)


MOE_TM = 512
MOE_TF = 1408
MOE_TT = 256


def _route_plan(route):
    t = route.shape[0]
    n_pairs = 2 * t
    n_tiles = n_pairs // MOE_TM + N_EXPERTS
    experts = route[:, :2].astype(jnp.int32).reshape(n_pairs)
    gates = route[:, 2:4].reshape(n_pairs)
    onehot = (experts[:, None] == jnp.arange(N_EXPERTS, dtype=jnp.int32)[None]).astype(jnp.int32)
    csum = jnp.cumsum(onehot, axis=0)
    rank = jnp.sum(onehot * csum, axis=1) - 1
    counts = csum[-1]
    padded = (counts + MOE_TM - 1) // MOE_TM * MOE_TM
    ends = jnp.cumsum(padded)
    pos = jnp.sum(onehot * (ends - padded)[None], axis=1) + rank
    src = jnp.zeros((n_tiles * MOE_TM,), jnp.int32).at[pos].set(jnp.arange(n_pairs, dtype=jnp.int32) // 2)
    gate_rows = jnp.zeros((n_tiles * MOE_TM,), F32).at[pos].set(gates)
    tile_start = jnp.arange(n_tiles, dtype=jnp.int32) * MOE_TM
    tile_valid = (tile_start < ends[-1]).astype(jnp.int32)
    tile_expert = jnp.sum((tile_start[:, None] >= ends[None]).astype(jnp.int32), axis=1)
    last_expert = jnp.max(jnp.where(counts > 0, jnp.arange(N_EXPERTS, dtype=jnp.int32), 0))
    tile_expert = jnp.where(tile_valid == 1, tile_expert, last_expert)
    return pos, src, gate_rows[:, None], tile_expert, tile_valid


def _gather_rows(idx_of_row, n_rows, src_hbm, dst, sem):
    def row(r, carry):
        pltpu.make_async_copy(src_hbm.at[pl.ds(idx_of_row(r), 1)], dst.at[pl.ds(r, 1)], sem).start()
        return carry
    lax.fori_loop(0, n_rows, row, 0, unroll=8)


def _wait_rows(n_rows, src_hbm, dst, sem):
    pltpu.make_async_copy(src_hbm.at[pl.ds(0, n_rows)], dst, sem).wait()


def _moe_experts_kernel(src_ref, texp_ref, valid_ref, h_hbm, w1_ref, w3_ref, w2_ref, g_ref, o_ref,
                        xbuf, xbf, sem):
    i = pl.program_id(0)
    f = pl.program_id(1)

    def gather(tile):
        _gather_rows(lambda r: src_ref[tile * MOE_TM + r], MOE_TM, h_hbm, xbuf, sem.at[0])

    @pl.when(f == 0)
    def _():
        @pl.when(i == 0)
        def _():
            gather(0)

        _wait_rows(MOE_TM, h_hbm, xbuf, sem.at[0])
        xbf[...] = xbuf[...].astype(BF16)

        @pl.when(i + 1 < pl.num_programs(0))
        def _():
            gather(i + 1)

    valid = valid_ref[i] == 1

    @pl.when(valid)
    def _():
        part = _swiglu_part(xbf[...], w1_ref[0], w3_ref[0], w2_ref[0])

        @pl.when(f == 0)
        def _():
            o_ref[...] = part

        @pl.when(f > 0)
        def _():
            o_ref[...] += part

        @pl.when(f == pl.num_programs(1) - 1)
        def _():
            o_ref[...] = g_ref[...] * o_ref[...]

    @pl.when(jnp.logical_not(valid) & (f == 0))
    def _():
        o_ref[...] = jnp.zeros_like(o_ref)


def _moe_experts(h2, src, gate_rows, tile_expert, tile_valid, w1, w3, w2):
    t, d = h2.shape
    n_tiles = tile_expert.shape[0]
    f_e = w1.shape[2]
    n_f = f_e // MOE_TF

    def wmap(i, f, src, texp, valid):
        return (texp[i], 0, jnp.where(valid[i] == 1, f, n_f - 1))

    def w2map(i, f, src, texp, valid):
        return (texp[i], jnp.where(valid[i] == 1, f, n_f - 1), 0)

    grid_spec = pltpu.PrefetchScalarGridSpec(
        num_scalar_prefetch=3,
        grid=(n_tiles, n_f),
        in_specs=[pl.BlockSpec(memory_space=pl.ANY),
                  pl.BlockSpec((1, d, MOE_TF), wmap),
                  pl.BlockSpec((1, d, MOE_TF), wmap),
                  pl.BlockSpec((1, MOE_TF, d), w2map),
                  pl.BlockSpec((MOE_TM, 1), lambda i, f, *_: (i, 0))],
        out_specs=pl.BlockSpec((MOE_TM, d), lambda i, f, *_: (i, 0)),
        scratch_shapes=[pltpu.VMEM((MOE_TM, d), F32), pltpu.VMEM((MOE_TM, d), BF16),
                        pltpu.SemaphoreType.DMA((1,))])
    return pl.pallas_call(
        _moe_experts_kernel,
        grid_spec=grid_spec,
        out_shape=jax.ShapeDtypeStruct((n_tiles * MOE_TM, d), F32),
        compiler_params=_cparams(("arbitrary", "arbitrary"), 62),
        name="moe_experts",
    )(src, tile_expert, tile_valid, h2, w1, w3, w2, gate_rows)


def _moe_combine_kernel(pos_ref, x_ref, mod_ref, y_hbm, o_ref, buf, sem, *, tok_base):
    i = pl.program_id(0)
    slot = i % 2

    def gather(tile, slot):
        for k in range(2):
            _gather_rows(lambda r, k=k: pos_ref[2 * (tok_base + tile * MOE_TT + r) + k], MOE_TT, y_hbm,
                         buf.at[slot, k], sem.at[slot])

    @pl.when(i == 0)
    def _():
        gather(0, 0)

    for k in range(2):
        _wait_rows(MOE_TT, y_hbm, buf.at[slot, k], sem.at[slot])

    @pl.when(i + 1 < pl.num_programs(0))
    def _():
        gather(i + 1, 1 - slot)

    o_ref[...] = x_ref[...] + mod_ref[0, 5:6, :] * (buf[slot, 0] + buf[slot, 1])


def _moe_combine(pos, x1, mod, mod_base, rows_per_mod, y_rows, tok_base):
    r, d = x1.shape
    tiles_per_mod = rows_per_mod // MOE_TT
    grid_spec = pltpu.PrefetchScalarGridSpec(
        num_scalar_prefetch=1,
        grid=(r // MOE_TT,),
        in_specs=[pl.BlockSpec((MOE_TT, d), lambda i, *_: (i, 0)),
                  pl.BlockSpec((1, N_MOD, d), lambda i, *_: (mod_base + i // tiles_per_mod, 0, 0)),
                  pl.BlockSpec(memory_space=pl.ANY)],
        out_specs=pl.BlockSpec((MOE_TT, d), lambda i, *_: (i, 0)),
        scratch_shapes=[pltpu.VMEM((2, 2, MOE_TT, d), F32), pltpu.SemaphoreType.DMA((2,))])
    return pl.pallas_call(
        functools.partial(_moe_combine_kernel, tok_base=tok_base),
        grid_spec=grid_spec,
        out_shape=jax.ShapeDtypeStruct((r, d), F32),
        compiler_params=_cparams(("arbitrary",), 32),
        name="moe_combine",
    )(pos, x1, mod, y_rows)


def _time_major(rec, batch, seq):
    c = rec.shape[-1]
    ng = batch // SEQ_PER_GROUP
    x = rec.reshape(ng, SEQ_PER_GROUP, seq, c)
    x = jnp.concatenate([x, jnp.flip(x, axis=2)], axis=1)
    x = jnp.pad(x, ((0, 0), (0, 0), (T_PAD, T_PAD), (0, 0)))
    return x.transpose(0, 2, 1, 3).reshape(ng, (seq + 2 * T_PAD) * SUBLANES, c)


def _batch_major(y_t, batch, seq):
    ng, _, c = y_t.shape
    y = y_t.reshape(ng, seq, SUBLANES, c)[:, :, :SEQ_PER_GROUP].transpose(0, 2, 1, 3)
    return y.reshape(batch * seq, c)


def _group_states(s, batch):
    ng = batch // SEQ_PER_GROUP
    return s.reshape(ng, SEQ_PER_GROUP, 2, -1).transpose(0, 2, 1, 3).reshape(ng, SUBLANES, -1)


def _ungroup_states(s, batch):
    ng = s.shape[0]
    return s.reshape(ng, 2, SEQ_PER_GROUP, -1).transpose(0, 2, 1, 3).reshape(batch, 2, -1)


def _token_mixer(x2d, batch, seq, mod, mod_base, rows_per_mod, p, kv_ctx, lru_h0, s5_h0):
    is_ctx = kv_ctx is None
    outs = _in_proj(x2d, p['g_mix'], mod, mod_base, rows_per_mod, p['w_in'], p['q_g'], p['k_g'], p['ones_bd'],
                    seq, is_ctx)
    q, k, v, rec = outs[:4]
    if is_ctx:
        oa = _ctx_attn(q, k, v, seq)
        lru_h0 = jnp.zeros((batch, 2, D_LRU), F32)
        s5_h0 = (jnp.zeros((batch, 2, N_S5_GROUPS * S5_STATE), F32),) * 2
    else:
        oa = _nbr_attn(q, k, v, kv_ctx[0], kv_ctx[1], kv_ctx[2], p['tb'], seq)
    rec_t = _time_major(rec, batch, seq)
    ob_t, lru_fin = _rglru(rec_t, p['conv_w'], p['conv_b'], p['lru_w'], p['lru_b'], p['lru_lam'],
                           _group_states(lru_h0, batch), seq)
    y_t, fin_re, fin_im = _s5(rec_t, p['s5_bb'], p['s5_cc'], p['s5_are'], p['s5_aim'], p['d_skip'],
                              _group_states(s5_h0[0], batch), _group_states(s5_h0[1], batch), seq)
    ob = _batch_major(ob_t, batch, seq)
    y = _batch_major(y_t, batch, seq)
    res = _out_proj(x2d, oa, ob, y, p['w_glu'], p['b_glu'], p['w_out'], mod, mod_base, rows_per_mod,
                    p['g_ffn'], p.get('router'))
    if not is_ctx:
        return res, None
    state = (outs[4], outs[5], _ungroup_states(lru_fin, batch),
             _ungroup_states(fin_re, batch).reshape(batch, 2, N_S5_GROUPS, S5_STATE),
             _ungroup_states(fin_im, batch).reshape(batch, 2, N_S5_GROUPS, S5_STATE))
    return res, state


def kernel(x_prompt, x_sample, c, cache_k, cache_v, state_lru, state_s5_re, state_s5_im, c_ctx, norm_mix_g, norm_ffn_g, w_mod, b_mod, w_in, w_out, q_norm_g, k_norm_g, rpb, lru_conv_w, lru_conv_b, lru_wa, lru_ba, lru_wi, lru_bi, lru_lam, s5_lam_re, s5_lam_im, s5_log_dt, s5_b_re, s5_b_im, s5_c_re, s5_c_im, s5_d, s5_w_glu, s5_b_glu, ffn_w1, ffn_w3, ffn_w2, moe_router, moe_w1, moe_w3, moe_w2):
    batch, seq, d = x_prompt.shape
    dec_batch, dec_seq, _ = x_sample.shape
    depth = w_in.shape[0]
    assert dec_batch + 1 <= SUBLANES and batch % SEQ_PER_GROUP == 0 and dec_batch % SEQ_PER_GROUP == 0

    cvecs = jnp.concatenate([c_ctx[None], c, jnp.zeros((SUBLANES - 1 - dec_batch, d), F32)], axis=0)
    mods = _adaln(cvecs, w_mod, b_mod).reshape(depth, SUBLANES, N_MOD, d)

    heads_per_tile = 512 // HEAD_DIM
    ones_bd = jnp.asarray(np.kron(np.eye(heads_per_tile), np.ones((HEAD_DIM, HEAD_DIM))), BF16)

    xp = x_prompt.reshape(batch * seq, d)
    xs = x_sample.reshape(dec_batch * dec_seq, d)
    ks, vs, lrus, s5rs, s5is = [], [], [], [], []
    for l in range(depth):
        lru_w, lru_b, lru_lam_g = _rglru_params(lru_wa[l], lru_ba[l], lru_wi[l], lru_bi[l], lru_lam[l])
        s5_bb, s5_cc, s5_are, s5_aim = _s5_params(s5_lam_re[l], s5_lam_im[l], s5_log_dt[l], s5_b_re[l],
                                                  s5_b_im[l], s5_c_re[l], s5_c_im[l])
        p = {
            'g_mix': norm_mix_g[l][None], 'g_ffn': norm_ffn_g[l][None],
            'w_in': w_in[l].astype(BF16), 'w_out': w_out[l].astype(BF16),
            'q_g': jnp.tile(q_norm_g[l], heads_per_tile)[None], 'k_g': jnp.tile(k_norm_g[l], heads_per_tile)[None],
            'ones_bd': ones_bd, 'tb': _bias_table(rpb[l]),
            'conv_w': lru_conv_w[l], 'conv_b': lru_conv_b[l][None],
            'lru_w': lru_w, 'lru_b': lru_b, 'lru_lam': lru_lam_g,
            's5_bb': s5_bb, 's5_cc': s5_cc, 's5_are': s5_are, 's5_aim': s5_aim,
            'd_skip': s5_d[l][None], 'w_glu': s5_w_glu[l].astype(BF16), 'b_glu': s5_b_glu[l][None],
        }
        j = l // 2
        dense = l % 2 == 0
        if not dense:
            p['router'] = jnp.pad(moe_router[j], ((0, 0), (0, LANES - N_EXPERTS)))
        res_p, (k_c, v_c, lru_c, s5r_c, s5i_c) = _token_mixer(
            xp, batch, seq, mods[l], 0, batch * seq, p, None, None, None)
        ks.append(k_c)
        vs.append(v_c)
        lrus.append(lru_c)
        s5rs.append(s5r_c)
        s5is.append(s5i_c)
        res_s, _ = _token_mixer(
            xs, dec_batch, dec_seq, mods[l], 1, dec_seq, p, (cache_k, cache_v, l), state_lru[:, l],
            (state_s5_re[:, l].reshape(dec_batch, 2, -1), state_s5_im[:, l].reshape(dec_batch, 2, -1)))
        if dense:
            w = (ffn_w1[j].astype(BF16), ffn_w3[j].astype(BF16), ffn_w2[j].astype(BF16))
            xp = _ffn(res_p[1], res_p[0], mods[l], 0, batch * seq, *w)
            xs = _ffn(res_s[1], res_s[0], mods[l], 1, dec_seq, *w)
        else:
            h2 = jnp.concatenate([res_p[1], res_s[1]], axis=0)
            pos, src, gate_rows, tile_expert, tile_valid = _route_plan(jnp.concatenate([res_p[2], res_s[2]], axis=0))
            y_rows = _moe_experts(h2, src, gate_rows, tile_expert, tile_valid,
                                  moe_w1[j].astype(BF16), moe_w3[j].astype(BF16), moe_w2[j].astype(BF16))
            xp = _moe_combine(pos, res_p[0], mods[l], 0, batch * seq, y_rows, 0)
            xs = _moe_combine(pos, res_s[0], mods[l], 1, dec_seq, y_rows, batch * seq)
    return (xp.reshape(batch, seq, d), xs.reshape(dec_batch, dec_seq, d),
            jnp.stack(ks, axis=1), jnp.stack(vs, axis=1), jnp.stack(lrus, axis=1),
            jnp.stack(s5rs, axis=1), jnp.stack(s5is, axis=1))
```

```python
import functools

import numpy as np
import jax
import jax.numpy as jnp
from jax import lax
from jax.experimental import pallas as pl
from jax.experimental.pallas import tpu as pltpu

F32 = jnp.float32
BF16 = jnp.bfloat16

D_MODEL = 2048
N_HEADS = 16
HEAD_DIM = 64
D_ATTN = N_HEADS * HEAD_DIM
GRID_W = 64
WIN_R = 8
WIN_C = 16
D_LRU = 512
LRU_BLOCK = 64
CONV_W = 4
LRU_C = 8.0
D_S5 = 512
S5_GROUP = 16
N_S5_GROUPS = 32
S5_STATE = 64
D_REC = 2 * D_LRU + D_S5
D_IN = 3 * D_ATTN + D_REC
N_MOD = 6
N_EXPERTS = 8
EPS = 1e-6
NEG = -1e30

LANES = 128
SUBLANES = 8
MIB = 1024 * 1024

CH = 128
N_CH = D_LRU // CH
S5_ST = (CH // S5_GROUP) * S5_STATE


def _cparams(sem, vmem_mib):
    return pltpu.CompilerParams(dimension_semantics=sem, vmem_limit_bytes=vmem_mib * MIB)


def _adaln_kernel(c_ref, w_ref, b_ref, o_ref):
    s = jax.nn.silu(c_ref[...]).astype(BF16)
    o_ref[0] = jnp.dot(s, w_ref[0].astype(BF16), preferred_element_type=F32) + b_ref[0]


def _adaln(cvecs, w_mod, b_mod):
    depth, d, n = w_mod.shape
    tn = 1024
    return pl.pallas_call(
        _adaln_kernel,
        grid=(depth, n // tn),
        in_specs=[pl.BlockSpec((SUBLANES, d), lambda l, j: (0, 0)),
                  pl.BlockSpec((1, d, tn), lambda l, j: (l, 0, j)),
                  pl.BlockSpec((1, 1, tn), lambda l, j: (l, 0, j))],
        out_specs=pl.BlockSpec((1, SUBLANES, tn), lambda l, j: (l, 0, j)),
        out_shape=jax.ShapeDtypeStruct((depth, SUBLANES, n), F32),
        compiler_params=_cparams(("parallel", "parallel"), 40),
        name="adaln",
    )(cvecs, w_mod, b_mod.reshape(depth, 1, n))


def _rms_mod(x, g, shift, scale):
    xf = x * lax.rsqrt(jnp.mean(x * x, axis=-1, keepdims=True) + EPS)
    return (xf * g) * (1 + scale) + shift


def _in_proj_kernel(x_ref, g_ref, mod_ref, w_ref, qg_ref, kg_ref, ones_ref, *refs, seq, write_cache):
    if write_cache:
        q_ref, k_ref, v_ref, rec_ref, kc_ref, vc_ref, h_scr = refs
    else:
        q_ref, k_ref, v_ref, rec_ref, h_scr = refs
    j = pl.program_id(1)
    tm, tn = q_ref.shape
    heads = tn // HEAD_DIM

    @pl.when(j == 0)
    def _():
        h_scr[...] = _rms_mod(x_ref[...], g_ref[...], mod_ref[0, 0:1, :], mod_ref[0, 1:2, :]).astype(BF16)

    y = jnp.dot(h_scr[...], w_ref[...], preferred_element_type=F32)

    def to_cache(c_ref, val):
        for b in range(tm // seq):
            for h in range(heads):
                c_ref[b, h, :, :] = val[b * seq:(b + 1) * seq, h * HEAD_DIM:(h + 1) * HEAD_DIM]

    @pl.when(j < 4)
    def _():
        y2 = y * y
        hi = y2.astype(BF16)
        lo = (y2 - hi.astype(F32)).astype(BF16)
        ss = (jnp.dot(hi, ones_ref[...], preferred_element_type=F32)
              + jnp.dot(lo, ones_ref[...], preferred_element_type=F32))
        gain = jnp.where(j < 2, qg_ref[...], kg_ref[...])
        yn = (y * lax.rsqrt(ss * (1.0 / HEAD_DIM) + EPS)) * gain

        @pl.when(j < 2)
        def _():
            q_ref[...] = yn.astype(BF16)

        @pl.when(j >= 2)
        def _():
            k_ref[...] = yn.astype(BF16)
            if write_cache:
                to_cache(kc_ref, yn)

    @pl.when((j >= 4) & (j < 6))
    def _():
        v_ref[...] = y.astype(BF16)
        if write_cache:
            to_cache(vc_ref, y)

    @pl.when(j >= 6)
    def _():
        rec_ref[...] = y


def _in_proj(x2d, g, mod, mod_base, rows_per_mod, w_bf, qg, kg, ones_bd, seq, write_cache):
    r, d = x2d.shape
    tm, tn = 512, 512
    n_i = r // tm
    tiles_per_mod = rows_per_mod // tm
    bpt = tm // seq
    hpt = tn // HEAD_DIM

    def col(lo, n):
        return lambda i, j: (i, jnp.clip(j - lo, 0, n - 1))

    out_shape = [jax.ShapeDtypeStruct((r, D_ATTN), BF16)] * 3 + [jax.ShapeDtypeStruct((r, D_REC), F32)]
    out_specs = [pl.BlockSpec((tm, tn), col(0, 2)), pl.BlockSpec((tm, tn), col(2, 2)),
                 pl.BlockSpec((tm, tn), col(4, 2)), pl.BlockSpec((tm, tn), col(6, 3))]
    if write_cache:
        cshape = jax.ShapeDtypeStruct((r // seq, N_HEADS, seq, HEAD_DIM), F32)
        out_shape += [cshape, cshape]
        out_specs += [pl.BlockSpec((bpt, hpt, seq, HEAD_DIM), lambda i, j: (i, jnp.clip(j - 2, 0, 1), 0, 0)),
                      pl.BlockSpec((bpt, hpt, seq, HEAD_DIM), lambda i, j: (i, jnp.clip(j - 4, 0, 1), 0, 0))]
    return pl.pallas_call(
        functools.partial(_in_proj_kernel, seq=seq, write_cache=write_cache),
        grid=(n_i, D_IN // tn),
        in_specs=[pl.BlockSpec((tm, d), lambda i, j: (i, 0)),
                  pl.BlockSpec((1, d), lambda i, j: (0, 0)),
                  pl.BlockSpec((1, N_MOD, d), lambda i, j: (mod_base + i // tiles_per_mod, 0, 0)),
                  pl.BlockSpec((d, tn), lambda i, j: (0, j)),
                  pl.BlockSpec((1, tn), lambda i, j: (0, 0)),
                  pl.BlockSpec((1, tn), lambda i, j: (0, 0)),
                  pl.BlockSpec((tn, tn), lambda i, j: (0, 0))],
        out_specs=out_specs,
        out_shape=out_shape,
        scratch_shapes=[pltpu.VMEM((tm, d), BF16)],
        compiler_params=_cparams(("parallel", "arbitrary"), 52),
        name="in_proj",
    )(x2d, g, mod, w_bf, qg, kg, ones_bd)


def _softmax_rows(parts):
    m = functools.reduce(jnp.maximum, [jnp.max(s, axis=-1, keepdims=True) for s in parts])
    es = [jnp.exp(s - m) for s in parts]
    den = functools.reduce(jnp.add, [jnp.sum(e, axis=-1, keepdims=True) for e in es])
    inv = 1.0 / den
    return [(e * inv).astype(BF16) for e in es]


def _qk(q, k):
    return lax.dot_general(q, k, (((1,), (1,)), ((), ())), preferred_element_type=F32)


def _ctx_attn_kernel(q_ref, k_ref, v_ref, o_ref):
    scale = HEAD_DIM ** -0.5
    for h in range(N_HEADS):
        sl = slice(h * HEAD_DIM, (h + 1) * HEAD_DIM)
        (p,) = _softmax_rows([_qk(q_ref[:, sl], k_ref[:, sl]) * scale])
        o_ref[:, sl] = jnp.dot(p, v_ref[:, sl], preferred_element_type=F32).astype(BF16)


def _ctx_attn(q, k, v, seq):
    r = q.shape[0]
    spec = pl.BlockSpec((seq, D_ATTN), lambda b: (b, 0))
    return pl.pallas_call(
        _ctx_attn_kernel,
        grid=(r // seq,),
        in_specs=[spec, spec, spec],
        out_specs=spec,
        out_shape=jax.ShapeDtypeStruct((r, D_ATTN), BF16),
        compiler_params=_cparams(("parallel",), 32),
        name="ctx_attn",
    )(q, k, v)


def _nbr_plan(seq):
    rows = seq // GRID_W
    wr = min(WIN_R, rows)
    row_start = np.clip(np.arange(rows) - wr // 2, 0, rows - wr)
    rows_per_blk = 256 // GRID_W
    ranges = []
    for qb in range(rows // rows_per_blk):
        rs = row_start[qb * rows_per_blk:(qb + 1) * rows_per_blk]
        lo = int(rs.min()) * GRID_W // LANES * LANES
        hi = -(-(int(rs.max()) + wr) * GRID_W // LANES) * LANES
        ranges.append((lo, hi))
    return rows, wr, row_start, ranges


def _nbr_attn_kernel(q_ref, k_ref, v_ref, kc_ref, vc_ref, tb_ref, o_ref, bias_scr, *, seq):
    rows, wr, row_start, ranges = _nbr_plan(seq)
    scale = HEAD_DIM ** -0.5
    heads = q_ref.shape[1] // HEAD_DIM
    neg = jnp.full((GRID_W, GRID_W), NEG, F32)

    @pl.when(pl.program_id(1) == 0)
    def _():
        for h in range(heads):
            for qr in range(rows):
                for kp in range(rows // 2):
                    blks = []
                    for kr in (2 * kp, 2 * kp + 1):
                        inside = row_start[qr] <= kr < row_start[qr] + wr
                        blks.append(tb_ref[h, kr - qr + WIN_R - 1] if inside else neg)
                    bias_scr[h, qr * GRID_W:(qr + 1) * GRID_W, kp * LANES:(kp + 1) * LANES] = (
                        jnp.concatenate(blks, axis=1))

    for h in range(heads):
        sl = slice(h * HEAD_DIM, (h + 1) * HEAD_DIM)
        kc = kc_ref[0, 0, h].astype(BF16)
        vc = vc_ref[0, 0, h].astype(BF16)
        for qb, (lo, hi) in enumerate(ranges):
            qs = slice(qb * 256, (qb + 1) * 256)
            q = q_ref[qs, sl]
            s_loc = _qk(q, k_ref[lo:hi, sl]) * scale + bias_scr[h, qs, lo:hi]
            s_ctx = _qk(q, kc) * scale
            p_loc, p_ctx = _softmax_rows([s_loc, s_ctx])
            o = (jnp.dot(p_loc, v_ref[lo:hi, sl], preferred_element_type=F32)
                 + jnp.dot(p_ctx, vc, preferred_element_type=F32))
            o_ref[qs, sl] = o.astype(BF16)


def _nbr_attn(q, k, v, cache_k, cache_v, layer, tb, seq):
    r = q.shape[0]
    past = cache_k.shape[3]
    hp = LANES // HEAD_DIM
    spec = pl.BlockSpec((seq, LANES), lambda g, b: (b, g))
    cspec = pl.BlockSpec((1, 1, hp, past, HEAD_DIM), lambda g, b: (b, layer, g, 0, 0))
    return pl.pallas_call(
        functools.partial(_nbr_attn_kernel, seq=seq),
        grid=(N_HEADS // hp, r // seq),
        in_specs=[spec, spec, spec, cspec, cspec,
                  pl.BlockSpec((hp,) + tb.shape[1:], lambda g, b: (g, 0, 0, 0))],
        out_specs=spec,
        out_shape=jax.ShapeDtypeStruct((r, D_ATTN), BF16),
        scratch_shapes=[pltpu.VMEM((hp, seq, seq), F32)],
        compiler_params=_cparams(("parallel", "arbitrary"), 48),
        name="nbr_attn",
    )(q, k, v, cache_k, cache_v, tb)


def _bias_table(rpb_l):
    col = np.arange(GRID_W)
    col_start = np.clip(col - WIN_C // 2, 0, GRID_W - WIN_C)
    col_in = (col[None, :] >= col_start[:, None]) & (col[None, :] < col_start[:, None] + WIN_C)
    dc_idx = np.clip(col[None, :] - col[:, None] + WIN_C - 1, 0, 2 * WIN_C - 2)
    tb = rpb_l[:, :, dc_idx]
    return jnp.where(col_in[None, None], tb, NEG).astype(F32)


def _expm1(x):
    u = jnp.exp(x)
    um1 = u - 1.0
    near = um1 * x / jnp.where(u == 1.0, 1.0, jnp.log(u))
    return jnp.where(x < -0.5, um1, jnp.where(u == 1.0, x, near))


def _slot0(n_rows, n_cols, ns):
    return (lax.broadcasted_iota(jnp.int32, (n_rows, n_cols), 0) % SUBLANES) < ns


def _per_slot(vec, n_rows, ns, nc):
    if nc == 1:
        return vec
    w = vec.shape[1] // 2
    return jnp.where(_slot0(n_rows, w, ns), vec[:, :w], vec[:, w:])


def _stack_slots(x, ns, nc):
    if nc == 1:
        return x
    m = _slot0(x.shape[0], x.shape[1], ns)
    zero = jnp.zeros_like(x)
    return jnp.concatenate([jnp.where(m, x, zero), jnp.where(m, zero, x)], axis=1)


def _pick_slot(y, ns, nc):
    if nc == 1:
        return y
    w = y.shape[1] // 2
    return jnp.where(_slot0(y.shape[0], w, ns), y[:, :w], y[:, w:])


def _to_time_major(x_ref, t_scr, row0, seq, ns, nc):
    for k in range(nc):
        for s in range(ns):
            t_scr[pl.ds(row0 + k * ns + s, seq, stride=SUBLANES), :] = (
                x_ref[s * seq:(s + 1) * seq, k * CH:(k + 1) * CH])


def _rows(start, n):
    return pl.ds(pl.multiple_of(start, SUBLANES), n)


def _rglru_kernel(x_ref, g_ref, cw_ref, cb_ref, w_ref, b_ref, lam_ref, h0_ref, o_ref, fin_ref,
                  xt_scr, yt_scr, af_scr, bf_scr, ab_scr, bb_scr, *, seq, tc, ns, nc):
    rows = tc * SUBLANES
    n_chunks = seq // tc
    front = (CONV_W // 2) * SUBLANES
    back = (CONV_W - 1 - CONV_W // 2) * SUBLANES

    xt_scr[0:front, :] = jnp.zeros((front, CH), F32)
    xt_scr[front + seq * SUBLANES:front + seq * SUBLANES + back, :] = jnp.zeros((back, CH), F32)
    _to_time_major(x_ref, xt_scr, front, seq, ns, nc)
    yt_scr[...] = jnp.zeros_like(yt_scr)

    def coefficients(r0, d, a_scr, b_scr):
        xc = _per_slot(cb_ref[...], rows, ns, nc)
        for j in range(CONV_W):
            xc = xc + xt_scr[_rows(r0 + j * SUBLANES, rows), :] * _per_slot(cw_ref[j:j + 1, :], rows, ns, nc)
        gates = (jnp.dot(_stack_slots(xc, ns, nc).astype(BF16), w_ref[d, 0], preferred_element_type=F32)
                 + _per_slot(b_ref[d, 0], rows, ns, nc))
        rg = jax.nn.sigmoid(gates[:, :CH])
        ig = jax.nn.sigmoid(gates[:, CH:])
        log_a = -LRU_C * rg * _per_slot(jax.nn.softplus(-lam_ref[d:d + 1, :]), rows, ns, nc)
        a_scr[...] = jnp.exp(log_a)
        b_scr[...] = jnp.sqrt(-_expm1(2 * log_a)) * (ig * xc)

    def chunk(c, carry):
        rf = pl.multiple_of(c * rows, rows)
        rb = pl.multiple_of((n_chunks - 1 - c) * rows, rows)
        coefficients(rf, 0, af_scr, bf_scr)
        coefficients(rb, 1, ab_scr, bb_scr)

        def step(t, carry):
            h_f, h_b = carry
            tf = _rows(t * SUBLANES, SUBLANES)
            tb = _rows((tc - 1 - t) * SUBLANES, SUBLANES)
            h_f = af_scr[tf, :] * h_f + bf_scr[tf, :]
            h_b = ab_scr[tb, :] * h_b + bb_scr[tb, :]
            bf_scr[tf, :] = h_f
            bb_scr[tb, :] = h_b
            return h_f, h_b

        carry = lax.fori_loop(0, tc, step, carry, unroll=8)
        yt_scr[pl.ds(rf, rows), :] += bf_scr[...]
        yt_scr[pl.ds(rb, rows), :] += bb_scr[...]
        return carry

    h_f, h_b = lax.fori_loop(0, n_chunks, chunk, (h0_ref[0, 0, 0], h0_ref[0, 0, 1]))
    fin_ref[0, 0, 0] = h_f
    fin_ref[0, 0, 1] = h_b
    for k in range(nc):
        for s in range(ns):
            blk = (slice(s * seq, (s + 1) * seq), slice(k * CH, (k + 1) * CH))
            o_ref[blk] = yt_scr[pl.ds(k * ns + s, seq, stride=SUBLANES), :] * jax.nn.gelu(g_ref[blk])


def _rglru(rec, conv_w, conv_b, w_g, b_g, lam, h0, seq, ns, nc):
    r = rec.shape[0]
    n_cb = N_CH // nc
    tc = 128
    wide = nc * CH
    n_pad = (seq + CONV_W - 1) * SUBLANES
    state = pl.BlockSpec((1, 1, 2, SUBLANES, CH), lambda g, c: (g, c, 0, 0, 0))
    return pl.pallas_call(
        functools.partial(_rglru_kernel, seq=seq, tc=tc, ns=ns, nc=nc),
        grid=(r // (ns * seq), n_cb),
        in_specs=[pl.BlockSpec((ns * seq, wide), lambda g, c: (g, c)),
                  pl.BlockSpec((ns * seq, wide), lambda g, c: (g, n_cb + c)),
                  pl.BlockSpec((CONV_W, wide), lambda g, c: (0, c)),
                  pl.BlockSpec((1, wide), lambda g, c: (0, c)),
                  pl.BlockSpec((2, 1, wide, 2 * CH), lambda g, c: (0, c, 0, 0)),
                  pl.BlockSpec((2, 1, 1, 2 * wide), lambda g, c: (0, c, 0, 0)),
                  pl.BlockSpec((2, wide), lambda g, c: (0, c)),
                  state],
        out_specs=[pl.BlockSpec((ns * seq, wide), lambda g, c: (g, c)), state],
        out_shape=[jax.ShapeDtypeStruct((r, D_LRU), F32),
                   jax.ShapeDtypeStruct((r // (ns * seq), n_cb, 2, SUBLANES, CH), F32)],
        scratch_shapes=[pltpu.VMEM((n_pad, CH), F32), pltpu.VMEM((seq * SUBLANES, CH), F32)]
        + [pltpu.VMEM((tc * SUBLANES, CH), F32)] * 4,
        compiler_params=_cparams(("parallel", "parallel"), 48),
        name="rglru",
    )(rec, rec, conv_w, conv_b, w_g, b_g, lam, h0)


def _rglru_params(wa, ba, wi, bi, nc):
    bpc = CH // LRU_BLOCK
    n_cb = N_CH // nc
    eye = jnp.eye(bpc, dtype=F32)

    def dense(w):
        w = w.reshape(2, N_CH, bpc, LRU_BLOCK, LRU_BLOCK)
        return (w[:, :, :, :, None, :] * eye[None, None, :, None, :, None]).reshape(2, N_CH, CH, CH)

    w_g = jnp.concatenate([dense(wa), dense(wi)], axis=-1).reshape(2, n_cb, nc * CH, 2 * CH).astype(BF16)
    b_g = jnp.concatenate([ba.reshape(2, N_CH, CH), bi.reshape(2, N_CH, CH)], axis=-1)
    return w_g, b_g.reshape(2, n_cb, 1, nc * 2 * CH)


def _s5_kernel(u_ref, bb_ref, cc_ref, ar_ref, ai_ref, d_ref, s0r_ref, s0i_ref, y_ref, fr_ref, fi_ref,
               ut_scr, yt_scr, sf_scr, sb_scr, *, seq, tc, ns, nc):
    rows = tc * SUBLANES
    n_chunks = seq // tc
    _to_time_major(u_ref, ut_scr, 0, seq, ns, nc)
    yt_scr[...] = jnp.zeros_like(yt_scr)

    def chunk(c, carry):
        rf = pl.multiple_of(c * rows, rows)
        rb = pl.multiple_of((n_chunks - 1 - c) * rows, rows)
        for d, r0, st_scr in ((0, rf, sf_scr), (1, rb, sb_scr)):
            u = _stack_slots(ut_scr[pl.ds(r0, rows), :], ns, nc)
            st_scr[...] = jnp.dot(u.astype(BF16), bb_ref[d, 0], preferred_element_type=F32)

        def update(st_scr, rr, d, s_re, s_im):
            a_re = ar_ref[d, 0]
            a_im = ai_ref[d, 0]
            n_re = a_re * s_re - a_im * s_im + st_scr[rr, :S5_ST]
            n_im = a_re * s_im + a_im * s_re + st_scr[rr, S5_ST:]
            st_scr[rr, :S5_ST] = n_re
            st_scr[rr, S5_ST:] = n_im
            return n_re, n_im

        def step(t, carry):
            f_re, f_im, b_re, b_im = carry
            f_re, f_im = update(sf_scr, _rows(t * SUBLANES, SUBLANES), 0, f_re, f_im)
            b_re, b_im = update(sb_scr, _rows((tc - 1 - t) * SUBLANES, SUBLANES), 1, b_re, b_im)
            return f_re, f_im, b_re, b_im

        carry = lax.fori_loop(0, tc, step, carry, unroll=2)
        for d, r0, st_scr in ((0, rf, sf_scr), (1, rb, sb_scr)):
            y = jnp.dot(st_scr[...].astype(BF16), cc_ref[d, 0], preferred_element_type=F32)
            yt_scr[pl.ds(r0, rows), :] += _pick_slot(y, ns, nc)
        return carry

    f_re, f_im, b_re, b_im = lax.fori_loop(
        0, n_chunks, chunk, (s0r_ref[0, 0, 0], s0i_ref[0, 0, 0], s0r_ref[0, 0, 1], s0i_ref[0, 0, 1]))
    fr_ref[0, 0, 0] = f_re
    fi_ref[0, 0, 0] = f_im
    fr_ref[0, 0, 1] = b_re
    fi_ref[0, 0, 1] = b_im
    for k in range(nc):
        for s in range(ns):
            blk = (slice(s * seq, (s + 1) * seq), slice(k * CH, (k + 1) * CH))
            y_ref[blk] = (d_ref[:, k * CH:(k + 1) * CH] * u_ref[blk]
                          + yt_scr[pl.ds(k * ns + s, seq, stride=SUBLANES), :])


def _s5(rec, bb, cc, a_re, a_im, d_skip, s0_re, s0_im, seq, ns, nc):
    r = rec.shape[0]
    n_cb = N_CH // nc
    tc = 64
    wide = nc * CH
    state = pl.BlockSpec((1, 1, 2, SUBLANES, S5_ST), lambda g, c: (g, c, 0, 0, 0))
    aspec = pl.BlockSpec((2, 1, SUBLANES, S5_ST), lambda g, c: (0, c, 0, 0))
    st_shape = jax.ShapeDtypeStruct((r // (ns * seq), n_cb, 2, SUBLANES, S5_ST), F32)
    return pl.pallas_call(
        functools.partial(_s5_kernel, seq=seq, tc=tc, ns=ns, nc=nc),
        grid=(r // (ns * seq), n_cb),
        in_specs=[pl.BlockSpec((ns * seq, wide), lambda g, c: (g, 2 * n_cb + c)),
                  pl.BlockSpec((2, 1, wide, 2 * S5_ST), lambda g, c: (0, c, 0, 0)),
                  pl.BlockSpec((2, 1, 2 * S5_ST, wide), lambda g, c: (0, c, 0, 0)),
                  aspec, aspec,
                  pl.BlockSpec((1, wide), lambda g, c: (0, c)),
                  state, state],
        out_specs=[pl.BlockSpec((ns * seq, wide), lambda g, c: (g, c)), state, state],
        out_shape=[jax.ShapeDtypeStruct((r, D_S5), F32), st_shape, st_shape],
        scratch_shapes=[pltpu.VMEM((seq * SUBLANES, CH), F32), pltpu.VMEM((seq * SUBLANES, CH), F32),
                        pltpu.VMEM((tc * SUBLANES, 2 * S5_ST), F32), pltpu.VMEM((tc * SUBLANES, 2 * S5_ST), F32)],
        compiler_params=_cparams(("parallel", "parallel"), 48),
        name="s5",
    )(rec, bb, cc, a_re, a_im, d_skip, s0_re, s0_im)


def _s5_params(lam_re, lam_im, log_dt, b_re, b_im, c_re, c_im, ns, nc):
    dt = jnp.exp(log_dt)[..., None]
    mag = jnp.exp(lam_re * dt)
    abar_re, abar_im = mag * jnp.cos(lam_im * dt), mag * jnp.sin(lam_im * dt)
    den = lam_re * lam_re + lam_im * lam_im
    nr, ni = abar_re - 1, abar_im
    cr = (nr * lam_re + ni * lam_im) / den
    ci = (ni * lam_re - nr * lam_im) / den
    bb_re = cr[..., None] * b_re - ci[..., None] * b_im
    bb_im = cr[..., None] * b_im + ci[..., None] * b_re
    gpc = CH // S5_GROUP
    eye = jnp.eye(gpc, dtype=F32)

    def in_map(b):
        b = b.reshape(2, N_CH, gpc, S5_STATE, S5_GROUP).transpose(0, 1, 2, 4, 3)
        return (b[:, :, :, :, None, :] * eye[None, None, :, None, :, None]).reshape(2, N_CH, CH, S5_ST)

    def out_map(c):
        c = c.reshape(2, N_CH, gpc, S5_GROUP, S5_STATE).transpose(0, 1, 2, 4, 3)
        return (c[:, :, :, :, None, :] * eye[None, None, :, None, :, None]).reshape(2, N_CH, S5_ST, CH)

    bb = jnp.concatenate([in_map(bb_re), in_map(bb_im)], axis=-1).astype(BF16)
    cc = jnp.concatenate([out_map(c_re), -out_map(c_im)], axis=-2).astype(BF16)
    n_cb = N_CH // nc
    bb = bb.reshape(2, n_cb, nc * CH, 2 * S5_ST)
    cc = cc.reshape(2, n_cb, nc, 2 * S5_ST, CH).transpose(0, 1, 3, 2, 4).reshape(2, n_cb, 2 * S5_ST, nc * CH)

    def per_slot(a):
        a = jnp.broadcast_to(a.reshape(2, n_cb, nc, 1, S5_ST), (2, n_cb, nc, ns, S5_ST))
        return a.reshape(2, n_cb, SUBLANES, S5_ST)

    return bb, cc, per_slot(abar_re), per_slot(abar_im)


def _out_proj_kernel(x_ref, oa_ref, ob_ref, y_ref, wg_ref, bg_ref, wo_ref, mod_ref, g_ref, *refs, route):
    if route:
        r_ref, x1_ref, h2_ref, route_ref = refs
    else:
        x1_ref, h2_ref = refs
    z = jax.nn.gelu(y_ref[...])
    gl = jnp.dot(z.astype(BF16), wg_ref[...], preferred_element_type=F32) + bg_ref[...]
    oc = z * jax.nn.sigmoid(gl)
    mix = (jnp.dot(oa_ref[...], wo_ref[:D_ATTN, :], preferred_element_type=F32)
           + jnp.dot(ob_ref[...].astype(BF16), wo_ref[D_ATTN:D_ATTN + D_LRU, :], preferred_element_type=F32)
           + jnp.dot(oc.astype(BF16), wo_ref[D_ATTN + D_LRU:, :], preferred_element_type=F32))
    x1 = x_ref[...] + mod_ref[0, 2:3, :] * mix
    x1_ref[...] = x1
    h2 = _rms_mod(x1, g_ref[...], mod_ref[0, 3:4, :], mod_ref[0, 4:5, :])
    h2_ref[...] = h2.astype(h2_ref.dtype)
    if route:
        hi = h2.astype(BF16)
        lo = (h2 - hi.astype(F32)).astype(BF16)
        pr = (jnp.dot(hi, r_ref[...], preferred_element_type=F32)
              + jnp.dot(lo, r_ref[...], preferred_element_type=F32))
        logits = pr + pltpu.roll(pr, LANES - N_EXPERTS, axis=1)
        lane = lax.broadcasted_iota(jnp.int32, logits.shape, 1)
        lg = jnp.where(lane < N_EXPERTS, logits, -jnp.inf)
        m1 = jnp.max(lg, axis=-1, keepdims=True)
        i1 = jnp.min(jnp.where(lg == m1, lane, LANES), axis=-1, keepdims=True)
        lg2 = jnp.where(lane == i1, -jnp.inf, lg)
        m2 = jnp.max(lg2, axis=-1, keepdims=True)
        i2 = jnp.min(jnp.where(lg2 == m2, lane, LANES), axis=-1, keepdims=True)
        e2 = jnp.exp(m2 - m1)
        den = 1.0 + e2
        route_ref[...] = (jnp.where(lane == 0, i1.astype(F32), 0.0) + jnp.where(lane == 1, i2.astype(F32), 0.0)
                          + jnp.where(lane == 2, 1.0 / den, 0.0) + jnp.where(lane == 3, e2 / den, 0.0))


def _out_proj(x2d, oa, ob, y, w_glu_bf, b_glu, w_out_bf, mod, mod_base, rows_per_mod, g, router_pad):
    r, d = x2d.shape
    tm = 512
    tiles_per_mod = rows_per_mod // tm
    route = router_pad is not None
    row = lambda n: pl.BlockSpec((tm, n), lambda i: (i, 0))
    full = lambda a: pl.BlockSpec(a.shape, lambda i: (0,) * a.ndim, pipeline_mode=pl.Buffered(1))
    args = [x2d, oa, ob, y, w_glu_bf, b_glu, w_out_bf, mod, g]
    in_specs = [row(d), row(D_ATTN), row(D_LRU), row(D_S5), full(w_glu_bf), full(b_glu), full(w_out_bf),
                pl.BlockSpec((1, N_MOD, d), lambda i: (mod_base + i // tiles_per_mod, 0, 0)), full(g)]
    out_shape = [jax.ShapeDtypeStruct((r, d), F32), jax.ShapeDtypeStruct((r, d), F32 if route else BF16)]
    out_specs = [row(d), row(d)]
    if route:
        args.append(router_pad)
        in_specs.append(full(router_pad))
        out_shape.append(jax.ShapeDtypeStruct((r, LANES), F32))
        out_specs.append(row(LANES))
    return pl.pallas_call(
        functools.partial(_out_proj_kernel, route=route),
        grid=(r // tm,),
        in_specs=in_specs,
        out_specs=out_specs,
        out_shape=out_shape,
        compiler_params=_cparams(("parallel",), 56),
        name="out_proj",
    )(*args)


def _swiglu_part(h, w1, w3, w2):
    a = jnp.dot(h, w1, preferred_element_type=F32)
    b = jnp.dot(h, w3, preferred_element_type=F32)
    return jnp.dot((jax.nn.silu(a) * b).astype(BF16), w2, preferred_element_type=F32)


def _ffn_kernel(h_ref, x_ref, mod_ref, w1_ref, w3_ref, w2_ref, o_ref):
    f = pl.program_id(1)
    part = _swiglu_part(h_ref[...], w1_ref[...], w3_ref[...], w2_ref[...])

    @pl.when(f == 0)
    def _():
        o_ref[...] = part

    @pl.when(f > 0)
    def _():
        o_ref[...] += part

    @pl.when(f == pl.num_programs(1) - 1)
    def _():
        o_ref[...] = x_ref[...] + mod_ref[0, 5:6, :] * o_ref[...]


def _ffn(h2, x1, mod, mod_base, rows_per_mod, w1, w3, w2):
    r, d = x1.shape
    tm, tf = 1024, 512
    tiles_per_mod = rows_per_mod // tm
    return pl.pallas_call(
        _ffn_kernel,
        grid=(r // tm, w1.shape[1] // tf),
        in_specs=[pl.BlockSpec((tm, d), lambda i, f: (i, 0)),
                  pl.BlockSpec((tm, d), lambda i, f: (i, 0), pipeline_mode=pl.Buffered(1)),
                  pl.BlockSpec((1, N_MOD, d), lambda i, f: (mod_base + i // tiles_per_mod, 0, 0)),
                  pl.BlockSpec((d, tf), lambda i, f: (0, f)),
                  pl.BlockSpec((d, tf), lambda i, f: (0, f)),
                  pl.BlockSpec((tf, d), lambda i, f: (f, 0))],
        out_specs=pl.BlockSpec((tm, d), lambda i, f: (i, 0)),
        out_shape=jax.ShapeDtypeStruct((r, d), F32),
        compiler_params=_cparams(("parallel", "arbitrary"), 60),
        name="ffn",
    )(h2, x1, mod, w1, w3, w2)


MOE_TM = 512
MOE_TF = 1408
MOE_TT = 256


def _route_plan(route):
    t = route.shape[0]
    n_pairs = 2 * t
    n_tiles = n_pairs // MOE_TM + N_EXPERTS
    experts = route[:, :2].astype(jnp.int32).reshape(n_pairs)
    onehot = (experts[:, None] == jnp.arange(N_EXPERTS, dtype=jnp.int32)[None]).astype(jnp.int32)
    csum = jnp.cumsum(onehot, axis=0)
    rank = jnp.sum(onehot * csum, axis=1) - 1
    counts = csum[-1]
    padded = (counts + MOE_TM - 1) // MOE_TM * MOE_TM
    ends = jnp.cumsum(padded)
    pos = jnp.sum(onehot * (ends - padded)[None], axis=1) + rank
    src = jnp.zeros((n_tiles * MOE_TM,), jnp.int32).at[pos].set(jnp.arange(n_pairs, dtype=jnp.int32) // 2)
    tile_start = jnp.arange(n_tiles, dtype=jnp.int32) * MOE_TM
    tile_valid = (tile_start < ends[-1]).astype(jnp.int32)
    tile_expert = jnp.sum((tile_start[:, None] >= ends[None]).astype(jnp.int32), axis=1)
    last_expert = jnp.max(jnp.where(counts > 0, jnp.arange(N_EXPERTS, dtype=jnp.int32), 0))
    tile_expert = jnp.where(tile_valid == 1, tile_expert, last_expert)
    return pos, src, tile_expert, tile_valid


def _gather_rows(idx_of_row, n_rows, src_hbm, dst, sem):
    def row(r, carry):
        pltpu.make_async_copy(src_hbm.at[pl.ds(idx_of_row(r), 1)], dst.at[pl.ds(r, 1)], sem).start()
        return carry
    lax.fori_loop(0, n_rows, row, 0, unroll=8)


def _wait_rows(n_rows, src_hbm, dst, sem):
    pltpu.make_async_copy(src_hbm.at[pl.ds(0, n_rows)], dst, sem).wait()


def _moe_experts_kernel(src_ref, texp_ref, valid_ref, h_hbm, w1_ref, w3_ref, w2_ref, o_ref, xbuf, xbf, sem):
    i = pl.program_id(0)
    f = pl.program_id(1)

    def gather(tile):
        _gather_rows(lambda r: src_ref[tile * MOE_TM + r], MOE_TM, h_hbm, xbuf, sem.at[0])

    @pl.when(f == 0)
    def _():
        @pl.when(i == 0)
        def _():
            gather(0)

        _wait_rows(MOE_TM, h_hbm, xbuf, sem.at[0])
        xbf[...] = xbuf[...].astype(BF16)

        @pl.when(i + 1 < pl.num_programs(0))
        def _():
            gather(i + 1)

    valid = valid_ref[i] == 1

    @pl.when(valid)
    def _():
        part = _swiglu_part(xbf[...], w1_ref[0], w3_ref[0], w2_ref[0])

        @pl.when(f == 0)
        def _():
            o_ref[...] = part

        @pl.when(f > 0)
        def _():
            o_ref[...] += part

    @pl.when(jnp.logical_not(valid) & (f == 0))
    def _():
        o_ref[...] = jnp.zeros_like(o_ref)


def _moe_experts(h2, src, tile_expert, tile_valid, w1, w3, w2):
    t, d = h2.shape
    n_tiles = tile_expert.shape[0]
    f_e = w1.shape[2]
    n_f = f_e // MOE_TF

    def wmap(i, f, src, texp, valid):
        return (texp[i], 0, jnp.where(valid[i] == 1, f, n_f - 1))

    def w2map(i, f, src, texp, valid):
        return (texp[i], jnp.where(valid[i] == 1, f, n_f - 1), 0)

    grid_spec = pltpu.PrefetchScalarGridSpec(
        num_scalar_prefetch=3,
        grid=(n_tiles, n_f),
        in_specs=[pl.BlockSpec(memory_space=pl.ANY),
                  pl.BlockSpec((1, d, MOE_TF), wmap),
                  pl.BlockSpec((1, d, MOE_TF), wmap),
                  pl.BlockSpec((1, MOE_TF, d), w2map)],
        out_specs=pl.BlockSpec((MOE_TM, d), lambda i, f, *_: (i, 0)),
        scratch_shapes=[pltpu.VMEM((MOE_TM, d), F32), pltpu.VMEM((MOE_TM, d), BF16),
                        pltpu.SemaphoreType.DMA((1,))])
    return pl.pallas_call(
        _moe_experts_kernel,
        grid_spec=grid_spec,
        out_shape=jax.ShapeDtypeStruct((n_tiles * MOE_TM, d), F32),
        compiler_params=_cparams(("arbitrary", "arbitrary"), 62),
        name="moe_experts",
    )(src, tile_expert, tile_valid, h2, w1, w3, w2)


def _moe_combine_kernel(pos_ref, x_ref, mod_ref, route_ref, y_hbm, o_ref, buf, sem, *, tok_base):
    i = pl.program_id(0)
    slot = i % 2

    def gather(tile, slot):
        for k in range(2):
            _gather_rows(lambda r, k=k: pos_ref[2 * (tok_base + tile * MOE_TT + r) + k], MOE_TT, y_hbm,
                         buf.at[slot, k], sem.at[slot])

    @pl.when(i == 0)
    def _():
        gather(0, 0)

    for k in range(2):
        _wait_rows(MOE_TT, y_hbm, buf.at[slot, k], sem.at[slot])

    @pl.when(i + 1 < pl.num_programs(0))
    def _():
        gather(i + 1, 1 - slot)

    mixed = route_ref[:, 2:3] * buf[slot, 0] + route_ref[:, 3:4] * buf[slot, 1]
    o_ref[...] = x_ref[...] + mod_ref[0, 5:6, :] * mixed


def _moe_combine(pos, x1, mod, mod_base, rows_per_mod, route, y_rows, tok_base):
    r, d = x1.shape
    tiles_per_mod = rows_per_mod // MOE_TT
    grid_spec = pltpu.PrefetchScalarGridSpec(
        num_scalar_prefetch=1,
        grid=(r // MOE_TT,),
        in_specs=[pl.BlockSpec((MOE_TT, d), lambda i, *_: (i, 0)),
                  pl.BlockSpec((1, N_MOD, d), lambda i, *_: (mod_base + i // tiles_per_mod, 0, 0)),
                  pl.BlockSpec((MOE_TT, LANES), lambda i, *_: (i, 0)),
                  pl.BlockSpec(memory_space=pl.ANY)],
        out_specs=pl.BlockSpec((MOE_TT, d), lambda i, *_: (i, 0)),
        scratch_shapes=[pltpu.VMEM((2, 2, MOE_TT, d), F32), pltpu.SemaphoreType.DMA((2,))])
    return pl.pallas_call(
        functools.partial(_moe_combine_kernel, tok_base=tok_base),
        grid_spec=grid_spec,
        out_shape=jax.ShapeDtypeStruct((r, d), F32),
        compiler_params=_cparams(("arbitrary",), 32),
        name="moe_combine",
    )(pos, x1, mod, route, y_rows)


def _slot_states(s, ns, nc, width):
    ng = s.shape[0] // ns
    s = s.reshape(ng, ns, 2, N_CH // nc, nc, width).transpose(0, 3, 2, 4, 1, 5)
    return s.reshape(ng, N_CH // nc, 2, SUBLANES, width)


def _unslot_states(s, ns, nc, width):
    ng = s.shape[0]
    s = s.reshape(ng, N_CH // nc, 2, nc, ns, width).transpose(0, 4, 2, 1, 3, 5)
    return s.reshape(ng * ns, 2, N_CH * width)


def _token_mixer(x2d, batch, seq, mod, mod_base, rows_per_mod, p, kv_ctx, lru_h0, s5_h0):
    is_ctx = kv_ctx is None
    ns = min(batch, SUBLANES)
    nc = SUBLANES // ns
    outs = _in_proj(x2d, p['g_mix'], mod, mod_base, rows_per_mod, p['w_in'], p['q_g'], p['k_g'], p['ones_bd'],
                    seq, is_ctx)
    q, k, v, rec = outs[:4]
    if is_ctx:
        oa = _ctx_attn(q, k, v, seq)
        lru_h0 = jnp.zeros((batch, 2, D_LRU), F32)
        s5_h0 = (jnp.zeros((batch, 2, N_S5_GROUPS * S5_STATE), F32),) * 2
    else:
        oa = _nbr_attn(q, k, v, kv_ctx[0], kv_ctx[1], kv_ctx[2], p['tb'], seq)
    lru_w, lru_b = _rglru_params(p['lru_wa'], p['lru_ba'], p['lru_wi'], p['lru_bi'], nc)
    ob, lru_fin = _rglru(rec, p['conv_w'], p['conv_b'], lru_w, lru_b, p['lru_lam'],
                         _slot_states(lru_h0, ns, nc, CH), seq, ns, nc)
    s5_bb, s5_cc, s5_are, s5_aim = _s5_params(*p['s5'], ns, nc)
    y, fin_re, fin_im = _s5(rec, s5_bb, s5_cc, s5_are, s5_aim, p['d_skip'],
                            _slot_states(s5_h0[0], ns, nc, S5_ST), _slot_states(s5_h0[1], ns, nc, S5_ST),
                            seq, ns, nc)
    res = _out_proj(x2d, oa, ob, y, p['w_glu'], p['b_glu'], p['w_out'], mod, mod_base, rows_per_mod,
                    p['g_ffn'], p.get('router'))
    if not is_ctx:
        return res, None
    state = (outs[4], outs[5], _unslot_states(lru_fin, ns, nc, CH),
             _unslot_states(fin_re, ns, nc, S5_ST).reshape(batch, 2, N_S5_GROUPS, S5_STATE),
             _unslot_states(fin_im, ns, nc, S5_ST).reshape(batch, 2, N_S5_GROUPS, S5_STATE))
    return res, state


def kernel(x_prompt, x_sample, c, cache_k, cache_v, state_lru, state_s5_re, state_s5_im, c_ctx, norm_mix_g, norm_ffn_g, w_mod, b_mod, w_in, w_out, q_norm_g, k_norm_g, rpb, lru_conv_w, lru_conv_b, lru_wa, lru_ba, lru_wi, lru_bi, lru_lam, s5_lam_re, s5_lam_im, s5_log_dt, s5_b_re, s5_b_im, s5_c_re, s5_c_im, s5_d, s5_w_glu, s5_b_glu, ffn_w1, ffn_w3, ffn_w2, moe_router, moe_w1, moe_w3, moe_w2):
    batch, seq, d = x_prompt.shape
    dec_batch, dec_seq, _ = x_sample.shape
    depth = w_in.shape[0]
    assert dec_batch + 1 <= SUBLANES and batch % SUBLANES == 0 and dec_batch in (4, 8)

    cvecs = jnp.concatenate([c_ctx[None], c, jnp.zeros((SUBLANES - 1 - dec_batch, d), F32)], axis=0)
    mods = _adaln(cvecs, w_mod, b_mod).reshape(depth, SUBLANES, N_MOD, d)

    heads_per_tile = 512 // HEAD_DIM
    ones_bd = jnp.asarray(np.kron(np.eye(heads_per_tile), np.ones((HEAD_DIM, HEAD_DIM))), BF16)

    xp = x_prompt.reshape(batch * seq, d)
    xs = x_sample.reshape(dec_batch * dec_seq, d)
    ks, vs, lrus, s5rs, s5is = [], [], [], [], []
    for l in range(depth):
        p = {
            'g_mix': norm_mix_g[l][None], 'g_ffn': norm_ffn_g[l][None],
            'w_in': w_in[l].astype(BF16), 'w_out': w_out[l].astype(BF16),
            'q_g': jnp.tile(q_norm_g[l], heads_per_tile)[None], 'k_g': jnp.tile(k_norm_g[l], heads_per_tile)[None],
            'ones_bd': ones_bd, 'tb': _bias_table(rpb[l]),
            'conv_w': lru_conv_w[l], 'conv_b': lru_conv_b[l][None],
            'lru_wa': lru_wa[l], 'lru_ba': lru_ba[l], 'lru_wi': lru_wi[l], 'lru_bi': lru_bi[l], 'lru_lam': lru_lam[l],
            's5': (s5_lam_re[l], s5_lam_im[l], s5_log_dt[l], s5_b_re[l], s5_b_im[l], s5_c_re[l], s5_c_im[l]),
            'd_skip': s5_d[l][None], 'w_glu': s5_w_glu[l].astype(BF16), 'b_glu': s5_b_glu[l][None],
        }
        j = l // 2
        dense = l % 2 == 0
        if not dense:
            r_hi = moe_router[j].astype(BF16)
            r_lo = (moe_router[j] - r_hi.astype(F32)).astype(BF16)
            p['router'] = jnp.pad(jnp.concatenate([r_hi, r_lo], axis=1), ((0, 0), (0, LANES - 2 * N_EXPERTS)))
        res_p, (k_c, v_c, lru_c, s5r_c, s5i_c) = _token_mixer(
            xp, batch, seq, mods[l], 0, batch * seq, p, None, None, None)
        ks.append(k_c)
        vs.append(v_c)
        lrus.append(lru_c)
        s5rs.append(s5r_c)
        s5is.append(s5i_c)
        res_s, _ = _token_mixer(
            xs, dec_batch, dec_seq, mods[l], 1, dec_seq, p, (cache_k, cache_v, l), state_lru[:, l],
            (state_s5_re[:, l].reshape(dec_batch, 2, -1), state_s5_im[:, l].reshape(dec_batch, 2, -1)))
        if dense:
            w = (ffn_w1[j].astype(BF16), ffn_w3[j].astype(BF16), ffn_w2[j].astype(BF16))
            xp = _ffn(res_p[1], res_p[0], mods[l], 0, batch * seq, *w)
            xs = _ffn(res_s[1], res_s[0], mods[l], 1, dec_seq, *w)
        else:
            h2 = jnp.concatenate([res_p[1], res_s[1]], axis=0)
            pos, src, tile_expert, tile_valid = _route_plan(jnp.concatenate([res_p[2], res_s[2]], axis=0))
            y_rows = _moe_experts(h2, src, tile_expert, tile_valid,
                                  moe_w1[j].astype(BF16), moe_w3[j].astype(BF16), moe_w2[j].astype(BF16))
            xp = _moe_combine(pos, res_p[0], mods[l], 0, batch * seq, res_p[2], y_rows, 0)
            xs = _moe_combine(pos, res_s[0], mods[l], 1, dec_seq, res_s[2], y_rows, batch * seq)
    return (xp.reshape(batch, seq, d), xs.reshape(dec_batch, dec_seq, d),
            jnp.stack(ks, axis=1), jnp.stack(vs, axis=1), jnp.stack(lrus, axis=1),
            jnp.stack(s5rs, axis=1), jnp.stack(s5is, axis=1))
```

```python
import functools

import numpy as np
import jax
import jax.numpy as jnp
from jax import lax
from jax.experimental import pallas as pl
from jax.experimental.pallas import tpu as pltpu

F32 = jnp.float32
BF16 = jnp.bfloat16

D_MODEL = 2048
N_HEADS = 16
HEAD_DIM = 64
D_ATTN = N_HEADS * HEAD_DIM
GRID_W = 64
WIN_R = 8
WIN_C = 16
D_LRU = 512
LRU_BLOCK = 64
CONV_W = 4
LRU_C = 8.0
D_S5 = 512
S5_GROUP = 16
N_S5_GROUPS = 32
S5_STATE = 64
D_REC = 2 * D_LRU + D_S5
D_IN = 3 * D_ATTN + D_REC
N_MOD = 6
N_EXPERTS = 8
EPS = 1e-6
NEG = -1e30

LANES = 128
SUBLANES = 8
MIB = 1024 * 1024

CH = 128
N_CH = D_LRU // CH
S5_ST = (CH // S5_GROUP) * S5_STATE


def _cparams(sem, vmem_mib):
    return pltpu.CompilerParams(dimension_semantics=sem, vmem_limit_bytes=vmem_mib * MIB)


def _adaln_kernel(c_ref, w_ref, b_ref, o_ref):
    s = jax.nn.silu(c_ref[...]).astype(BF16)
    o_ref[0] = jnp.dot(s, w_ref[0].astype(BF16), preferred_element_type=F32) + b_ref[0]


def _adaln(cvecs, w_mod, b_mod):
    depth, d, n = w_mod.shape
    tn = 1024
    return pl.pallas_call(
        _adaln_kernel,
        grid=(depth, n // tn),
        in_specs=[pl.BlockSpec((SUBLANES, d), lambda l, j: (0, 0)),
                  pl.BlockSpec((1, d, tn), lambda l, j: (l, 0, j)),
                  pl.BlockSpec((1, 1, tn), lambda l, j: (l, 0, j))],
        out_specs=pl.BlockSpec((1, SUBLANES, tn), lambda l, j: (l, 0, j)),
        out_shape=jax.ShapeDtypeStruct((depth, SUBLANES, n), F32),
        compiler_params=_cparams(("parallel", "parallel"), 40),
        name="adaln",
    )(cvecs, w_mod, b_mod.reshape(depth, 1, n))


def _rms_mod(x, g, shift, scale):
    xf = x * lax.rsqrt(jnp.mean(x * x, axis=-1, keepdims=True) + EPS)
    return (xf * g) * (1 + scale) + shift


def _in_proj_kernel(x_ref, g_ref, mod_ref, w_ref, qg_ref, kg_ref, ones_ref, *refs, seq, write_cache, n_prev):
    kp_ref = vp_ref = None
    if write_cache and n_prev:
        kp_ref, vp_ref, *refs = refs
    if write_cache:
        q_ref, k_ref, v_ref, rec_ref, kc_ref, vc_ref, h_scr = refs
    else:
        q_ref, k_ref, v_ref, rec_ref, h_scr = refs
    j = pl.program_id(1)
    tm, tn = q_ref.shape
    heads = tn // HEAD_DIM

    @pl.when(j == 0)
    def _():
        h_scr[...] = _rms_mod(x_ref[...], g_ref[...], mod_ref[0, 0:1, :], mod_ref[0, 1:2, :]).astype(BF16)

    y = jnp.dot(h_scr[...], w_ref[...], preferred_element_type=F32)

    def to_cache(c_ref, prev_ref, val):
        if prev_ref is not None:
            c_ref[:, :n_prev] = prev_ref[...]
        for b in range(tm // seq):
            for h in range(heads):
                c_ref[b, n_prev, h, :, :] = val[b * seq:(b + 1) * seq, h * HEAD_DIM:(h + 1) * HEAD_DIM]

    @pl.when(j < 4)
    def _():
        y2 = y * y
        hi = y2.astype(BF16)
        lo = (y2 - hi.astype(F32)).astype(BF16)
        half = ones_ref.shape[0]
        ss = jnp.concatenate(
            [jnp.dot(hi[:, c:c + half], ones_ref[...], preferred_element_type=F32)
             + jnp.dot(lo[:, c:c + half], ones_ref[...], preferred_element_type=F32) for c in range(0, tn, half)],
            axis=1)
        gain = jnp.where(j < 2, qg_ref[...], kg_ref[...])
        yn = (y * lax.rsqrt(ss * (1.0 / HEAD_DIM) + EPS)) * gain

        @pl.when(j < 2)
        def _():
            q_ref[...] = yn.astype(BF16)

        @pl.when(j >= 2)
        def _():
            k_ref[...] = yn.astype(BF16)
            if write_cache:
                to_cache(kc_ref, kp_ref, yn)

    @pl.when((j >= 4) & (j < 6))
    def _():
        v_ref[...] = y.astype(BF16)
        if write_cache:
            to_cache(vc_ref, vp_ref, y)

    @pl.when(j >= 6)
    def _():
        rec_ref[...] = y


def _in_proj(x2d, g, mod, mod_base, rows_per_mod, w_bf, qg, kg, ones_bd, seq, write_cache, prev_kv):
    r, d = x2d.shape
    tm, tn = 512, 512
    n_i = r // tm
    tiles_per_mod = rows_per_mod // tm
    bpt = tm // seq
    hpt = tn // HEAD_DIM
    n_prev = prev_kv[0].shape[1] if prev_kv else 0

    def col(lo, n):
        return lambda i, j: (i, jnp.clip(j - lo, 0, n - 1))

    def cache_spec(lo, n_layers):
        return pl.BlockSpec((bpt, n_layers, hpt, seq, HEAD_DIM), lambda i, j: (i, 0, jnp.clip(j - lo, 0, 1), 0, 0))

    args = [x2d, g, mod, w_bf, qg, kg, ones_bd]
    in_specs = [pl.BlockSpec((tm, d), lambda i, j: (i, 0)),
                pl.BlockSpec((1, d), lambda i, j: (0, 0)),
                pl.BlockSpec((1, N_MOD, d), lambda i, j: (mod_base + i // tiles_per_mod, 0, 0)),
                pl.BlockSpec((d, tn), lambda i, j: (0, j)),
                pl.BlockSpec((1, tn), lambda i, j: (0, 0)),
                pl.BlockSpec((1, tn), lambda i, j: (0, 0)),
                pl.BlockSpec(ones_bd.shape, lambda i, j: (0, 0))]
    out_shape = [jax.ShapeDtypeStruct((r, D_ATTN), BF16)] * 3 + [jax.ShapeDtypeStruct((r, D_REC), F32)]
    out_specs = [pl.BlockSpec((tm, tn), col(0, 2)), pl.BlockSpec((tm, tn), col(2, 2)),
                 pl.BlockSpec((tm, tn), col(4, 2)), pl.BlockSpec((tm, tn), col(6, 3))]
    if write_cache:
        if n_prev:
            args += list(prev_kv)
            in_specs += [cache_spec(2, n_prev), cache_spec(4, n_prev)]
        cshape = jax.ShapeDtypeStruct((r // seq, n_prev + 1, N_HEADS, seq, HEAD_DIM), F32)
        out_shape += [cshape, cshape]
        out_specs += [cache_spec(2, n_prev + 1), cache_spec(4, n_prev + 1)]
    return pl.pallas_call(
        functools.partial(_in_proj_kernel, seq=seq, write_cache=write_cache, n_prev=n_prev),
        grid=(n_i, D_IN // tn),
        in_specs=in_specs,
        out_specs=out_specs,
        out_shape=out_shape,
        scratch_shapes=[pltpu.VMEM((tm, d), BF16)],
        compiler_params=_cparams(("parallel", "arbitrary"), 56),
        name="in_proj",
    )(*args)


def _softmax_rows(parts):
    m = functools.reduce(jnp.maximum, [jnp.max(s, axis=-1, keepdims=True) for s in parts])
    es = [jnp.exp(s - m) for s in parts]
    den = functools.reduce(jnp.add, [jnp.sum(e, axis=-1, keepdims=True) for e in es])
    inv = 1.0 / den
    return [(e * inv).astype(BF16) for e in es]


def _qk(q, k):
    return lax.dot_general(q, k, (((1,), (1,)), ((), ())), preferred_element_type=F32)


def _ctx_attn_kernel(q_ref, k_ref, v_ref, o_ref):
    scale = HEAD_DIM ** -0.5
    for h in range(N_HEADS):
        sl = slice(h * HEAD_DIM, (h + 1) * HEAD_DIM)
        (p,) = _softmax_rows([_qk(q_ref[:, sl], k_ref[:, sl]) * scale])
        o_ref[:, sl] = jnp.dot(p, v_ref[:, sl], preferred_element_type=F32).astype(BF16)


def _ctx_attn(q, k, v, seq):
    r = q.shape[0]
    spec = pl.BlockSpec((seq, D_ATTN), lambda b: (b, 0))
    return pl.pallas_call(
        _ctx_attn_kernel,
        grid=(r // seq,),
        in_specs=[spec, spec, spec],
        out_specs=spec,
        out_shape=jax.ShapeDtypeStruct((r, D_ATTN), BF16),
        compiler_params=_cparams(("parallel",), 32),
        name="ctx_attn",
    )(q, k, v)


def _nbr_plan(seq):
    rows = seq // GRID_W
    wr = min(WIN_R, rows)
    row_start = np.clip(np.arange(rows) - wr // 2, 0, rows - wr)
    rows_per_blk = 256 // GRID_W
    ranges = []
    for qb in range(rows // rows_per_blk):
        rs = row_start[qb * rows_per_blk:(qb + 1) * rows_per_blk]
        lo = int(rs.min()) * GRID_W // LANES * LANES
        hi = -(-(int(rs.max()) + wr) * GRID_W // LANES) * LANES
        ranges.append((lo, hi))
    return rows, wr, row_start, ranges


def _nbr_attn_kernel(q_ref, k_ref, v_ref, kc_ref, vc_ref, tb_ref, o_ref, bias_scr, *, seq):
    rows, wr, row_start, ranges = _nbr_plan(seq)
    scale = HEAD_DIM ** -0.5
    heads = q_ref.shape[1] // HEAD_DIM
    neg = jnp.full((GRID_W, GRID_W), NEG, F32)

    @pl.when(pl.program_id(1) == 0)
    def _():
        for h in range(heads):
            for qr in range(rows):
                for kp in range(rows // 2):
                    blks = []
                    for kr in (2 * kp, 2 * kp + 1):
                        inside = row_start[qr] <= kr < row_start[qr] + wr
                        blks.append(tb_ref[h, kr - qr + WIN_R - 1] if inside else neg)
                    bias_scr[h, qr * GRID_W:(qr + 1) * GRID_W, kp * LANES:(kp + 1) * LANES] = (
                        jnp.concatenate(blks, axis=1))

    for h in range(heads):
        sl = slice(h * HEAD_DIM, (h + 1) * HEAD_DIM)
        kc = kc_ref[0, 0, h].astype(BF16)
        vc = vc_ref[0, 0, h].astype(BF16)
        for qb, (lo, hi) in enumerate(ranges):
            qs = slice(qb * 256, (qb + 1) * 256)
            q = q_ref[qs, sl]
            s_loc = _qk(q, k_ref[lo:hi, sl]) * scale + bias_scr[h, qs, lo:hi]
            s_ctx = _qk(q, kc) * scale
            p_loc, p_ctx = _softmax_rows([s_loc, s_ctx])
            o = (jnp.dot(p_loc, v_ref[lo:hi, sl], preferred_element_type=F32)
                 + jnp.dot(p_ctx, vc, preferred_element_type=F32))
            o_ref[qs, sl] = o.astype(BF16)


def _nbr_attn(q, k, v, cache_k, cache_v, layer, tb, seq):
    r = q.shape[0]
    past = cache_k.shape[3]
    hp = LANES // HEAD_DIM
    spec = pl.BlockSpec((seq, LANES), lambda g, b: (b, g))
    cspec = pl.BlockSpec((1, 1, hp, past, HEAD_DIM), lambda g, b: (b, layer, g, 0, 0))
    return pl.pallas_call(
        functools.partial(_nbr_attn_kernel, seq=seq),
        grid=(N_HEADS // hp, r // seq),
        in_specs=[spec, spec, spec, cspec, cspec,
                  pl.BlockSpec((hp,) + tb.shape[1:], lambda g, b: (g, 0, 0, 0))],
        out_specs=spec,
        out_shape=jax.ShapeDtypeStruct((r, D_ATTN), BF16),
        scratch_shapes=[pltpu.VMEM((hp, seq, seq), F32)],
        compiler_params=_cparams(("parallel", "arbitrary"), 48),
        name="nbr_attn",
    )(q, k, v, cache_k, cache_v, tb)


def _bias_table(rpb_l):
    col = np.arange(GRID_W)
    col_start = np.clip(col - WIN_C // 2, 0, GRID_W - WIN_C)
    col_in = (col[None, :] >= col_start[:, None]) & (col[None, :] < col_start[:, None] + WIN_C)
    dc_idx = np.clip(col[None, :] - col[:, None] + WIN_C - 1, 0, 2 * WIN_C - 2)
    onehot = (dc_idx.reshape(1, -1) == np.arange(2 * WIN_C - 1)[:, None]).astype(np.float32)
    tb = jnp.einsum('hrc,cq->hrq', rpb_l, onehot, precision=lax.Precision.HIGHEST)
    tb = tb.reshape(rpb_l.shape[0], rpb_l.shape[1], GRID_W, GRID_W)
    return jnp.where(col_in[None, None], tb, NEG).astype(F32)


def _expm1(x):
    u = jnp.exp(x)
    um1 = u - 1.0
    near = um1 * x / jnp.where(u == 1.0, 1.0, jnp.log(u))
    return jnp.where(x < -0.5, um1, jnp.where(u == 1.0, x, near))


def _slot0(n_rows, n_cols, ns):
    return (lax.broadcasted_iota(jnp.int32, (n_rows, n_cols), 0) % SUBLANES) < ns


def _per_slot(vec, n_rows, ns, nc):
    if nc == 1:
        return vec
    w = vec.shape[1] // 2
    return jnp.where(_slot0(n_rows, w, ns), vec[:, :w], vec[:, w:])


def _stack_slots(x, ns, nc):
    if nc == 1:
        return x
    m = _slot0(x.shape[0], x.shape[1], ns)
    zero = jnp.zeros_like(x)
    return jnp.concatenate([jnp.where(m, x, zero), jnp.where(m, zero, x)], axis=1)


def _pick_slot(y, ns, nc):
    if nc == 1:
        return y
    w = y.shape[1] // 2
    return jnp.where(_slot0(y.shape[0], w, ns), y[:, :w], y[:, w:])


def _to_time_major(x_ref, t_scr, row0, seq, ns, nc):
    for k in range(nc):
        for s in range(ns):
            t_scr[pl.ds(row0 + k * ns + s, seq, stride=SUBLANES), :] = (
                x_ref[s * seq:(s + 1) * seq, k * CH:(k + 1) * CH])


def _rows(start, n):
    return pl.ds(pl.multiple_of(start, SUBLANES), n)


def _rglru_kernel(x_ref, g_ref, cw_ref, cb_ref, w_ref, b_ref, lam_ref, h0_ref, o_ref, fin_ref,
                  xt_scr, yt_scr, af_scr, bf_scr, ab_scr, bb_scr, *, seq, tc, ns, nc):
    rows = tc * SUBLANES
    n_chunks = seq // tc
    front = (CONV_W // 2) * SUBLANES
    back = (CONV_W - 1 - CONV_W // 2) * SUBLANES

    xt_scr[0:front, :] = jnp.zeros((front, CH), F32)
    xt_scr[front + seq * SUBLANES:front + seq * SUBLANES + back, :] = jnp.zeros((back, CH), F32)
    _to_time_major(x_ref, xt_scr, front, seq, ns, nc)
    yt_scr[...] = jnp.zeros_like(yt_scr)

    def coefficients(r0, d, a_scr, b_scr):
        xc = _per_slot(cb_ref[...], rows, ns, nc)
        for j in range(CONV_W):
            xc = xc + xt_scr[_rows(r0 + j * SUBLANES, rows), :] * _per_slot(cw_ref[j:j + 1, :], rows, ns, nc)
        gates = (jnp.dot(_stack_slots(xc, ns, nc).astype(BF16), w_ref[d, 0], preferred_element_type=F32)
                 + _per_slot(b_ref[d, 0], rows, ns, nc))
        rg = jax.nn.sigmoid(gates[:, :CH])
        ig = jax.nn.sigmoid(gates[:, CH:])
        log_a = -LRU_C * rg * _per_slot(jax.nn.softplus(-lam_ref[d:d + 1, :]), rows, ns, nc)
        a_scr[...] = jnp.exp(log_a)
        b_scr[...] = jnp.sqrt(-_expm1(2 * log_a)) * (ig * xc)

    def chunk(c, carry):
        rf = pl.multiple_of(c * rows, rows)
        rb = pl.multiple_of((n_chunks - 1 - c) * rows, rows)
        coefficients(rf, 0, af_scr, bf_scr)
        coefficients(rb, 1, ab_scr, bb_scr)

        def step(t, carry):
            h_f, h_b = carry
            tf = _rows(t * SUBLANES, SUBLANES)
            tb = _rows((tc - 1 - t) * SUBLANES, SUBLANES)
            h_f = af_scr[tf, :] * h_f + bf_scr[tf, :]
            h_b = ab_scr[tb, :] * h_b + bb_scr[tb, :]
            bf_scr[tf, :] = h_f
            bb_scr[tb, :] = h_b
            return h_f, h_b

        carry = lax.fori_loop(0, tc, step, carry, unroll=8)
        yt_scr[pl.ds(rf, rows), :] += bf_scr[...]
        yt_scr[pl.ds(rb, rows), :] += bb_scr[...]
        return carry

    h_f, h_b = lax.fori_loop(0, n_chunks, chunk, (h0_ref[0, 0, 0], h0_ref[0, 0, 1]))
    fin_ref[0, 0, 0] = h_f
    fin_ref[0, 0, 1] = h_b
    for k in range(nc):
        for s in range(ns):
            blk = (slice(s * seq, (s + 1) * seq), slice(k * CH, (k + 1) * CH))
            o_ref[blk] = yt_scr[pl.ds(k * ns + s, seq, stride=SUBLANES), :] * jax.nn.gelu(g_ref[blk])


def _rglru(rec, conv_w, conv_b, w_g, b_g, lam, h0, seq, ns, nc):
    r = rec.shape[0]
    n_cb = N_CH // nc
    tc = 128
    wide = nc * CH
    n_pad = (seq + CONV_W - 1) * SUBLANES
    state = pl.BlockSpec((1, 1, 2, SUBLANES, CH), lambda g, c: (g, c, 0, 0, 0))
    return pl.pallas_call(
        functools.partial(_rglru_kernel, seq=seq, tc=tc, ns=ns, nc=nc),
        grid=(r // (ns * seq), n_cb),
        in_specs=[pl.BlockSpec((ns * seq, wide), lambda g, c: (g, c)),
                  pl.BlockSpec((ns * seq, wide), lambda g, c: (g, n_cb + c)),
                  pl.BlockSpec((CONV_W, wide), lambda g, c: (0, c)),
                  pl.BlockSpec((1, wide), lambda g, c: (0, c)),
                  pl.BlockSpec((2, 1, wide, 2 * CH), lambda g, c: (0, c, 0, 0)),
                  pl.BlockSpec((2, 1, 1, 2 * wide), lambda g, c: (0, c, 0, 0)),
                  pl.BlockSpec((2, wide), lambda g, c: (0, c)),
                  state],
        out_specs=[pl.BlockSpec((ns * seq, wide), lambda g, c: (g, c)), state],
        out_shape=[jax.ShapeDtypeStruct((r, D_LRU), F32),
                   jax.ShapeDtypeStruct((r // (ns * seq), n_cb, 2, SUBLANES, CH), F32)],
        scratch_shapes=[pltpu.VMEM((n_pad, CH), F32), pltpu.VMEM((seq * SUBLANES, CH), F32)]
        + [pltpu.VMEM((tc * SUBLANES, CH), F32)] * 4,
        compiler_params=_cparams(("parallel", "parallel"), 48),
        name="rglru",
    )(rec, rec, conv_w, conv_b, w_g, b_g, lam, h0)


def _rglru_params(wa, ba, wi, bi, nc):
    bpc = CH // LRU_BLOCK
    n_cb = N_CH // nc
    eye = jnp.eye(bpc, dtype=F32)

    def dense(w):
        w = w.reshape(2, N_CH, bpc, LRU_BLOCK, LRU_BLOCK)
        return (w[:, :, :, :, None, :] * eye[None, None, :, None, :, None]).reshape(2, N_CH, CH, CH)

    w_g = jnp.concatenate([dense(wa), dense(wi)], axis=-1).reshape(2, n_cb, nc * CH, 2 * CH).astype(BF16)
    b_g = jnp.concatenate([ba.reshape(2, N_CH, CH), bi.reshape(2, N_CH, CH)], axis=-1)
    return w_g, b_g.reshape(2, n_cb, 1, nc * 2 * CH)


def _s5_kernel(u_ref, bb_ref, cc_ref, ar_ref, ai_ref, d_ref, s0r_ref, s0i_ref, y_ref, fr_ref, fi_ref,
               ut_scr, yt_scr, sf_scr, sb_scr, *, seq, tc, ns, nc):
    rows = tc * SUBLANES
    n_chunks = seq // tc
    _to_time_major(u_ref, ut_scr, 0, seq, ns, nc)
    yt_scr[...] = jnp.zeros_like(yt_scr)

    def chunk(c, carry):
        rf = pl.multiple_of(c * rows, rows)
        rb = pl.multiple_of((n_chunks - 1 - c) * rows, rows)
        for d, r0, st_scr in ((0, rf, sf_scr), (1, rb, sb_scr)):
            u = _stack_slots(ut_scr[pl.ds(r0, rows), :], ns, nc)
            st_scr[...] = jnp.dot(u.astype(BF16), bb_ref[d, 0], preferred_element_type=F32)

        def update(st_scr, rr, d, s_re, s_im):
            a_re = ar_ref[d, 0]
            a_im = ai_ref[d, 0]
            n_re = a_re * s_re - a_im * s_im + st_scr[rr, :S5_ST]
            n_im = a_re * s_im + a_im * s_re + st_scr[rr, S5_ST:]
            st_scr[rr, :S5_ST] = n_re
            st_scr[rr, S5_ST:] = n_im
            return n_re, n_im

        def step(t, carry):
            f_re, f_im, b_re, b_im = carry
            f_re, f_im = update(sf_scr, _rows(t * SUBLANES, SUBLANES), 0, f_re, f_im)
            b_re, b_im = update(sb_scr, _rows((tc - 1 - t) * SUBLANES, SUBLANES), 1, b_re, b_im)
            return f_re, f_im, b_re, b_im

        carry = lax.fori_loop(0, tc, step, carry, unroll=2)
        for d, r0, st_scr in ((0, rf, sf_scr), (1, rb, sb_scr)):
            y = jnp.dot(st_scr[...].astype(BF16), cc_ref[d, 0], preferred_element_type=F32)
            yt_scr[pl.ds(r0, rows), :] += _pick_slot(y, ns, nc)
        return carry

    f_re, f_im, b_re, b_im = lax.fori_loop(
        0, n_chunks, chunk, (s0r_ref[0, 0, 0], s0i_ref[0, 0, 0], s0r_ref[0, 0, 1], s0i_ref[0, 0, 1]))
    fr_ref[0, 0, 0] = f_re
    fi_ref[0, 0, 0] = f_im
    fr_ref[0, 0, 1] = b_re
    fi_ref[0, 0, 1] = b_im
    for k in range(nc):
        for s in range(ns):
            blk = (slice(s * seq, (s + 1) * seq), slice(k * CH, (k + 1) * CH))
            y_ref[blk] = (d_ref[:, k * CH:(k + 1) * CH] * u_ref[blk]
                          + yt_scr[pl.ds(k * ns + s, seq, stride=SUBLANES), :])


def _s5(rec, bb, cc, a_re, a_im, d_skip, s0_re, s0_im, seq, ns, nc):
    r = rec.shape[0]
    n_cb = N_CH // nc
    tc = 64
    wide = nc * CH
    state = pl.BlockSpec((1, 1, 2, SUBLANES, S5_ST), lambda g, c: (g, c, 0, 0, 0))
    aspec = pl.BlockSpec((2, 1, SUBLANES, S5_ST), lambda g, c: (0, c, 0, 0))
    st_shape = jax.ShapeDtypeStruct((r // (ns * seq), n_cb, 2, SUBLANES, S5_ST), F32)
    return pl.pallas_call(
        functools.partial(_s5_kernel, seq=seq, tc=tc, ns=ns, nc=nc),
        grid=(r // (ns * seq), n_cb),
        in_specs=[pl.BlockSpec((ns * seq, wide), lambda g, c: (g, 2 * n_cb + c)),
                  pl.BlockSpec((2, 1, wide, 2 * S5_ST), lambda g, c: (0, c, 0, 0)),
                  pl.BlockSpec((2, 1, 2 * S5_ST, wide), lambda g, c: (0, c, 0, 0)),
                  aspec, aspec,
                  pl.BlockSpec((1, wide), lambda g, c: (0, c)),
                  state, state],
        out_specs=[pl.BlockSpec((ns * seq, wide), lambda g, c: (g, c)), state, state],
        out_shape=[jax.ShapeDtypeStruct((r, D_S5), F32), st_shape, st_shape],
        scratch_shapes=[pltpu.VMEM((seq * SUBLANES, CH), F32), pltpu.VMEM((seq * SUBLANES, CH), F32),
                        pltpu.VMEM((tc * SUBLANES, 2 * S5_ST), F32), pltpu.VMEM((tc * SUBLANES, 2 * S5_ST), F32)],
        compiler_params=_cparams(("parallel", "parallel"), 48),
        name="s5",
    )(rec, bb, cc, a_re, a_im, d_skip, s0_re, s0_im)


def _s5_params(lam_re, lam_im, log_dt, b_re, b_im, c_re, c_im, ns, nc):
    dt = jnp.exp(log_dt)[..., None]
    mag = jnp.exp(lam_re * dt)
    abar_re, abar_im = mag * jnp.cos(lam_im * dt), mag * jnp.sin(lam_im * dt)
    den = lam_re * lam_re + lam_im * lam_im
    nr, ni = abar_re - 1, abar_im
    cr = (nr * lam_re + ni * lam_im) / den
    ci = (ni * lam_re - nr * lam_im) / den
    bb_re = cr[..., None] * b_re - ci[..., None] * b_im
    bb_im = cr[..., None] * b_im + ci[..., None] * b_re
    gpc = CH // S5_GROUP
    eye = jnp.eye(gpc, dtype=F32)

    def in_map(b):
        b = b.reshape(2, N_CH, gpc, S5_STATE, S5_GROUP).transpose(0, 1, 2, 4, 3)
        return (b[:, :, :, :, None, :] * eye[None, None, :, None, :, None]).reshape(2, N_CH, CH, S5_ST)

    def out_map(c):
        c = c.reshape(2, N_CH, gpc, S5_GROUP, S5_STATE).transpose(0, 1, 2, 4, 3)
        return (c[:, :, :, :, None, :] * eye[None, None, :, None, :, None]).reshape(2, N_CH, S5_ST, CH)

    bb = jnp.concatenate([in_map(bb_re), in_map(bb_im)], axis=-1).astype(BF16)
    cc = jnp.concatenate([out_map(c_re), -out_map(c_im)], axis=-2).astype(BF16)
    n_cb = N_CH // nc
    bb = bb.reshape(2, n_cb, nc * CH, 2 * S5_ST)
    cc = cc.reshape(2, n_cb, nc, 2 * S5_ST, CH).transpose(0, 1, 3, 2, 4).reshape(2, n_cb, 2 * S5_ST, nc * CH)

    def per_slot(a):
        a = jnp.broadcast_to(a.reshape(2, n_cb, nc, 1, S5_ST), (2, n_cb, nc, ns, S5_ST))
        return a.reshape(2, n_cb, SUBLANES, S5_ST)

    return bb, cc, per_slot(abar_re), per_slot(abar_im)


def _out_proj_kernel(x_ref, oa_ref, ob_ref, y_ref, wg_ref, bg_ref, wo_ref, mod_ref, g_ref, *refs, route):
    if route:
        r_ref, _, x1_ref, h2_ref, route_ref = refs
    else:
        x1_ref, h2_ref = refs
    z = jax.nn.gelu(y_ref[...])
    gl = jnp.dot(z.astype(BF16), wg_ref[...], preferred_element_type=F32) + bg_ref[...]
    oc = z * jax.nn.sigmoid(gl)
    mix = (jnp.dot(oa_ref[...], wo_ref[:D_ATTN, :], preferred_element_type=F32)
           + jnp.dot(ob_ref[...].astype(BF16), wo_ref[D_ATTN:D_ATTN + D_LRU, :], preferred_element_type=F32)
           + jnp.dot(oc.astype(BF16), wo_ref[D_ATTN + D_LRU:, :], preferred_element_type=F32))
    x1 = x_ref[...] + mod_ref[0, 2:3, :] * mix
    x1_ref[...] = x1
    h2 = _rms_mod(x1, g_ref[...], mod_ref[0, 3:4, :], mod_ref[0, 4:5, :])
    h2_ref[...] = h2.astype(h2_ref.dtype)
    if route:
        hi = h2.astype(BF16)
        lo = (h2 - hi.astype(F32)).astype(BF16)
        pr = (jnp.dot(hi, r_ref[...], preferred_element_type=F32)
              + jnp.dot(lo, r_ref[...], preferred_element_type=F32))
        logits = pr + pltpu.roll(pr, LANES - N_EXPERTS, axis=1)
        lane = lax.broadcasted_iota(jnp.int32, logits.shape, 1)
        lg = jnp.where(lane < N_EXPERTS, logits, -jnp.inf)
        m1 = jnp.max(lg, axis=-1, keepdims=True)
        i1 = jnp.min(jnp.where(lg == m1, lane, LANES), axis=-1, keepdims=True)
        lg2 = jnp.where(lane == i1, -jnp.inf, lg)
        m2 = jnp.max(lg2, axis=-1, keepdims=True)
        i2 = jnp.min(jnp.where(lg2 == m2, lane, LANES), axis=-1, keepdims=True)
        e2 = jnp.exp(m2 - m1)
        den = 1.0 + e2
        route_ref[...] = (jnp.where(lane == 0, i1.astype(F32), 0.0) + jnp.where(lane == 1, i2.astype(F32), 0.0)
                          + jnp.where(lane == 2, 1.0 / den, 0.0) + jnp.where(lane == 3, e2 / den, 0.0))


def _out_proj(x2d, oa, ob, y, w_glu_bf, b_glu, w_out_bf, mod, mod_base, rows_per_mod, g, routing):
    r, d = x2d.shape
    tm = 512
    tiles_per_mod = rows_per_mod // tm
    route = routing is not None
    row = lambda n: pl.BlockSpec((tm, n), lambda i: (i, 0))
    full = lambda a: pl.BlockSpec(a.shape, lambda i: (0,) * a.ndim, pipeline_mode=pl.Buffered(1))
    args = [x2d, oa, ob, y, w_glu_bf, b_glu, w_out_bf, mod, g]
    in_specs = [row(d), row(D_ATTN), row(D_LRU), row(D_S5), full(w_glu_bf), full(b_glu), full(w_out_bf),
                pl.BlockSpec((1, N_MOD, d), lambda i: (mod_base + i // tiles_per_mod, 0, 0)), full(g)]
    aliases = {}
    if route:
        router_pad, shared_rows, row_base = routing
        args += [router_pad, shared_rows]
        in_specs += [full(router_pad), pl.BlockSpec(memory_space=pl.ANY)]
        aliases = {len(args) - 1: 1}
        out_shape = [jax.ShapeDtypeStruct((r, d), F32), jax.ShapeDtypeStruct(shared_rows.shape, F32),
                     jax.ShapeDtypeStruct((r, LANES), F32)]
        out_specs = [row(d), pl.BlockSpec((tm, d), lambda i: (i + row_base // tm, 0)), row(LANES)]
    else:
        out_shape = [jax.ShapeDtypeStruct((r, d), F32), jax.ShapeDtypeStruct((r, d), BF16)]
        out_specs = [row(d), row(d)]
    return pl.pallas_call(
        functools.partial(_out_proj_kernel, route=route),
        grid=(r // tm,),
        in_specs=in_specs,
        out_specs=out_specs,
        out_shape=out_shape,
        input_output_aliases=aliases,
        compiler_params=_cparams(("parallel",), 56),
        name="out_proj",
    )(*args)


def _swiglu_part(h, w1, w3, w2):
    a = jnp.dot(h, w1, preferred_element_type=F32)
    b = jnp.dot(h, w3, preferred_element_type=F32)
    return jnp.dot((jax.nn.silu(a) * b).astype(BF16), w2, preferred_element_type=F32)


def _ffn_kernel(h_ref, x_ref, mod_ref, w1_ref, w3_ref, w2_ref, o_ref):
    f = pl.program_id(1)
    part = _swiglu_part(h_ref[...], w1_ref[...], w3_ref[...], w2_ref[...])

    @pl.when(f == 0)
    def _():
        o_ref[...] = part

    @pl.when(f > 0)
    def _():
        o_ref[...] += part

    @pl.when(f == pl.num_programs(1) - 1)
    def _():
        o_ref[...] = x_ref[...] + mod_ref[0, 5:6, :] * o_ref[...]


def _ffn(h2, x1, mod, mod_base, rows_per_mod, w1, w3, w2):
    r, d = x1.shape
    tm, tf = 512, 1408
    tiles_per_mod = rows_per_mod // tm
    return pl.pallas_call(
        _ffn_kernel,
        grid=(r // tm, w1.shape[1] // tf),
        in_specs=[pl.BlockSpec((tm, d), lambda i, f: (i, 0)),
                  pl.BlockSpec((tm, d), lambda i, f: (i, 0), pipeline_mode=pl.Buffered(1)),
                  pl.BlockSpec((1, N_MOD, d), lambda i, f: (mod_base + i // tiles_per_mod, 0, 0)),
                  pl.BlockSpec((d, tf), lambda i, f: (0, f)),
                  pl.BlockSpec((d, tf), lambda i, f: (0, f)),
                  pl.BlockSpec((tf, d), lambda i, f: (f, 0))],
        out_specs=pl.BlockSpec((tm, d), lambda i, f: (i, 0)),
        out_shape=jax.ShapeDtypeStruct((r, d), F32),
        compiler_params=_cparams(("parallel", "arbitrary"), 60),
        name="ffn",
    )(h2, x1, mod, w1, w3, w2)


MOE_TM = 512
MOE_TF = 1408
MOE_TT = 256


def _route_plan(route):
    t = route.shape[0]
    n_pairs = 2 * t
    n_tiles = n_pairs // MOE_TM + N_EXPERTS
    experts = route[:, :2].astype(jnp.int32).reshape(n_pairs)
    onehot = (experts[:, None] == jnp.arange(N_EXPERTS, dtype=jnp.int32)[None]).astype(jnp.int32)
    csum = jnp.cumsum(onehot, axis=0)
    rank = jnp.sum(onehot * csum, axis=1) - 1
    counts = csum[-1]
    padded = (counts + MOE_TM - 1) // MOE_TM * MOE_TM
    ends = jnp.cumsum(padded)
    pos = jnp.sum(onehot * (ends - padded)[None], axis=1) + rank
    src = jnp.zeros((n_tiles * MOE_TM,), jnp.int32).at[pos].set(jnp.arange(n_pairs, dtype=jnp.int32) // 2)
    tile_start = jnp.arange(n_tiles, dtype=jnp.int32) * MOE_TM
    tile_valid = (tile_start < ends[-1]).astype(jnp.int32)
    tile_expert = jnp.sum((tile_start[:, None] >= ends[None]).astype(jnp.int32), axis=1)
    last_expert = jnp.max(jnp.where(counts > 0, jnp.arange(N_EXPERTS, dtype=jnp.int32), 0))
    tile_expert = jnp.where(tile_valid == 1, tile_expert, last_expert)
    return pos, src, tile_expert, tile_valid


def _gather_rows(idx_of_row, n_rows, src_hbm, dst, sem):
    def row(r, carry):
        pltpu.make_async_copy(src_hbm.at[pl.ds(idx_of_row(r), 1)], dst.at[pl.ds(r, 1)], sem).start()
        return carry
    lax.fori_loop(0, n_rows, row, 0, unroll=8)


def _wait_rows(n_rows, src_hbm, dst, sem):
    pltpu.make_async_copy(src_hbm.at[pl.ds(0, n_rows)], dst, sem).wait()


def _moe_experts_kernel(src_ref, texp_ref, valid_ref, h_hbm, w1_ref, w3_ref, w2_ref, o_ref, xbuf, xbf, sem):
    i = pl.program_id(0)
    f = pl.program_id(1)

    def gather(tile):
        _gather_rows(lambda r: src_ref[tile * MOE_TM + r], MOE_TM, h_hbm, xbuf, sem.at[0])

    @pl.when(f == 0)
    def _():
        @pl.when(i == 0)
        def _():
            gather(0)

        _wait_rows(MOE_TM, h_hbm, xbuf, sem.at[0])
        xbf[...] = xbuf[...].astype(BF16)

        @pl.when(i + 1 < pl.num_programs(0))
        def _():
            gather(i + 1)

    valid = valid_ref[i] == 1

    @pl.when(valid)
    def _():
        part = _swiglu_part(xbf[...], w1_ref[0], w3_ref[0], w2_ref[0])

        @pl.when(f == 0)
        def _():
            o_ref[...] = part

        @pl.when(f > 0)
        def _():
            o_ref[...] += part

    @pl.when(jnp.logical_not(valid) & (f == 0))
    def _():
        o_ref[...] = jnp.zeros_like(o_ref)


def _moe_experts(h2, src, tile_expert, tile_valid, w1, w3, w2):
    t, d = h2.shape
    n_tiles = tile_expert.shape[0]
    f_e = w1.shape[2]
    n_f = f_e // MOE_TF

    def wmap(i, f, src, texp, valid):
        return (texp[i], 0, jnp.where(valid[i] == 1, f, n_f - 1))

    def w2map(i, f, src, texp, valid):
        return (texp[i], jnp.where(valid[i] == 1, f, n_f - 1), 0)

    grid_spec = pltpu.PrefetchScalarGridSpec(
        num_scalar_prefetch=3,
        grid=(n_tiles, n_f),
        in_specs=[pl.BlockSpec(memory_space=pl.ANY),
                  pl.BlockSpec((1, d, MOE_TF), wmap),
                  pl.BlockSpec((1, d, MOE_TF), wmap),
                  pl.BlockSpec((1, MOE_TF, d), w2map)],
        out_specs=pl.BlockSpec((MOE_TM, d), lambda i, f, *_: (i, 0)),
        scratch_shapes=[pltpu.VMEM((MOE_TM, d), F32), pltpu.VMEM((MOE_TM, d), BF16),
                        pltpu.SemaphoreType.DMA((1,))])
    return pl.pallas_call(
        _moe_experts_kernel,
        grid_spec=grid_spec,
        out_shape=jax.ShapeDtypeStruct((n_tiles * MOE_TM, d), F32),
        compiler_params=_cparams(("arbitrary", "arbitrary"), 62),
        name="moe_experts",
    )(src, tile_expert, tile_valid, h2, w1, w3, w2)


def _moe_combine_kernel(pos_ref, x_ref, mod_ref, route_ref, y_hbm, o_ref, buf, sem, *, tok_base):
    i = pl.program_id(0)
    slot = i % 2

    def gather(tile, slot):
        for k in range(2):
            _gather_rows(lambda r, k=k: pos_ref[2 * (tok_base + tile * MOE_TT + r) + k], MOE_TT, y_hbm,
                         buf.at[slot, k], sem.at[slot])

    @pl.when(i == 0)
    def _():
        gather(0, 0)

    for k in range(2):
        _wait_rows(MOE_TT, y_hbm, buf.at[slot, k], sem.at[slot])

    @pl.when(i + 1 < pl.num_programs(0))
    def _():
        gather(i + 1, 1 - slot)

    mixed = route_ref[:, 2:3] * buf[slot, 0] + route_ref[:, 3:4] * buf[slot, 1]
    o_ref[...] = x_ref[...] + mod_ref[0, 5:6, :] * mixed


def _moe_combine(pos, x1, mod, mod_base, rows_per_mod, route, y_rows, tok_base):
    r, d = x1.shape
    tiles_per_mod = rows_per_mod // MOE_TT
    grid_spec = pltpu.PrefetchScalarGridSpec(
        num_scalar_prefetch=1,
        grid=(r // MOE_TT,),
        in_specs=[pl.BlockSpec((MOE_TT, d), lambda i, *_: (i, 0)),
                  pl.BlockSpec((1, N_MOD, d), lambda i, *_: (mod_base + i // tiles_per_mod, 0, 0)),
                  pl.BlockSpec((MOE_TT, LANES), lambda i, *_: (i, 0)),
                  pl.BlockSpec(memory_space=pl.ANY)],
        out_specs=pl.BlockSpec((MOE_TT, d), lambda i, *_: (i, 0)),
        scratch_shapes=[pltpu.VMEM((2, 2, MOE_TT, d), F32), pltpu.SemaphoreType.DMA((2,))])
    return pl.pallas_call(
        functools.partial(_moe_combine_kernel, tok_base=tok_base),
        grid_spec=grid_spec,
        out_shape=jax.ShapeDtypeStruct((r, d), F32),
        compiler_params=_cparams(("arbitrary",), 32),
        name="moe_combine",
    )(pos, x1, mod, route, y_rows)


def _slot_states(s, ns, nc, width):
    ng = s.shape[0] // ns
    s = s.reshape(ng, ns, 2, N_CH // nc, nc, width).transpose(0, 3, 2, 4, 1, 5)
    return s.reshape(ng, N_CH // nc, 2, SUBLANES, width)


def _unslot_states(s, ns, nc, width):
    ng = s.shape[0]
    s = s.reshape(ng, N_CH // nc, 2, nc, ns, width).transpose(0, 4, 2, 1, 3, 5)
    return s.reshape(ng * ns, 2, N_CH * width)


def _token_mixer(x2d, batch, seq, mod, mod_base, rows_per_mod, p, kv_ctx, lru_h0, s5_h0, prev_kv=None,
                 routing=None):
    is_ctx = kv_ctx is None
    ns = min(batch, SUBLANES)
    nc = SUBLANES // ns
    outs = _in_proj(x2d, p['g_mix'], mod, mod_base, rows_per_mod, p['w_in'], p['q_g'], p['k_g'], p['ones_bd'],
                    seq, is_ctx, prev_kv)
    q, k, v, rec = outs[:4]
    if is_ctx:
        oa = _ctx_attn(q, k, v, seq)
        lru_h0 = jnp.zeros((batch, 2, D_LRU), F32)
        s5_h0 = (jnp.zeros((batch, 2, N_S5_GROUPS * S5_STATE), F32),) * 2
    else:
        oa = _nbr_attn(q, k, v, kv_ctx[0], kv_ctx[1], kv_ctx[2], p['tb'], seq)
    lru_w, lru_b = _rglru_params(p['lru_wa'], p['lru_ba'], p['lru_wi'], p['lru_bi'], nc)
    ob, lru_fin = _rglru(rec, p['conv_w'], p['conv_b'], lru_w, lru_b, p['lru_lam'],
                         _slot_states(lru_h0, ns, nc, CH), seq, ns, nc)
    s5_bb, s5_cc, s5_are, s5_aim = _s5_params(*p['s5'], ns, nc)
    y, fin_re, fin_im = _s5(rec, s5_bb, s5_cc, s5_are, s5_aim, p['d_skip'],
                            _slot_states(s5_h0[0], ns, nc, S5_ST), _slot_states(s5_h0[1], ns, nc, S5_ST),
                            seq, ns, nc)
    res = _out_proj(x2d, oa, ob, y, p['w_glu'], p['b_glu'], p['w_out'], mod, mod_base, rows_per_mod,
                    p['g_ffn'], routing)
    if not is_ctx:
        return res, None
    state = (outs[4], outs[5], _unslot_states(lru_fin, ns, nc, CH),
             _unslot_states(fin_re, ns, nc, S5_ST).reshape(batch, 2, N_S5_GROUPS, S5_STATE),
             _unslot_states(fin_im, ns, nc, S5_ST).reshape(batch, 2, N_S5_GROUPS, S5_STATE))
    return res, state


def kernel(x_prompt, x_sample, c, cache_k, cache_v, state_lru, state_s5_re, state_s5_im, c_ctx, norm_mix_g, norm_ffn_g, w_mod, b_mod, w_in, w_out, q_norm_g, k_norm_g, rpb, lru_conv_w, lru_conv_b, lru_wa, lru_ba, lru_wi, lru_bi, lru_lam, s5_lam_re, s5_lam_im, s5_log_dt, s5_b_re, s5_b_im, s5_c_re, s5_c_im, s5_d, s5_w_glu, s5_b_glu, ffn_w1, ffn_w3, ffn_w2, moe_router, moe_w1, moe_w3, moe_w2):
    batch, seq, d = x_prompt.shape
    dec_batch, dec_seq, _ = x_sample.shape
    depth = w_in.shape[0]
    assert dec_batch + 1 <= SUBLANES and batch % SUBLANES == 0 and dec_batch in (4, 8)

    cvecs = jnp.concatenate([c_ctx[None], c, jnp.zeros((SUBLANES - 1 - dec_batch, d), F32)], axis=0)
    mods = _adaln(cvecs, w_mod, b_mod).reshape(depth, SUBLANES, N_MOD, d)

    heads_per_tile = 512 // HEAD_DIM
    ones_bd = jnp.asarray(np.kron(np.eye(heads_per_tile // 2), np.ones((HEAD_DIM, HEAD_DIM))), BF16)

    xp = x_prompt.reshape(batch * seq, d)
    xs = x_sample.reshape(dec_batch * dec_seq, d)
    n_tok = (batch * seq, dec_batch * dec_seq)
    kv, lrus, s5rs, s5is = None, [], [], []
    for l in range(depth):
        p = {
            'g_mix': norm_mix_g[l][None], 'g_ffn': norm_ffn_g[l][None],
            'w_in': w_in[l].astype(BF16), 'w_out': w_out[l].astype(BF16),
            'q_g': jnp.tile(q_norm_g[l], heads_per_tile)[None], 'k_g': jnp.tile(k_norm_g[l], heads_per_tile)[None],
            'ones_bd': ones_bd, 'tb': _bias_table(rpb[l]),
            'conv_w': lru_conv_w[l], 'conv_b': lru_conv_b[l][None],
            'lru_wa': lru_wa[l], 'lru_ba': lru_ba[l], 'lru_wi': lru_wi[l], 'lru_bi': lru_bi[l], 'lru_lam': lru_lam[l],
            's5': (s5_lam_re[l], s5_lam_im[l], s5_log_dt[l], s5_b_re[l], s5_b_im[l], s5_c_re[l], s5_c_im[l]),
            'd_skip': s5_d[l][None], 'w_glu': s5_w_glu[l].astype(BF16), 'b_glu': s5_b_glu[l][None],
        }
        j = l // 2
        dense = l % 2 == 0
        route_p = route_s = None
        if not dense:
            r_hi = moe_router[j].astype(BF16)
            r_lo = (moe_router[j] - r_hi.astype(F32)).astype(BF16)
            router = jnp.pad(jnp.concatenate([r_hi, r_lo], axis=1), ((0, 0), (0, LANES - 2 * N_EXPERTS)))
            route_p = (router, jnp.zeros((sum(n_tok), d), F32), 0)
        res_p, (k_c, v_c, lru_c, s5r_c, s5i_c) = _token_mixer(
            xp, batch, seq, mods[l], 0, batch * seq, p, None, None, None, prev_kv=kv, routing=route_p)
        kv = (k_c, v_c)
        lrus.append(lru_c)
        s5rs.append(s5r_c)
        s5is.append(s5i_c)
        if not dense:
            route_s = (router, res_p[1], n_tok[0])
        res_s, _ = _token_mixer(
            xs, dec_batch, dec_seq, mods[l], 1, dec_seq, p, (cache_k, cache_v, l), state_lru[:, l],
            (state_s5_re[:, l].reshape(dec_batch, 2, -1), state_s5_im[:, l].reshape(dec_batch, 2, -1)),
            routing=route_s)
        if dense:
            w = (ffn_w1[j].astype(BF16), ffn_w3[j].astype(BF16), ffn_w2[j].astype(BF16))
            xp = _ffn(res_p[1], res_p[0], mods[l], 0, batch * seq, *w)
            xs = _ffn(res_s[1], res_s[0], mods[l], 1, dec_seq, *w)
        else:
            h2 = res_s[1]
            pos, src, tile_expert, tile_valid = _route_plan(jnp.concatenate([res_p[2], res_s[2]], axis=0))
            y_rows = _moe_experts(h2, src, tile_expert, tile_valid,
                                  moe_w1[j].astype(BF16), moe_w3[j].astype(BF16), moe_w2[j].astype(BF16))
            xp = _moe_combine(pos, res_p[0], mods[l], 0, batch * seq, res_p[2], y_rows, 0)
            xs = _moe_combine(pos, res_s[0], mods[l], 1, dec_seq, res_s[2], y_rows, batch * seq)
    return (xp.reshape(batch, seq, d), xs.reshape(dec_batch, dec_seq, d), kv[0], kv[1],
            jnp.stack(lrus, axis=1), jnp.stack(s5rs, axis=1), jnp.stack(s5is, axis=1))
```

```python
import functools

import numpy as np
import jax
import jax.numpy as jnp
from jax import lax
from jax.experimental import pallas as pl
from jax.experimental.pallas import tpu as pltpu

F32 = jnp.float32
BF16 = jnp.bfloat16

D_MODEL = 2048
N_HEADS = 16
HEAD_DIM = 64
D_ATTN = N_HEADS * HEAD_DIM
GRID_W = 64
WIN_R = 8
WIN_C = 16
D_LRU = 512
LRU_BLOCK = 64
CONV_W = 4
LRU_C = 8.0
D_S5 = 512
S5_GROUP = 16
N_S5_GROUPS = 32
S5_STATE = 64
D_REC = 2 * D_LRU + D_S5
D_IN = 3 * D_ATTN + D_REC
N_MOD = 6
N_EXPERTS = 8
EPS = 1e-6
NEG = -1e30

LANES = 128
SUBLANES = 8
MIB = 1024 * 1024

CH = 128
N_CH = D_LRU // CH
S5_ST = (CH // S5_GROUP) * S5_STATE


def _cparams(sem, vmem_mib):
    return pltpu.CompilerParams(dimension_semantics=sem, vmem_limit_bytes=vmem_mib * MIB)


def _adaln_kernel(c_ref, w_ref, b_ref, o_ref):
    s = jax.nn.silu(c_ref[...]).astype(BF16)
    o_ref[0] = jnp.dot(s, w_ref[0].astype(BF16), preferred_element_type=F32) + b_ref[0]


def _adaln(cvecs, w_mod, b_mod):
    depth, d, n = w_mod.shape
    tn = 1024
    return pl.pallas_call(
        _adaln_kernel,
        grid=(depth, n // tn),
        in_specs=[pl.BlockSpec((SUBLANES, d), lambda l, j: (0, 0)),
                  pl.BlockSpec((1, d, tn), lambda l, j: (l, 0, j)),
                  pl.BlockSpec((1, 1, tn), lambda l, j: (l, 0, j))],
        out_specs=pl.BlockSpec((1, SUBLANES, tn), lambda l, j: (l, 0, j)),
        out_shape=jax.ShapeDtypeStruct((depth, SUBLANES, n), F32),
        compiler_params=_cparams(("parallel", "parallel"), 40),
        name="adaln",
    )(cvecs, w_mod, b_mod.reshape(depth, 1, n))


def _rms_mod(x, g, shift, scale):
    xf = x * lax.rsqrt(jnp.mean(x * x, axis=-1, keepdims=True) + EPS)
    return (xf * g) * (1 + scale) + shift


def _in_proj_kernel(x_ref, g_ref, mod_ref, w_ref, qg_ref, kg_ref, ones_ref, *refs, seq, write_cache, n_prev):
    kp_ref = vp_ref = None
    if write_cache and n_prev:
        kp_ref, vp_ref, *refs = refs
    if write_cache:
        q_ref, k_ref, v_ref, rec_ref, kc_ref, vc_ref, h_scr = refs
    else:
        q_ref, k_ref, v_ref, rec_ref, h_scr = refs
    j = pl.program_id(1)
    tm, tn = q_ref.shape
    heads = tn // HEAD_DIM

    @pl.when(j == 0)
    def _():
        h_scr[...] = _rms_mod(x_ref[...], g_ref[...], mod_ref[0, 0:1, :], mod_ref[0, 1:2, :]).astype(BF16)

    y = jnp.dot(h_scr[...], w_ref[...], preferred_element_type=F32)

    def to_cache(c_ref, prev_ref, val):
        if prev_ref is not None:
            c_ref[:, :n_prev] = prev_ref[...]
        for b in range(tm // seq):
            for h in range(heads):
                c_ref[b, n_prev, h, :, :] = val[b * seq:(b + 1) * seq, h * HEAD_DIM:(h + 1) * HEAD_DIM]

    @pl.when(j < 4)
    def _():
        y2 = y * y
        hi = y2.astype(BF16)
        lo = (y2 - hi.astype(F32)).astype(BF16)
        half = ones_ref.shape[0]
        ss = jnp.concatenate(
            [jnp.dot(hi[:, c:c + half], ones_ref[...], preferred_element_type=F32)
             + jnp.dot(lo[:, c:c + half], ones_ref[...], preferred_element_type=F32) for c in range(0, tn, half)],
            axis=1)
        gain = jnp.where(j < 2, qg_ref[...], kg_ref[...])
        yn = (y * lax.rsqrt(ss * (1.0 / HEAD_DIM) + EPS)) * gain

        @pl.when(j < 2)
        def _():
            q_ref[...] = yn.astype(BF16)

        @pl.when(j >= 2)
        def _():
            k_ref[...] = yn.astype(BF16)
            if write_cache:
                to_cache(kc_ref, kp_ref, yn)

    @pl.when((j >= 4) & (j < 6))
    def _():
        v_ref[...] = y.astype(BF16)
        if write_cache:
            to_cache(vc_ref, vp_ref, y)

    @pl.when(j >= 6)
    def _():
        rec_ref[...] = y


def _in_proj(x2d, g, mod, mod_base, rows_per_mod, w_bf, qg, kg, ones_bd, seq, write_cache, prev_kv):
    r, d = x2d.shape
    tm, tn = 512, 512
    n_i = r // tm
    tiles_per_mod = rows_per_mod // tm
    bpt = tm // seq
    hpt = tn // HEAD_DIM
    n_prev = prev_kv[0].shape[1] if prev_kv else 0

    def col(lo, n):
        return lambda i, j: (i, jnp.clip(j - lo, 0, n - 1))

    def cache_spec(lo, n_layers):
        return pl.BlockSpec((bpt, n_layers, hpt, seq, HEAD_DIM), lambda i, j: (i, 0, jnp.clip(j - lo, 0, 1), 0, 0))

    args = [x2d, g, mod, w_bf, qg, kg, ones_bd]
    in_specs = [pl.BlockSpec((tm, d), lambda i, j: (i, 0)),
                pl.BlockSpec((1, d), lambda i, j: (0, 0)),
                pl.BlockSpec((1, N_MOD, d), lambda i, j: (mod_base + i // tiles_per_mod, 0, 0)),
                pl.BlockSpec((d, tn), lambda i, j: (0, j)),
                pl.BlockSpec((1, tn), lambda i, j: (0, 0)),
                pl.BlockSpec((1, tn), lambda i, j: (0, 0)),
                pl.BlockSpec(ones_bd.shape, lambda i, j: (0, 0))]
    out_shape = [jax.ShapeDtypeStruct((r, D_ATTN), BF16)] * 3 + [jax.ShapeDtypeStruct((r, D_REC), F32)]
    out_specs = [pl.BlockSpec((tm, tn), col(0, 2)), pl.BlockSpec((tm, tn), col(2, 2)),
                 pl.BlockSpec((tm, tn), col(4, 2)), pl.BlockSpec((tm, tn), col(6, 3))]
    if write_cache:
        if n_prev:
            args += list(prev_kv)
            in_specs += [cache_spec(2, n_prev), cache_spec(4, n_prev)]
        cshape = jax.ShapeDtypeStruct((r // seq, n_prev + 1, N_HEADS, seq, HEAD_DIM), F32)
        out_shape += [cshape, cshape]
        out_specs += [cache_spec(2, n_prev + 1), cache_spec(4, n_prev + 1)]
    return pl.pallas_call(
        functools.partial(_in_proj_kernel, seq=seq, write_cache=write_cache, n_prev=n_prev),
        grid=(n_i, D_IN // tn),
        in_specs=in_specs,
        out_specs=out_specs,
        out_shape=out_shape,
        scratch_shapes=[pltpu.VMEM((tm, d), BF16)],
        compiler_params=_cparams(("parallel", "arbitrary"), 56),
        name="in_proj",
    )(*args)


def _qk(q, k):
    return lax.dot_general(q, k, (((1,), (1,)), ((), ())), preferred_element_type=F32)


def _head_lanes(n_rows, first):
    lane = lax.broadcasted_iota(jnp.int32, (n_rows, LANES), 1)
    return (lane < HEAD_DIM) if first else (lane >= HEAD_DIM)


def _own(x, first):
    return jnp.where(_head_lanes(x.shape[0], first), x, jnp.zeros_like(x))


def _values_and_ones(v, first):
    ones = jnp.where(_head_lanes(v.shape[0], first), 1.0, 0.0).astype(BF16)
    return jnp.concatenate([_own(v, first), ones], axis=1)


def _exp_rows(parts):
    m = functools.reduce(jnp.maximum, [jnp.max(s, axis=-1, keepdims=True) for s in parts])
    return [jnp.exp(s - m).astype(BF16) for s in parts]


def _ctx_attn_kernel(q_ref, k_ref, v_ref, o_ref):
    scale = HEAD_DIM ** -0.5
    for hp in range(N_HEADS * HEAD_DIM // LANES):
        sl = slice(hp * LANES, (hp + 1) * LANES)
        q, k, v = q_ref[:, sl], k_ref[:, sl], v_ref[:, sl]
        nd = None
        for first in (True, False):
            (e,) = _exp_rows([_qk(_own(q, first), k) * scale])
            part = jnp.dot(e, _values_and_ones(v, first), preferred_element_type=F32)
            nd = part if nd is None else nd + part
        o_ref[:, sl] = (nd[:, :LANES] / nd[:, LANES:]).astype(BF16)


def _ctx_attn(q, k, v, seq):
    r = q.shape[0]
    spec = pl.BlockSpec((seq, D_ATTN), lambda b: (b, 0))
    return pl.pallas_call(
        _ctx_attn_kernel,
        grid=(r // seq,),
        in_specs=[spec, spec, spec],
        out_specs=spec,
        out_shape=jax.ShapeDtypeStruct((r, D_ATTN), BF16),
        compiler_params=_cparams(("parallel",), 32),
        name="ctx_attn",
    )(q, k, v)


def _nbr_plan(seq):
    rows = seq // GRID_W
    wr = min(WIN_R, rows)
    row_start = np.clip(np.arange(rows) - wr // 2, 0, rows - wr)
    rows_per_blk = 256 // GRID_W
    ranges = []
    for qb in range(rows // rows_per_blk):
        rs = row_start[qb * rows_per_blk:(qb + 1) * rows_per_blk]
        lo = int(rs.min()) * GRID_W // LANES * LANES
        hi = -(-(int(rs.max()) + wr) * GRID_W // LANES) * LANES
        ranges.append((lo, hi))
    return rows, wr, row_start, ranges


def _nbr_attn_kernel(q_ref, k_ref, v_ref, kc_ref, vc_ref, tb_ref, o_ref, bias_scr, *, seq):
    rows, wr, row_start, ranges = _nbr_plan(seq)
    scale = HEAD_DIM ** -0.5
    heads = q_ref.shape[1] // HEAD_DIM
    neg = jnp.full((GRID_W, GRID_W), NEG, F32)

    @pl.when(pl.program_id(1) == 0)
    def _():
        for h in range(heads):
            for qr in range(rows):
                for kp in range(rows // 2):
                    blks = []
                    for kr in (2 * kp, 2 * kp + 1):
                        inside = row_start[qr] <= kr < row_start[qr] + wr
                        blks.append(tb_ref[h, kr - qr + WIN_R - 1] if inside else neg)
                    bias_scr[h, qr * GRID_W:(qr + 1) * GRID_W, kp * LANES:(kp + 1) * LANES] = (
                        jnp.concatenate(blks, axis=1))

    kc = jnp.concatenate([kc_ref[0, 0, h] for h in range(heads)], axis=1).astype(BF16)
    vc = jnp.concatenate([vc_ref[0, 0, h] for h in range(heads)], axis=1).astype(BF16)
    for qb, (lo, hi) in enumerate(ranges):
        qs = slice(qb * 256, (qb + 1) * 256)
        q = q_ref[qs, :]
        nd = None
        for h, first in enumerate((True, False)):
            qh = _own(q, first)
            s_loc = _qk(qh, k_ref[lo:hi, :]) * scale + bias_scr[h, qs, lo:hi]
            s_ctx = _qk(qh, kc) * scale
            e_loc, e_ctx = _exp_rows([s_loc, s_ctx])
            part = (jnp.dot(e_loc, _values_and_ones(v_ref[lo:hi, :], first), preferred_element_type=F32)
                    + jnp.dot(e_ctx, _values_and_ones(vc, first), preferred_element_type=F32))
            nd = part if nd is None else nd + part
        o_ref[qs, :] = (nd[:, :LANES] / nd[:, LANES:]).astype(BF16)


def _nbr_attn(q, k, v, cache_k, cache_v, layer, tb, seq):
    r = q.shape[0]
    past = cache_k.shape[3]
    hp = LANES // HEAD_DIM
    spec = pl.BlockSpec((seq, LANES), lambda g, b: (b, g))
    cspec = pl.BlockSpec((1, 1, hp, past, HEAD_DIM), lambda g, b: (b, layer, g, 0, 0))
    return pl.pallas_call(
        functools.partial(_nbr_attn_kernel, seq=seq),
        grid=(N_HEADS // hp, r // seq),
        in_specs=[spec, spec, spec, cspec, cspec,
                  pl.BlockSpec((hp,) + tb.shape[1:], lambda g, b: (g, 0, 0, 0))],
        out_specs=spec,
        out_shape=jax.ShapeDtypeStruct((r, D_ATTN), BF16),
        scratch_shapes=[pltpu.VMEM((hp, seq, seq), F32)],
        compiler_params=_cparams(("parallel", "arbitrary"), 48),
        name="nbr_attn",
    )(q, k, v, cache_k, cache_v, tb)


def _bias_table(rpb_l):
    col = np.arange(GRID_W)
    col_start = np.clip(col - WIN_C // 2, 0, GRID_W - WIN_C)
    col_in = (col[None, :] >= col_start[:, None]) & (col[None, :] < col_start[:, None] + WIN_C)
    dc_idx = np.clip(col[None, :] - col[:, None] + WIN_C - 1, 0, 2 * WIN_C - 2)
    onehot = (dc_idx.reshape(1, -1) == np.arange(2 * WIN_C - 1)[:, None]).astype(np.float32)
    tb = jnp.einsum('hrc,cq->hrq', rpb_l, onehot, precision=lax.Precision.HIGHEST)
    tb = tb.reshape(rpb_l.shape[0], rpb_l.shape[1], GRID_W, GRID_W)
    return jnp.where(col_in[None, None], tb, NEG).astype(F32)


def _expm1(x):
    u = jnp.exp(x)
    um1 = u - 1.0
    near = um1 * x / jnp.where(u == 1.0, 1.0, jnp.log(u))
    return jnp.where(x < -0.5, um1, jnp.where(u == 1.0, x, near))


def _slot0(n_rows, n_cols, ns):
    return (lax.broadcasted_iota(jnp.int32, (n_rows, n_cols), 0) % SUBLANES) < ns


def _per_slot(vec, n_rows, ns, nc):
    if nc == 1:
        return vec
    w = vec.shape[1] // 2
    return jnp.where(_slot0(n_rows, w, ns), vec[:, :w], vec[:, w:])


def _stack_slots(x, ns, nc):
    if nc == 1:
        return x
    m = _slot0(x.shape[0], x.shape[1], ns)
    zero = jnp.zeros_like(x)
    return jnp.concatenate([jnp.where(m, x, zero), jnp.where(m, zero, x)], axis=1)


def _pick_slot(y, ns, nc):
    if nc == 1:
        return y
    w = y.shape[1] // 2
    return jnp.where(_slot0(y.shape[0], w, ns), y[:, :w], y[:, w:])


def _to_time_major(x_ref, t_scr, row0, seq, ns, nc):
    for k in range(nc):
        for s in range(ns):
            t_scr[pl.ds(row0 + k * ns + s, seq, stride=SUBLANES), :] = (
                x_ref[s * seq:(s + 1) * seq, k * CH:(k + 1) * CH])


def _rows(start, n):
    return pl.ds(pl.multiple_of(start, SUBLANES), n)


def _rglru_kernel(x_ref, g_ref, cw_ref, cb_ref, w_ref, b_ref, lam_ref, h0_ref, o_ref, fin_ref,
                  xt_scr, yt_scr, af_scr, bf_scr, ab_scr, bb_scr, *, seq, tc, ns, nc):
    rows = tc * SUBLANES
    n_chunks = seq // tc
    front = (CONV_W // 2) * SUBLANES
    back = (CONV_W - 1 - CONV_W // 2) * SUBLANES

    xt_scr[0:front, :] = jnp.zeros((front, CH), F32)
    xt_scr[front + seq * SUBLANES:front + seq * SUBLANES + back, :] = jnp.zeros((back, CH), F32)
    _to_time_major(x_ref, xt_scr, front, seq, ns, nc)
    yt_scr[...] = jnp.zeros_like(yt_scr)

    def coefficients(r0, d, a_scr, b_scr):
        xc = _per_slot(cb_ref[...], rows, ns, nc)
        for j in range(CONV_W):
            xc = xc + xt_scr[_rows(r0 + j * SUBLANES, rows), :] * _per_slot(cw_ref[j:j + 1, :], rows, ns, nc)
        gates = (jnp.dot(_stack_slots(xc, ns, nc).astype(BF16), w_ref[d, 0], preferred_element_type=F32)
                 + _per_slot(b_ref[d, 0], rows, ns, nc))
        rg = jax.nn.sigmoid(gates[:, :CH])
        ig = jax.nn.sigmoid(gates[:, CH:])
        log_a = -LRU_C * rg * _per_slot(jax.nn.softplus(-lam_ref[d:d + 1, :]), rows, ns, nc)
        a_scr[...] = jnp.exp(log_a)
        b_scr[...] = jnp.sqrt(-_expm1(2 * log_a)) * (ig * xc)

    def chunk(c, carry):
        rf = pl.multiple_of(c * rows, rows)
        rb = pl.multiple_of((n_chunks - 1 - c) * rows, rows)
        coefficients(rf, 0, af_scr, bf_scr)
        coefficients(rb, 1, ab_scr, bb_scr)

        def step(t, carry):
            h_f, h_b = carry
            tf = _rows(t * SUBLANES, SUBLANES)
            tb = _rows((tc - 1 - t) * SUBLANES, SUBLANES)
            h_f = af_scr[tf, :] * h_f + bf_scr[tf, :]
            h_b = ab_scr[tb, :] * h_b + bb_scr[tb, :]
            bf_scr[tf, :] = h_f
            bb_scr[tb, :] = h_b
            return h_f, h_b

        carry = lax.fori_loop(0, tc, step, carry, unroll=8)
        yt_scr[pl.ds(rf, rows), :] += bf_scr[...]
        yt_scr[pl.ds(rb, rows), :] += bb_scr[...]
        return carry

    h_f, h_b = lax.fori_loop(0, n_chunks, chunk, (h0_ref[0, 0, 0], h0_ref[0, 0, 1]))
    fin_ref[0, 0, 0] = h_f
    fin_ref[0, 0, 1] = h_b
    for k in range(nc):
        for s in range(ns):
            blk = (slice(s * seq, (s + 1) * seq), slice(k * CH, (k + 1) * CH))
            o_ref[blk] = yt_scr[pl.ds(k * ns + s, seq, stride=SUBLANES), :] * jax.nn.gelu(g_ref[blk])


def _rglru(rec, conv_w, conv_b, w_g, b_g, lam, h0, seq, ns, nc):
    r = rec.shape[0]
    n_cb = N_CH // nc
    tc = 128
    wide = nc * CH
    n_pad = (seq + CONV_W - 1) * SUBLANES
    state = pl.BlockSpec((1, 1, 2, SUBLANES, CH), lambda g, c: (g, c, 0, 0, 0))
    return pl.pallas_call(
        functools.partial(_rglru_kernel, seq=seq, tc=tc, ns=ns, nc=nc),
        grid=(r // (ns * seq), n_cb),
        in_specs=[pl.BlockSpec((ns * seq, wide), lambda g, c: (g, c)),
                  pl.BlockSpec((ns * seq, wide), lambda g, c: (g, n_cb + c)),
                  pl.BlockSpec((CONV_W, wide), lambda g, c: (0, c)),
                  pl.BlockSpec((1, wide), lambda g, c: (0, c)),
                  pl.BlockSpec((2, 1, wide, 2 * CH), lambda g, c: (0, c, 0, 0)),
                  pl.BlockSpec((2, 1, 1, 2 * wide), lambda g, c: (0, c, 0, 0)),
                  pl.BlockSpec((2, wide), lambda g, c: (0, c)),
                  state],
        out_specs=[pl.BlockSpec((ns * seq, wide), lambda g, c: (g, c)), state],
        out_shape=[jax.ShapeDtypeStruct((r, D_LRU), F32),
                   jax.ShapeDtypeStruct((r // (ns * seq), n_cb, 2, SUBLANES, CH), F32)],
        scratch_shapes=[pltpu.VMEM((n_pad, CH), F32), pltpu.VMEM((seq * SUBLANES, CH), F32)]
        + [pltpu.VMEM((tc * SUBLANES, CH), F32)] * 4,
        compiler_params=_cparams(("parallel", "parallel"), 48),
        name="rglru",
    )(rec, rec, conv_w, conv_b, w_g, b_g, lam, h0)


def _rglru_params(wa, ba, wi, bi, nc):
    bpc = CH // LRU_BLOCK
    n_cb = N_CH // nc
    eye = jnp.eye(bpc, dtype=F32)

    def dense(w):
        w = w.reshape(2, N_CH, bpc, LRU_BLOCK, LRU_BLOCK)
        return (w[:, :, :, :, None, :] * eye[None, None, :, None, :, None]).reshape(2, N_CH, CH, CH)

    w_g = jnp.concatenate([dense(wa), dense(wi)], axis=-1).reshape(2, n_cb, nc * CH, 2 * CH).astype(BF16)
    b_g = jnp.concatenate([ba.reshape(2, N_CH, CH), bi.reshape(2, N_CH, CH)], axis=-1)
    return w_g, b_g.reshape(2, n_cb, 1, nc * 2 * CH)


def _s5_kernel(u_ref, bb_ref, cc_ref, ar_ref, ai_ref, d_ref, s0r_ref, s0i_ref, y_ref, fr_ref, fi_ref,
               ut_scr, yt_scr, sf_scr, sb_scr, *, seq, tc, ns, nc):
    rows = tc * SUBLANES
    n_chunks = seq // tc
    _to_time_major(u_ref, ut_scr, 0, seq, ns, nc)
    yt_scr[...] = jnp.zeros_like(yt_scr)

    def chunk(c, carry):
        rf = pl.multiple_of(c * rows, rows)
        rb = pl.multiple_of((n_chunks - 1 - c) * rows, rows)
        for d, r0, st_scr in ((0, rf, sf_scr), (1, rb, sb_scr)):
            u = _stack_slots(ut_scr[pl.ds(r0, rows), :], ns, nc)
            st_scr[...] = jnp.dot(u.astype(BF16), bb_ref[d, 0], preferred_element_type=F32)

        def update(st_scr, rr, d, s_re, s_im):
            a_re = ar_ref[d, 0]
            a_im = ai_ref[d, 0]
            n_re = a_re * s_re - a_im * s_im + st_scr[rr, :S5_ST]
            n_im = a_re * s_im + a_im * s_re + st_scr[rr, S5_ST:]
            st_scr[rr, :S5_ST] = n_re
            st_scr[rr, S5_ST:] = n_im
            return n_re, n_im

        def step(t, carry):
            f_re, f_im, b_re, b_im = carry
            f_re, f_im = update(sf_scr, _rows(t * SUBLANES, SUBLANES), 0, f_re, f_im)
            b_re, b_im = update(sb_scr, _rows((tc - 1 - t) * SUBLANES, SUBLANES), 1, b_re, b_im)
            return f_re, f_im, b_re, b_im

        carry = lax.fori_loop(0, tc, step, carry, unroll=2)
        for d, r0, st_scr in ((0, rf, sf_scr), (1, rb, sb_scr)):
            y = jnp.dot(st_scr[...].astype(BF16), cc_ref[d, 0], preferred_element_type=F32)
            yt_scr[pl.ds(r0, rows), :] += _pick_slot(y, ns, nc)
        return carry

    f_re, f_im, b_re, b_im = lax.fori_loop(
        0, n_chunks, chunk, (s0r_ref[0, 0, 0], s0i_ref[0, 0, 0], s0r_ref[0, 0, 1], s0i_ref[0, 0, 1]))
    fr_ref[0, 0, 0] = f_re
    fi_ref[0, 0, 0] = f_im
    fr_ref[0, 0, 1] = b_re
    fi_ref[0, 0, 1] = b_im
    for k in range(nc):
        for s in range(ns):
            blk = (slice(s * seq, (s + 1) * seq), slice(k * CH, (k + 1) * CH))
            y_ref[blk] = (d_ref[:, k * CH:(k + 1) * CH] * u_ref[blk]
                          + yt_scr[pl.ds(k * ns + s, seq, stride=SUBLANES), :])


def _s5(rec, bb, cc, a_re, a_im, d_skip, s0_re, s0_im, seq, ns, nc):
    r = rec.shape[0]
    n_cb = N_CH // nc
    tc = 64
    wide = nc * CH
    state = pl.BlockSpec((1, 1, 2, SUBLANES, S5_ST), lambda g, c: (g, c, 0, 0, 0))
    aspec = pl.BlockSpec((2, 1, SUBLANES, S5_ST), lambda g, c: (0, c, 0, 0))
    st_shape = jax.ShapeDtypeStruct((r // (ns * seq), n_cb, 2, SUBLANES, S5_ST), F32)
    return pl.pallas_call(
        functools.partial(_s5_kernel, seq=seq, tc=tc, ns=ns, nc=nc),
        grid=(r // (ns * seq), n_cb),
        in_specs=[pl.BlockSpec((ns * seq, wide), lambda g, c: (g, 2 * n_cb + c)),
                  pl.BlockSpec((2, 1, wide, 2 * S5_ST), lambda g, c: (0, c, 0, 0)),
                  pl.BlockSpec((2, 1, 2 * S5_ST, wide), lambda g, c: (0, c, 0, 0)),
                  aspec, aspec,
                  pl.BlockSpec((1, wide), lambda g, c: (0, c)),
                  state, state],
        out_specs=[pl.BlockSpec((ns * seq, wide), lambda g, c: (g, c)), state, state],
        out_shape=[jax.ShapeDtypeStruct((r, D_S5), F32), st_shape, st_shape],
        scratch_shapes=[pltpu.VMEM((seq * SUBLANES, CH), F32), pltpu.VMEM((seq * SUBLANES, CH), F32),
                        pltpu.VMEM((tc * SUBLANES, 2 * S5_ST), F32), pltpu.VMEM((tc * SUBLANES, 2 * S5_ST), F32)],
        compiler_params=_cparams(("parallel", "parallel"), 48),
        name="s5",
    )(rec, bb, cc, a_re, a_im, d_skip, s0_re, s0_im)


def _s5_params(lam_re, lam_im, log_dt, b_re, b_im, c_re, c_im, ns, nc):
    dt = jnp.exp(log_dt)[..., None]
    mag = jnp.exp(lam_re * dt)
    abar_re, abar_im = mag * jnp.cos(lam_im * dt), mag * jnp.sin(lam_im * dt)
    den = lam_re * lam_re + lam_im * lam_im
    nr, ni = abar_re - 1, abar_im
    cr = (nr * lam_re + ni * lam_im) / den
    ci = (ni * lam_re - nr * lam_im) / den
    bb_re = cr[..., None] * b_re - ci[..., None] * b_im
    bb_im = cr[..., None] * b_im + ci[..., None] * b_re
    gpc = CH // S5_GROUP
    eye = jnp.eye(gpc, dtype=F32)

    def in_map(b):
        b = b.reshape(2, N_CH, gpc, S5_STATE, S5_GROUP).transpose(0, 1, 2, 4, 3)
        return (b[:, :, :, :, None, :] * eye[None, None, :, None, :, None]).reshape(2, N_CH, CH, S5_ST)

    def out_map(c):
        c = c.reshape(2, N_CH, gpc, S5_GROUP, S5_STATE).transpose(0, 1, 2, 4, 3)
        return (c[:, :, :, :, None, :] * eye[None, None, :, None, :, None]).reshape(2, N_CH, S5_ST, CH)

    bb = jnp.concatenate([in_map(bb_re), in_map(bb_im)], axis=-1).astype(BF16)
    cc = jnp.concatenate([out_map(c_re), -out_map(c_im)], axis=-2).astype(BF16)
    n_cb = N_CH // nc
    bb = bb.reshape(2, n_cb, nc * CH, 2 * S5_ST)
    cc = cc.reshape(2, n_cb, nc, 2 * S5_ST, CH).transpose(0, 1, 3, 2, 4).reshape(2, n_cb, 2 * S5_ST, nc * CH)

    def per_slot(a):
        a = jnp.broadcast_to(a.reshape(2, n_cb, nc, 1, S5_ST), (2, n_cb, nc, ns, S5_ST))
        return a.reshape(2, n_cb, SUBLANES, S5_ST)

    return bb, cc, per_slot(abar_re), per_slot(abar_im)


def _out_proj_kernel(x_ref, oa_ref, ob_ref, y_ref, wg_ref, bg_ref, wo_ref, mod_ref, g_ref, *refs, route):
    if route:
        r_ref, _, x1_ref, h2_ref, route_ref = refs
    else:
        x1_ref, h2_ref = refs
    z = jax.nn.gelu(y_ref[...])
    gl = jnp.dot(z.astype(BF16), wg_ref[...], preferred_element_type=F32) + bg_ref[...]
    oc = z * jax.nn.sigmoid(gl)
    mix = (jnp.dot(oa_ref[...], wo_ref[:D_ATTN, :], preferred_element_type=F32)
           + jnp.dot(ob_ref[...].astype(BF16), wo_ref[D_ATTN:D_ATTN + D_LRU, :], preferred_element_type=F32)
           + jnp.dot(oc.astype(BF16), wo_ref[D_ATTN + D_LRU:, :], preferred_element_type=F32))
    x1 = x_ref[...] + mod_ref[0, 2:3, :] * mix
    x1_ref[...] = x1
    h2 = _rms_mod(x1, g_ref[...], mod_ref[0, 3:4, :], mod_ref[0, 4:5, :])
    h2_ref[...] = h2.astype(h2_ref.dtype)
    if route:
        hi = h2.astype(BF16)
        lo = (h2 - hi.astype(F32)).astype(BF16)
        pr = (jnp.dot(hi, r_ref[...], preferred_element_type=F32)
              + jnp.dot(lo, r_ref[...], preferred_element_type=F32))
        logits = pr + pltpu.roll(pr, LANES - N_EXPERTS, axis=1)
        lane = lax.broadcasted_iota(jnp.int32, logits.shape, 1)
        lg = jnp.where(lane < N_EXPERTS, logits, -jnp.inf)
        m1 = jnp.max(lg, axis=-1, keepdims=True)
        i1 = jnp.min(jnp.where(lg == m1, lane, LANES), axis=-1, keepdims=True)
        lg2 = jnp.where(lane == i1, -jnp.inf, lg)
        m2 = jnp.max(lg2, axis=-1, keepdims=True)
        i2 = jnp.min(jnp.where(lg2 == m2, lane, LANES), axis=-1, keepdims=True)
        e2 = jnp.exp(m2 - m1)
        den = 1.0 + e2
        route_ref[...] = (jnp.where(lane == 0, i1.astype(F32), 0.0) + jnp.where(lane == 1, i2.astype(F32), 0.0)
                          + jnp.where(lane == 2, 1.0 / den, 0.0) + jnp.where(lane == 3, e2 / den, 0.0))


def _out_proj(x2d, oa, ob, y, w_glu_bf, b_glu, w_out_bf, mod, mod_base, rows_per_mod, g, routing):
    r, d = x2d.shape
    tm = 512
    tiles_per_mod = rows_per_mod // tm
    route = routing is not None
    row = lambda n: pl.BlockSpec((tm, n), lambda i: (i, 0))
    full = lambda a: pl.BlockSpec(a.shape, lambda i: (0,) * a.ndim, pipeline_mode=pl.Buffered(1))
    args = [x2d, oa, ob, y, w_glu_bf, b_glu, w_out_bf, mod, g]
    in_specs = [row(d), row(D_ATTN), row(D_LRU), row(D_S5), full(w_glu_bf), full(b_glu), full(w_out_bf),
                pl.BlockSpec((1, N_MOD, d), lambda i: (mod_base + i // tiles_per_mod, 0, 0)), full(g)]
    aliases = {}
    if route:
        router_pad, shared_rows, row_base = routing
        args += [router_pad, shared_rows]
        in_specs += [full(router_pad), pl.BlockSpec(memory_space=pl.ANY)]
        aliases = {len(args) - 1: 1}
        out_shape = [jax.ShapeDtypeStruct((r, d), F32), jax.ShapeDtypeStruct(shared_rows.shape, F32),
                     jax.ShapeDtypeStruct((r, LANES), F32)]
        out_specs = [row(d), pl.BlockSpec((tm, d), lambda i: (i + row_base // tm, 0)), row(LANES)]
    else:
        out_shape = [jax.ShapeDtypeStruct((r, d), F32), jax.ShapeDtypeStruct((r, d), BF16)]
        out_specs = [row(d), row(d)]
    return pl.pallas_call(
        functools.partial(_out_proj_kernel, route=route),
        grid=(r // tm,),
        in_specs=in_specs,
        out_specs=out_specs,
        out_shape=out_shape,
        input_output_aliases=aliases,
        compiler_params=_cparams(("parallel",), 56),
        name="out_proj",
    )(*args)


def _swiglu_part(h, w1, w3, w2):
    a = jnp.dot(h, w1, preferred_element_type=F32)
    b = jnp.dot(h, w3, preferred_element_type=F32)
    return jnp.dot((jax.nn.silu(a) * b).astype(BF16), w2, preferred_element_type=F32)


def _ffn_kernel(h_ref, x_ref, mod_ref, w1_ref, w3_ref, w2_ref, o_ref):
    f = pl.program_id(1)
    part = _swiglu_part(h_ref[...], w1_ref[0], w3_ref[0], w2_ref[...])

    @pl.when(f == 0)
    def _():
        o_ref[...] = part

    @pl.when(f > 0)
    def _():
        o_ref[...] += part

    @pl.when(f == pl.num_programs(1) - 1)
    def _():
        o_ref[...] = x_ref[...] + mod_ref[0, 5:6, :] * o_ref[...]


FFN_TF = 512


def _hidden_tiles(w, tf):
    *lead, d, f = w.shape
    n = len(lead)
    return w.reshape(*lead, d, f // tf, tf).transpose(*range(n), n + 1, n, n + 2)


def _ffn(h2, x1, mod, mod_base, rows_per_mod, w1, w3, w2):
    r, d = x1.shape
    tm, tf = 1024, FFN_TF
    tiles_per_mod = rows_per_mod // tm
    return pl.pallas_call(
        _ffn_kernel,
        grid=(r // tm, w1.shape[0]),
        in_specs=[pl.BlockSpec((tm, d), lambda i, f: (i, 0)),
                  pl.BlockSpec((tm, d), lambda i, f: (i, 0), pipeline_mode=pl.Buffered(1)),
                  pl.BlockSpec((1, N_MOD, d), lambda i, f: (mod_base + i // tiles_per_mod, 0, 0)),
                  pl.BlockSpec((1, d, tf), lambda i, f: (f, 0, 0)),
                  pl.BlockSpec((1, d, tf), lambda i, f: (f, 0, 0)),
                  pl.BlockSpec((tf, d), lambda i, f: (f, 0))],
        out_specs=pl.BlockSpec((tm, d), lambda i, f: (i, 0)),
        out_shape=jax.ShapeDtypeStruct((r, d), F32),
        compiler_params=_cparams(("parallel", "arbitrary"), 60),
        name="ffn",
    )(h2, x1, mod, w1, w3, w2)


MOE_TM = 512
MOE_TF = 1408
MOE_TT = 256


def _route_plan(route):
    t = route.shape[0]
    n_pairs = 2 * t
    n_tiles = n_pairs // MOE_TM + N_EXPERTS
    experts = route[:, :2].astype(jnp.int32).reshape(n_pairs)
    onehot = (experts[:, None] == jnp.arange(N_EXPERTS, dtype=jnp.int32)[None]).astype(jnp.int32)
    csum = jnp.cumsum(onehot, axis=0)
    rank = jnp.sum(onehot * csum, axis=1) - 1
    counts = csum[-1]
    padded = (counts + MOE_TM - 1) // MOE_TM * MOE_TM
    ends = jnp.cumsum(padded)
    pos = jnp.sum(onehot * (ends - padded)[None], axis=1) + rank
    src = jnp.zeros((n_tiles * MOE_TM,), jnp.int32).at[pos].set(jnp.arange(n_pairs, dtype=jnp.int32) // 2)
    tile_start = jnp.arange(n_tiles, dtype=jnp.int32) * MOE_TM
    tile_valid = (tile_start < ends[-1]).astype(jnp.int32)
    tile_expert = jnp.sum((tile_start[:, None] >= ends[None]).astype(jnp.int32), axis=1)
    last_expert = jnp.max(jnp.where(counts > 0, jnp.arange(N_EXPERTS, dtype=jnp.int32), 0))
    tile_expert = jnp.where(tile_valid == 1, tile_expert, last_expert)
    return pos, src, tile_expert, tile_valid


def _gather_rows(idx_of_row, n_rows, src_hbm, dst, sem):
    def row(r, carry):
        pltpu.make_async_copy(src_hbm.at[pl.ds(idx_of_row(r), 1)], dst.at[pl.ds(r, 1)], sem).start()
        return carry
    lax.fori_loop(0, n_rows, row, 0, unroll=8)


def _wait_rows(n_rows, src_hbm, dst, sem):
    pltpu.make_async_copy(src_hbm.at[pl.ds(0, n_rows)], dst, sem).wait()


def _moe_experts_kernel(src_ref, texp_ref, valid_ref, h_hbm, w1_ref, w3_ref, w2_ref, o_ref, xbuf, xbf, sem):
    i = pl.program_id(0)
    f = pl.program_id(1)

    def gather(tile):
        _gather_rows(lambda r: src_ref[tile * MOE_TM + r], MOE_TM, h_hbm, xbuf, sem.at[0])

    @pl.when(f == 0)
    def _():
        @pl.when(i == 0)
        def _():
            gather(0)

        _wait_rows(MOE_TM, h_hbm, xbuf, sem.at[0])
        xbf[...] = xbuf[...].astype(BF16)

        @pl.when(i + 1 < pl.num_programs(0))
        def _():
            gather(i + 1)

    valid = valid_ref[i] == 1

    @pl.when(valid)
    def _():
        part = _swiglu_part(xbf[...], w1_ref[0, 0], w3_ref[0, 0], w2_ref[0])

        @pl.when(f == 0)
        def _():
            o_ref[...] = part

        @pl.when(f > 0)
        def _():
            o_ref[...] += part

    @pl.when(jnp.logical_not(valid) & (f == 0))
    def _():
        o_ref[...] = jnp.zeros_like(o_ref)


def _moe_experts(h2, src, tile_expert, tile_valid, w1, w3, w2):
    t, d = h2.shape
    n_tiles = tile_expert.shape[0]
    n_f = w1.shape[1]

    def wmap(i, f, src, texp, valid):
        return (texp[i], jnp.where(valid[i] == 1, f, n_f - 1), 0, 0)

    def w2map(i, f, src, texp, valid):
        return (texp[i], jnp.where(valid[i] == 1, f, n_f - 1), 0)

    grid_spec = pltpu.PrefetchScalarGridSpec(
        num_scalar_prefetch=3,
        grid=(n_tiles, n_f),
        in_specs=[pl.BlockSpec(memory_space=pl.ANY),
                  pl.BlockSpec((1, 1, d, MOE_TF), wmap),
                  pl.BlockSpec((1, 1, d, MOE_TF), wmap),
                  pl.BlockSpec((1, MOE_TF, d), w2map)],
        out_specs=pl.BlockSpec((MOE_TM, d), lambda i, f, *_: (i, 0)),
        scratch_shapes=[pltpu.VMEM((MOE_TM, d), F32), pltpu.VMEM((MOE_TM, d), BF16),
                        pltpu.SemaphoreType.DMA((1,))])
    return pl.pallas_call(
        _moe_experts_kernel,
        grid_spec=grid_spec,
        out_shape=jax.ShapeDtypeStruct((n_tiles * MOE_TM, d), F32),
        compiler_params=_cparams(("arbitrary", "arbitrary"), 62),
        name="moe_experts",
    )(src, tile_expert, tile_valid, h2, w1, w3, w2)


def _moe_combine_kernel(pos_ref, x_ref, mod_ref, route_ref, y_hbm, o_ref, buf, sem, *, tok_base):
    i = pl.program_id(0)
    slot = i % 2

    def gather(tile, slot):
        for k in range(2):
            _gather_rows(lambda r, k=k: pos_ref[2 * (tok_base + tile * MOE_TT + r) + k], MOE_TT, y_hbm,
                         buf.at[slot, k], sem.at[slot])

    @pl.when(i == 0)
    def _():
        gather(0, 0)

    for k in range(2):
        _wait_rows(MOE_TT, y_hbm, buf.at[slot, k], sem.at[slot])

    @pl.when(i + 1 < pl.num_programs(0))
    def _():
        gather(i + 1, 1 - slot)

    mixed = route_ref[:, 2:3] * buf[slot, 0] + route_ref[:, 3:4] * buf[slot, 1]
    o_ref[...] = x_ref[...] + mod_ref[0, 5:6, :] * mixed


def _moe_combine(pos, x1, mod, mod_base, rows_per_mod, route, y_rows, tok_base):
    r, d = x1.shape
    tiles_per_mod = rows_per_mod // MOE_TT
    grid_spec = pltpu.PrefetchScalarGridSpec(
        num_scalar_prefetch=1,
        grid=(r // MOE_TT,),
        in_specs=[pl.BlockSpec((MOE_TT, d), lambda i, *_: (i, 0)),
                  pl.BlockSpec((1, N_MOD, d), lambda i, *_: (mod_base + i // tiles_per_mod, 0, 0)),
                  pl.BlockSpec((MOE_TT, LANES), lambda i, *_: (i, 0)),
                  pl.BlockSpec(memory_space=pl.ANY)],
        out_specs=pl.BlockSpec((MOE_TT, d), lambda i, *_: (i, 0)),
        scratch_shapes=[pltpu.VMEM((2, 2, MOE_TT, d), F32), pltpu.SemaphoreType.DMA((2,))])
    return pl.pallas_call(
        functools.partial(_moe_combine_kernel, tok_base=tok_base),
        grid_spec=grid_spec,
        out_shape=jax.ShapeDtypeStruct((r, d), F32),
        compiler_params=_cparams(("arbitrary",), 32),
        name="moe_combine",
    )(pos, x1, mod, route, y_rows)


def _slot_states(s, ns, nc, width):
    ng = s.shape[0] // ns
    s = s.reshape(ng, ns, 2, N_CH // nc, nc, width).transpose(0, 3, 2, 4, 1, 5)
    return s.reshape(ng, N_CH // nc, 2, SUBLANES, width)


def _unslot_states(s, ns, nc, width):
    ng = s.shape[0]
    s = s.reshape(ng, N_CH // nc, 2, nc, ns, width).transpose(0, 4, 2, 1, 3, 5)
    return s.reshape(ng * ns, 2, N_CH * width)


def _token_mixer(x2d, batch, seq, mod, mod_base, rows_per_mod, p, kv_ctx, lru_h0, s5_h0, prev_kv=None,
                 routing=None):
    is_ctx = kv_ctx is None
    ns = min(batch, SUBLANES)
    nc = SUBLANES // ns
    outs = _in_proj(x2d, p['g_mix'], mod, mod_base, rows_per_mod, p['w_in'], p['q_g'], p['k_g'], p['ones_bd'],
                    seq, is_ctx, prev_kv)
    q, k, v, rec = outs[:4]
    if is_ctx:
        oa = _ctx_attn(q, k, v, seq)
        lru_h0 = jnp.zeros((batch, 2, D_LRU), F32)
        s5_h0 = (jnp.zeros((batch, 2, N_S5_GROUPS * S5_STATE), F32),) * 2
    else:
        oa = _nbr_attn(q, k, v, kv_ctx[0], kv_ctx[1], kv_ctx[2], p['tb'], seq)
    lru_w, lru_b = _rglru_params(p['lru_wa'], p['lru_ba'], p['lru_wi'], p['lru_bi'], nc)
    ob, lru_fin = _rglru(rec, p['conv_w'], p['conv_b'], lru_w, lru_b, p['lru_lam'],
                         _slot_states(lru_h0, ns, nc, CH), seq, ns, nc)
    s5_bb, s5_cc, s5_are, s5_aim = _s5_params(*p['s5'], ns, nc)
    y, fin_re, fin_im = _s5(rec, s5_bb, s5_cc, s5_are, s5_aim, p['d_skip'],
                            _slot_states(s5_h0[0], ns, nc, S5_ST), _slot_states(s5_h0[1], ns, nc, S5_ST),
                            seq, ns, nc)
    res = _out_proj(x2d, oa, ob, y, p['w_glu'], p['b_glu'], p['w_out'], mod, mod_base, rows_per_mod,
                    p['g_ffn'], routing)
    if not is_ctx:
        return res, None
    state = (outs[4], outs[5], _unslot_states(lru_fin, ns, nc, CH),
             _unslot_states(fin_re, ns, nc, S5_ST).reshape(batch, 2, N_S5_GROUPS, S5_STATE),
             _unslot_states(fin_im, ns, nc, S5_ST).reshape(batch, 2, N_S5_GROUPS, S5_STATE))
    return res, state


def kernel(x_prompt, x_sample, c, cache_k, cache_v, state_lru, state_s5_re, state_s5_im, c_ctx, norm_mix_g, norm_ffn_g, w_mod, b_mod, w_in, w_out, q_norm_g, k_norm_g, rpb, lru_conv_w, lru_conv_b, lru_wa, lru_ba, lru_wi, lru_bi, lru_lam, s5_lam_re, s5_lam_im, s5_log_dt, s5_b_re, s5_b_im, s5_c_re, s5_c_im, s5_d, s5_w_glu, s5_b_glu, ffn_w1, ffn_w3, ffn_w2, moe_router, moe_w1, moe_w3, moe_w2):
    batch, seq, d = x_prompt.shape
    dec_batch, dec_seq, _ = x_sample.shape
    depth = w_in.shape[0]
    assert dec_batch + 1 <= SUBLANES and batch % SUBLANES == 0 and dec_batch in (4, 8)

    cvecs = jnp.concatenate([c_ctx[None], c, jnp.zeros((SUBLANES - 1 - dec_batch, d), F32)], axis=0)
    mods = _adaln(cvecs, w_mod, b_mod).reshape(depth, SUBLANES, N_MOD, d)

    heads_per_tile = 512 // HEAD_DIM
    ones_bd = jnp.asarray(np.kron(np.eye(heads_per_tile // 2), np.ones((HEAD_DIM, HEAD_DIM))), BF16)

    xp = x_prompt.reshape(batch * seq, d)
    xs = x_sample.reshape(dec_batch * dec_seq, d)
    n_tok = (batch * seq, dec_batch * dec_seq)
    kv, lrus, s5rs, s5is = None, [], [], []
    for l in range(depth):
        p = {
            'g_mix': norm_mix_g[l][None], 'g_ffn': norm_ffn_g[l][None],
            'w_in': w_in[l].astype(BF16), 'w_out': w_out[l].astype(BF16),
            'q_g': jnp.tile(q_norm_g[l], heads_per_tile)[None], 'k_g': jnp.tile(k_norm_g[l], heads_per_tile)[None],
            'ones_bd': ones_bd, 'tb': _bias_table(rpb[l]),
            'conv_w': lru_conv_w[l], 'conv_b': lru_conv_b[l][None],
            'lru_wa': lru_wa[l], 'lru_ba': lru_ba[l], 'lru_wi': lru_wi[l], 'lru_bi': lru_bi[l], 'lru_lam': lru_lam[l],
            's5': (s5_lam_re[l], s5_lam_im[l], s5_log_dt[l], s5_b_re[l], s5_b_im[l], s5_c_re[l], s5_c_im[l]),
            'd_skip': s5_d[l][None], 'w_glu': s5_w_glu[l].astype(BF16), 'b_glu': s5_b_glu[l][None],
        }
        j = l // 2
        dense = l % 2 == 0
        route_p = route_s = None
        if not dense:
            r_hi = moe_router[j].astype(BF16)
            r_lo = (moe_router[j] - r_hi.astype(F32)).astype(BF16)
            router = jnp.pad(jnp.concatenate([r_hi, r_lo], axis=1), ((0, 0), (0, LANES - 2 * N_EXPERTS)))
            route_p = (router, jnp.zeros((sum(n_tok), d), F32), 0)
        res_p, (k_c, v_c, lru_c, s5r_c, s5i_c) = _token_mixer(
            xp, batch, seq, mods[l], 0, batch * seq, p, None, None, None, prev_kv=kv, routing=route_p)
        kv = (k_c, v_c)
        lrus.append(lru_c)
        s5rs.append(s5r_c)
        s5is.append(s5i_c)
        if not dense:
            route_s = (router, res_p[1], n_tok[0])
        res_s, _ = _token_mixer(
            xs, dec_batch, dec_seq, mods[l], 1, dec_seq, p, (cache_k, cache_v, l), state_lru[:, l],
            (state_s5_re[:, l].reshape(dec_batch, 2, -1), state_s5_im[:, l].reshape(dec_batch, 2, -1)),
            routing=route_s)
        if dense:
            w = (_hidden_tiles(ffn_w1[j].astype(BF16), FFN_TF), _hidden_tiles(ffn_w3[j].astype(BF16), FFN_TF),
                 ffn_w2[j].astype(BF16))
            xp = _ffn(res_p[1], res_p[0], mods[l], 0, batch * seq, *w)
            xs = _ffn(res_s[1], res_s[0], mods[l], 1, dec_seq, *w)
        else:
            h2 = res_s[1]
            pos, src, tile_expert, tile_valid = _route_plan(jnp.concatenate([res_p[2], res_s[2]], axis=0))
            y_rows = _moe_experts(h2, src, tile_expert, tile_valid,
                                  _hidden_tiles(moe_w1[j].astype(BF16), MOE_TF),
                                  _hidden_tiles(moe_w3[j].astype(BF16), MOE_TF), moe_w2[j].astype(BF16))
            xp = _moe_combine(pos, res_p[0], mods[l], 0, batch * seq, res_p[2], y_rows, 0)
            xs = _moe_combine(pos, res_s[0], mods[l], 1, dec_seq, res_s[2], y_rows, batch * seq)
    return (xp.reshape(batch, seq, d), xs.reshape(dec_batch, dec_seq, d), kv[0], kv[1],
            jnp.stack(lrus, axis=1), jnp.stack(s5rs, axis=1), jnp.stack(s5is, axis=1))
```

```python
import functools

import numpy as np
import jax
import jax.numpy as jnp
from jax import lax
from jax.experimental import pallas as pl
from jax.experimental.pallas import tpu as pltpu

F32 = jnp.float32
BF16 = jnp.bfloat16

D_MODEL = 2048
N_HEADS = 16
HEAD_DIM = 64
D_ATTN = N_HEADS * HEAD_DIM
GRID_W = 64
WIN_R = 8
WIN_C = 16
D_LRU = 512
LRU_BLOCK = 64
CONV_W = 4
LRU_C = 8.0
D_S5 = 512
S5_GROUP = 16
N_S5_GROUPS = 32
S5_STATE = 64
D_REC = 2 * D_LRU + D_S5
D_IN = 3 * D_ATTN + D_REC
N_MOD = 6
N_EXPERTS = 8
EPS = 1e-6
NEG = -1e30

LANES = 128
SUBLANES = 8
MIB = 1024 * 1024

CH = 128
N_CH = D_LRU // CH
S5_ST = (CH // S5_GROUP) * S5_STATE


def _cparams(sem, vmem_mib):
    return pltpu.CompilerParams(dimension_semantics=sem, vmem_limit_bytes=vmem_mib * MIB)


def _adaln_kernel(c_ref, w_ref, b_ref, o_ref):
    s = jax.nn.silu(c_ref[...]).astype(BF16)
    o_ref[0] = jnp.dot(s, w_ref[0].astype(BF16), preferred_element_type=F32) + b_ref[0]


def _adaln(cvecs, w_mod, b_mod):
    depth, d, n = w_mod.shape
    tn = 1024
    return pl.pallas_call(
        _adaln_kernel,
        grid=(depth, n // tn),
        in_specs=[pl.BlockSpec((SUBLANES, d), lambda l, j: (0, 0)),
                  pl.BlockSpec((1, d, tn), lambda l, j: (l, 0, j)),
                  pl.BlockSpec((1, 1, tn), lambda l, j: (l, 0, j))],
        out_specs=pl.BlockSpec((1, SUBLANES, tn), lambda l, j: (l, 0, j)),
        out_shape=jax.ShapeDtypeStruct((depth, SUBLANES, n), F32),
        compiler_params=_cparams(("parallel", "parallel"), 40),
        name="adaln",
    )(cvecs, w_mod, b_mod.reshape(depth, 1, n))


def _rms_mod(x, g, shift, scale):
    xf = x * lax.rsqrt(jnp.mean(x * x, axis=-1, keepdims=True) + EPS)
    return (xf * g) * (1 + scale) + shift


def _in_proj_kernel(x_ref, g_ref, mod_ref, w_ref, qg_ref, kg_ref, ones_ref, *refs, seq, write_cache, n_prev):
    kp_ref = vp_ref = None
    if write_cache and n_prev:
        kp_ref, vp_ref, *refs = refs
    if write_cache:
        q_ref, k_ref, v_ref, rec_ref, kc_ref, vc_ref, h_scr = refs
    else:
        q_ref, k_ref, v_ref, rec_ref, h_scr = refs
    j = pl.program_id(1)
    tm, tn = q_ref.shape
    heads = tn // HEAD_DIM

    @pl.when(j == 0)
    def _():
        h_scr[...] = _rms_mod(x_ref[...], g_ref[...], mod_ref[0, 0:1, :], mod_ref[0, 1:2, :]).astype(BF16)

    y = jnp.dot(h_scr[...], w_ref[...], preferred_element_type=F32)

    def to_cache(c_ref, prev_ref, val):
        if prev_ref is not None:
            c_ref[:, :n_prev] = prev_ref[...]
        for b in range(tm // seq):
            for h in range(heads):
                c_ref[b, n_prev, h, :, :] = val[b * seq:(b + 1) * seq, h * HEAD_DIM:(h + 1) * HEAD_DIM]

    @pl.when(j < 4)
    def _():
        y2 = y * y
        hi = y2.astype(BF16)
        lo = (y2 - hi.astype(F32)).astype(BF16)
        half = ones_ref.shape[0]
        ss = jnp.concatenate(
            [jnp.dot(hi[:, c:c + half], ones_ref[...], preferred_element_type=F32)
             + jnp.dot(lo[:, c:c + half], ones_ref[...], preferred_element_type=F32) for c in range(0, tn, half)],
            axis=1)
        gain = jnp.where(j < 2, qg_ref[...], kg_ref[...])
        yn = (y * lax.rsqrt(ss * (1.0 / HEAD_DIM) + EPS)) * gain

        @pl.when(j < 2)
        def _():
            q_ref[...] = yn.astype(BF16)

        @pl.when(j >= 2)
        def _():
            k_ref[...] = yn.astype(BF16)
            if write_cache:
                to_cache(kc_ref, kp_ref, yn)

    @pl.when((j >= 4) & (j < 6))
    def _():
        v_ref[...] = y.astype(BF16)
        if write_cache:
            to_cache(vc_ref, vp_ref, y)

    @pl.when(j >= 6)
    def _():
        rec_ref[...] = y


def _in_proj(x2d, g, mod, mod_base, rows_per_mod, w_bf, qg, kg, ones_bd, seq, write_cache, prev_kv):
    r, d = x2d.shape
    tm, tn = 512, 512
    n_i = r // tm
    tiles_per_mod = rows_per_mod // tm
    bpt = tm // seq
    hpt = tn // HEAD_DIM
    n_prev = prev_kv[0].shape[1] if prev_kv else 0

    def col(lo, n):
        return lambda i, j: (i, jnp.clip(j - lo, 0, n - 1))

    def cache_spec(lo, n_layers):
        return pl.BlockSpec((bpt, n_layers, hpt, seq, HEAD_DIM), lambda i, j: (i, 0, jnp.clip(j - lo, 0, 1), 0, 0))

    args = [x2d, g, mod, w_bf, qg, kg, ones_bd]
    in_specs = [pl.BlockSpec((tm, d), lambda i, j: (i, 0)),
                pl.BlockSpec((1, d), lambda i, j: (0, 0)),
                pl.BlockSpec((1, N_MOD, d), lambda i, j: (mod_base + i // tiles_per_mod, 0, 0)),
                pl.BlockSpec((d, tn), lambda i, j: (0, j)),
                pl.BlockSpec((1, tn), lambda i, j: (0, 0)),
                pl.BlockSpec((1, tn), lambda i, j: (0, 0)),
                pl.BlockSpec(ones_bd.shape, lambda i, j: (0, 0))]
    out_shape = [jax.ShapeDtypeStruct((r, D_ATTN), BF16)] * 3 + [jax.ShapeDtypeStruct((r, D_REC), F32)]
    out_specs = [pl.BlockSpec((tm, tn), col(0, 2)), pl.BlockSpec((tm, tn), col(2, 2)),
                 pl.BlockSpec((tm, tn), col(4, 2)), pl.BlockSpec((tm, tn), col(6, 3))]
    if write_cache:
        if n_prev:
            args += list(prev_kv)
            in_specs += [cache_spec(2, n_prev), cache_spec(4, n_prev)]
        cshape = jax.ShapeDtypeStruct((r // seq, n_prev + 1, N_HEADS, seq, HEAD_DIM), F32)
        out_shape += [cshape, cshape]
        out_specs += [cache_spec(2, n_prev + 1), cache_spec(4, n_prev + 1)]
    return pl.pallas_call(
        functools.partial(_in_proj_kernel, seq=seq, write_cache=write_cache, n_prev=n_prev),
        grid=(n_i, D_IN // tn),
        in_specs=in_specs,
        out_specs=out_specs,
        out_shape=out_shape,
        scratch_shapes=[pltpu.VMEM((tm, d), BF16)],
        compiler_params=_cparams(("parallel", "arbitrary"), 56),
        name="in_proj",
    )(*args)


def _qk(q, k):
    return lax.dot_general(q, k, (((1,), (1,)), ((), ())), preferred_element_type=F32)


def _head_lanes(n_rows, first):
    lane = lax.broadcasted_iota(jnp.int32, (n_rows, LANES), 1)
    return (lane < HEAD_DIM) if first else (lane >= HEAD_DIM)


def _own(x, first):
    return jnp.where(_head_lanes(x.shape[0], first), x, jnp.zeros_like(x))


def _values_and_ones(v, first):
    ones = jnp.where(_head_lanes(v.shape[0], first), 1.0, 0.0).astype(BF16)
    return jnp.concatenate([_own(v, first), ones], axis=1)


def _exp_rows(parts):
    m = functools.reduce(jnp.maximum, [jnp.max(s, axis=-1, keepdims=True) for s in parts])
    return [jnp.exp(s - m).astype(BF16) for s in parts]


def _ctx_attn_kernel(q_ref, k_ref, v_ref, o_ref):
    scale = HEAD_DIM ** -0.5
    for hp in range(N_HEADS * HEAD_DIM // LANES):
        sl = slice(hp * LANES, (hp + 1) * LANES)
        q, k, v = q_ref[:, sl], k_ref[:, sl], v_ref[:, sl]
        nd = None
        for first in (True, False):
            (e,) = _exp_rows([_qk(_own(q, first), k) * scale])
            part = jnp.dot(e, _values_and_ones(v, first), preferred_element_type=F32)
            nd = part if nd is None else nd + part
        o_ref[:, sl] = (nd[:, :LANES] / nd[:, LANES:]).astype(BF16)


def _ctx_attn(q, k, v, seq):
    r = q.shape[0]
    spec = pl.BlockSpec((seq, D_ATTN), lambda b: (b, 0))
    return pl.pallas_call(
        _ctx_attn_kernel,
        grid=(r // seq,),
        in_specs=[spec, spec, spec],
        out_specs=spec,
        out_shape=jax.ShapeDtypeStruct((r, D_ATTN), BF16),
        compiler_params=_cparams(("parallel",), 32),
        name="ctx_attn",
    )(q, k, v)


def _nbr_plan(seq):
    rows = seq // GRID_W
    wr = min(WIN_R, rows)
    row_start = np.clip(np.arange(rows) - wr // 2, 0, rows - wr)
    rows_per_blk = 256 // GRID_W
    ranges = []
    for qb in range(rows // rows_per_blk):
        rs = row_start[qb * rows_per_blk:(qb + 1) * rows_per_blk]
        lo = int(rs.min()) * GRID_W // LANES * LANES
        hi = -(-(int(rs.max()) + wr) * GRID_W // LANES) * LANES
        ranges.append((lo, hi))
    return rows, wr, row_start, ranges


def _nbr_attn_kernel(q_ref, k_ref, v_ref, kc_ref, vc_ref, tb_ref, o_ref, bias_scr, *, seq):
    rows, wr, row_start, ranges = _nbr_plan(seq)
    scale = HEAD_DIM ** -0.5
    heads = q_ref.shape[1] // HEAD_DIM
    neg = jnp.full((GRID_W, GRID_W), NEG, F32)

    @pl.when(pl.program_id(1) == 0)
    def _():
        for h in range(heads):
            for qr in range(rows):
                for kp in range(rows // 2):
                    blks = []
                    for kr in (2 * kp, 2 * kp + 1):
                        inside = row_start[qr] <= kr < row_start[qr] + wr
                        blks.append(tb_ref[h, kr - qr + WIN_R - 1] if inside else neg)
                    bias_scr[h, qr * GRID_W:(qr + 1) * GRID_W, kp * LANES:(kp + 1) * LANES] = (
                        jnp.concatenate(blks, axis=1))

    kc = jnp.concatenate([kc_ref[0, 0, h] for h in range(heads)], axis=1).astype(BF16)
    vc = jnp.concatenate([vc_ref[0, 0, h] for h in range(heads)], axis=1).astype(BF16)
    for qb, (lo, hi) in enumerate(ranges):
        qs = slice(qb * 256, (qb + 1) * 256)
        q = q_ref[qs, :]
        nd = None
        for h, first in enumerate((True, False)):
            qh = _own(q, first)
            s_loc = _qk(qh, k_ref[lo:hi, :]) * scale + bias_scr[h, qs, lo:hi]
            s_ctx = _qk(qh, kc) * scale
            e_loc, e_ctx = _exp_rows([s_loc, s_ctx])
            part = (jnp.dot(e_loc, _values_and_ones(v_ref[lo:hi, :], first), preferred_element_type=F32)
                    + jnp.dot(e_ctx, _values_and_ones(vc, first), preferred_element_type=F32))
            nd = part if nd is None else nd + part
        o_ref[qs, :] = (nd[:, :LANES] / nd[:, LANES:]).astype(BF16)


def _nbr_attn(q, k, v, cache_k, cache_v, layer, tb, seq):
    r = q.shape[0]
    past = cache_k.shape[3]
    hp = LANES // HEAD_DIM
    spec = pl.BlockSpec((seq, LANES), lambda g, b: (b, g))
    cspec = pl.BlockSpec((1, 1, hp, past, HEAD_DIM), lambda g, b: (b, layer, g, 0, 0))
    return pl.pallas_call(
        functools.partial(_nbr_attn_kernel, seq=seq),
        grid=(N_HEADS // hp, r // seq),
        in_specs=[spec, spec, spec, cspec, cspec,
                  pl.BlockSpec((hp,) + tb.shape[1:], lambda g, b: (g, 0, 0, 0))],
        out_specs=spec,
        out_shape=jax.ShapeDtypeStruct((r, D_ATTN), BF16),
        scratch_shapes=[pltpu.VMEM((hp, seq, seq), F32)],
        compiler_params=_cparams(("parallel", "arbitrary"), 48),
        name="nbr_attn",
    )(q, k, v, cache_k, cache_v, tb)


def _bias_table(rpb_l):
    col = np.arange(GRID_W)
    col_start = np.clip(col - WIN_C // 2, 0, GRID_W - WIN_C)
    col_in = (col[None, :] >= col_start[:, None]) & (col[None, :] < col_start[:, None] + WIN_C)
    dc_idx = np.clip(col[None, :] - col[:, None] + WIN_C - 1, 0, 2 * WIN_C - 2)
    onehot = (dc_idx.reshape(1, -1) == np.arange(2 * WIN_C - 1)[:, None]).astype(np.float32)
    tb = jnp.einsum('hrc,cq->hrq', rpb_l, onehot, precision=lax.Precision.HIGHEST)
    tb = tb.reshape(rpb_l.shape[0], rpb_l.shape[1], GRID_W, GRID_W)
    return jnp.where(col_in[None, None], tb, NEG).astype(F32)


def _expm1(x):
    u = jnp.exp(x)
    um1 = u - 1.0
    near = um1 * x / jnp.where(u == 1.0, 1.0, jnp.log(u))
    return jnp.where(x < -0.5, um1, jnp.where(u == 1.0, x, near))


def _slot0(n_rows, n_cols, ns):
    return (lax.broadcasted_iota(jnp.int32, (n_rows, n_cols), 0) % SUBLANES) < ns


def _per_slot(vec, n_rows, ns, nc):
    if nc == 1:
        return vec
    w = vec.shape[1] // 2
    return jnp.where(_slot0(n_rows, w, ns), vec[:, :w], vec[:, w:])


def _stack_slots(x, ns, nc):
    if nc == 1:
        return x
    m = _slot0(x.shape[0], x.shape[1], ns)
    zero = jnp.zeros_like(x)
    return jnp.concatenate([jnp.where(m, x, zero), jnp.where(m, zero, x)], axis=1)


def _pick_slot(y, ns, nc):
    if nc == 1:
        return y
    w = y.shape[1] // 2
    return jnp.where(_slot0(y.shape[0], w, ns), y[:, :w], y[:, w:])


def _to_time_major(x_ref, t_scr, row0, seq, ns, nc):
    for k in range(nc):
        for s in range(ns):
            t_scr[pl.ds(row0 + k * ns + s, seq, stride=SUBLANES), :] = (
                x_ref[s * seq:(s + 1) * seq, k * CH:(k + 1) * CH])


def _rows(start, n):
    return pl.ds(pl.multiple_of(start, SUBLANES), n)


def _rglru_kernel(x_ref, g_ref, cw_ref, cb_ref, w_ref, b_ref, lam_ref, h0_ref, o_ref, fin_ref,
                  xt_scr, yt_scr, af_scr, bf_scr, ab_scr, bb_scr, *, seq, tc, ns, nc):
    rows = tc * SUBLANES
    n_chunks = seq // tc
    front = (CONV_W // 2) * SUBLANES
    back = (CONV_W - 1 - CONV_W // 2) * SUBLANES

    xt_scr[0:front, :] = jnp.zeros((front, CH), F32)
    xt_scr[front + seq * SUBLANES:front + seq * SUBLANES + back, :] = jnp.zeros((back, CH), F32)
    _to_time_major(x_ref, xt_scr, front, seq, ns, nc)
    yt_scr[...] = jnp.zeros_like(yt_scr)

    def coefficients(r0, d, a_scr, b_scr):
        xc = _per_slot(cb_ref[...], rows, ns, nc)
        for j in range(CONV_W):
            xc = xc + xt_scr[_rows(r0 + j * SUBLANES, rows), :] * _per_slot(cw_ref[j:j + 1, :], rows, ns, nc)
        gates = (jnp.dot(_stack_slots(xc, ns, nc).astype(BF16), w_ref[d, 0], preferred_element_type=F32)
                 + _per_slot(b_ref[d, 0], rows, ns, nc))
        rg = jax.nn.sigmoid(gates[:, :CH])
        ig = jax.nn.sigmoid(gates[:, CH:])
        log_a = -LRU_C * rg * _per_slot(jax.nn.softplus(-lam_ref[d:d + 1, :]), rows, ns, nc)
        a_scr[...] = jnp.exp(log_a)
        b_scr[...] = jnp.sqrt(-_expm1(2 * log_a)) * (ig * xc)

    def chunk(c, carry):
        rf = pl.multiple_of(c * rows, rows)
        rb = pl.multiple_of((n_chunks - 1 - c) * rows, rows)
        coefficients(rf, 0, af_scr, bf_scr)
        coefficients(rb, 1, ab_scr, bb_scr)

        def step(t, carry):
            h_f, h_b = carry
            tf = _rows(t * SUBLANES, SUBLANES)
            tb = _rows((tc - 1 - t) * SUBLANES, SUBLANES)
            h_f = af_scr[tf, :] * h_f + bf_scr[tf, :]
            h_b = ab_scr[tb, :] * h_b + bb_scr[tb, :]
            bf_scr[tf, :] = h_f
            bb_scr[tb, :] = h_b
            return h_f, h_b

        carry = lax.fori_loop(0, tc, step, carry, unroll=8)
        yt_scr[pl.ds(rf, rows), :] += bf_scr[...]
        yt_scr[pl.ds(rb, rows), :] += bb_scr[...]
        return carry

    h_f, h_b = lax.fori_loop(0, n_chunks, chunk, (h0_ref[0, 0, 0], h0_ref[0, 0, 1]))
    fin_ref[0, 0, 0] = h_f
    fin_ref[0, 0, 1] = h_b
    for k in range(nc):
        for s in range(ns):
            blk = (slice(s * seq, (s + 1) * seq), slice(k * CH, (k + 1) * CH))
            o_ref[blk] = yt_scr[pl.ds(k * ns + s, seq, stride=SUBLANES), :] * jax.nn.gelu(g_ref[blk])


def _rglru(rec, conv_w, conv_b, w_g, b_g, lam, h0, seq, ns, nc):
    r = rec.shape[0]
    n_cb = N_CH // nc
    tc = 128
    wide = nc * CH
    n_pad = (seq + CONV_W - 1) * SUBLANES
    state = pl.BlockSpec((1, 1, 2, SUBLANES, CH), lambda g, c: (g, c, 0, 0, 0))
    return pl.pallas_call(
        functools.partial(_rglru_kernel, seq=seq, tc=tc, ns=ns, nc=nc),
        grid=(r // (ns * seq), n_cb),
        in_specs=[pl.BlockSpec((ns * seq, wide), lambda g, c: (g, c)),
                  pl.BlockSpec((ns * seq, wide), lambda g, c: (g, n_cb + c)),
                  pl.BlockSpec((CONV_W, wide), lambda g, c: (0, c)),
                  pl.BlockSpec((1, wide), lambda g, c: (0, c)),
                  pl.BlockSpec((2, 1, wide, 2 * CH), lambda g, c: (0, c, 0, 0)),
                  pl.BlockSpec((2, 1, 1, 2 * wide), lambda g, c: (0, c, 0, 0)),
                  pl.BlockSpec((2, wide), lambda g, c: (0, c)),
                  state],
        out_specs=[pl.BlockSpec((ns * seq, wide), lambda g, c: (g, c)), state],
        out_shape=[jax.ShapeDtypeStruct((r, D_LRU), F32),
                   jax.ShapeDtypeStruct((r // (ns * seq), n_cb, 2, SUBLANES, CH), F32)],
        scratch_shapes=[pltpu.VMEM((n_pad, CH), F32), pltpu.VMEM((seq * SUBLANES, CH), F32)]
        + [pltpu.VMEM((tc * SUBLANES, CH), F32)] * 4,
        compiler_params=_cparams(("parallel", "parallel"), 48),
        name="rglru",
    )(rec, rec, conv_w, conv_b, w_g, b_g, lam, h0)


def _rglru_params(wa, ba, wi, bi, nc):
    bpc = CH // LRU_BLOCK
    n_cb = N_CH // nc
    eye = jnp.eye(bpc, dtype=F32)

    def dense(w):
        w = w.reshape(2, N_CH, bpc, LRU_BLOCK, LRU_BLOCK)
        return (w[:, :, :, :, None, :] * eye[None, None, :, None, :, None]).reshape(2, N_CH, CH, CH)

    w_g = jnp.concatenate([dense(wa), dense(wi)], axis=-1).reshape(2, n_cb, nc * CH, 2 * CH).astype(BF16)
    b_g = jnp.concatenate([ba.reshape(2, N_CH, CH), bi.reshape(2, N_CH, CH)], axis=-1)
    return w_g, b_g.reshape(2, n_cb, 1, nc * 2 * CH)


def _s5_kernel(u_ref, bb_ref, cc_ref, ar_ref, ai_ref, d_ref, s0r_ref, s0i_ref, y_ref, fr_ref, fi_ref,
               ut_scr, yt_scr, sf_scr, sb_scr, *, seq, tc, ns, nc):
    rows = tc * SUBLANES
    n_chunks = seq // tc
    _to_time_major(u_ref, ut_scr, 0, seq, ns, nc)
    yt_scr[...] = jnp.zeros_like(yt_scr)

    def chunk(c, carry):
        rf = pl.multiple_of(c * rows, rows)
        rb = pl.multiple_of((n_chunks - 1 - c) * rows, rows)
        for d, r0, st_scr in ((0, rf, sf_scr), (1, rb, sb_scr)):
            u = _stack_slots(ut_scr[pl.ds(r0, rows), :], ns, nc)
            st_scr[...] = jnp.dot(u.astype(BF16), bb_ref[d, 0], preferred_element_type=F32)

        def update(st_scr, rr, d, s_re, s_im):
            a_re = ar_ref[d, 0]
            a_im = ai_ref[d, 0]
            n_re = a_re * s_re - a_im * s_im + st_scr[rr, :S5_ST]
            n_im = a_re * s_im + a_im * s_re + st_scr[rr, S5_ST:]
            st_scr[rr, :S5_ST] = n_re
            st_scr[rr, S5_ST:] = n_im
            return n_re, n_im

        def step(t, carry):
            f_re, f_im, b_re, b_im = carry
            f_re, f_im = update(sf_scr, _rows(t * SUBLANES, SUBLANES), 0, f_re, f_im)
            b_re, b_im = update(sb_scr, _rows((tc - 1 - t) * SUBLANES, SUBLANES), 1, b_re, b_im)
            return f_re, f_im, b_re, b_im

        carry = lax.fori_loop(0, tc, step, carry, unroll=2)
        for d, r0, st_scr in ((0, rf, sf_scr), (1, rb, sb_scr)):
            y = jnp.dot(st_scr[...].astype(BF16), cc_ref[d, 0], preferred_element_type=F32)
            yt_scr[pl.ds(r0, rows), :] += _pick_slot(y, ns, nc)
        return carry

    f_re, f_im, b_re, b_im = lax.fori_loop(
        0, n_chunks, chunk, (s0r_ref[0, 0, 0], s0i_ref[0, 0, 0], s0r_ref[0, 0, 1], s0i_ref[0, 0, 1]))
    fr_ref[0, 0, 0] = f_re
    fi_ref[0, 0, 0] = f_im
    fr_ref[0, 0, 1] = b_re
    fi_ref[0, 0, 1] = b_im
    for k in range(nc):
        for s in range(ns):
            blk = (slice(s * seq, (s + 1) * seq), slice(k * CH, (k + 1) * CH))
            y_ref[blk] = (d_ref[:, k * CH:(k + 1) * CH] * u_ref[blk]
                          + yt_scr[pl.ds(k * ns + s, seq, stride=SUBLANES), :])


def _s5(rec, bb, cc, a_re, a_im, d_skip, s0_re, s0_im, seq, ns, nc):
    r = rec.shape[0]
    n_cb = N_CH // nc
    tc = 64
    wide = nc * CH
    state = pl.BlockSpec((1, 1, 2, SUBLANES, S5_ST), lambda g, c: (g, c, 0, 0, 0))
    aspec = pl.BlockSpec((2, 1, SUBLANES, S5_ST), lambda g, c: (0, c, 0, 0))
    st_shape = jax.ShapeDtypeStruct((r // (ns * seq), n_cb, 2, SUBLANES, S5_ST), F32)
    return pl.pallas_call(
        functools.partial(_s5_kernel, seq=seq, tc=tc, ns=ns, nc=nc),
        grid=(r // (ns * seq), n_cb),
        in_specs=[pl.BlockSpec((ns * seq, wide), lambda g, c: (g, 2 * n_cb + c)),
                  pl.BlockSpec((2, 1, wide, 2 * S5_ST), lambda g, c: (0, c, 0, 0)),
                  pl.BlockSpec((2, 1, 2 * S5_ST, wide), lambda g, c: (0, c, 0, 0)),
                  aspec, aspec,
                  pl.BlockSpec((1, wide), lambda g, c: (0, c)),
                  state, state],
        out_specs=[pl.BlockSpec((ns * seq, wide), lambda g, c: (g, c)), state, state],
        out_shape=[jax.ShapeDtypeStruct((r, D_S5), F32), st_shape, st_shape],
        scratch_shapes=[pltpu.VMEM((seq * SUBLANES, CH), F32), pltpu.VMEM((seq * SUBLANES, CH), F32),
                        pltpu.VMEM((tc * SUBLANES, 2 * S5_ST), F32), pltpu.VMEM((tc * SUBLANES, 2 * S5_ST), F32)],
        compiler_params=_cparams(("parallel", "parallel"), 48),
        name="s5",
    )(rec, bb, cc, a_re, a_im, d_skip, s0_re, s0_im)


def _s5_params(lam_re, lam_im, log_dt, b_re, b_im, c_re, c_im, ns, nc):
    dt = jnp.exp(log_dt)[..., None]
    mag = jnp.exp(lam_re * dt)
    abar_re, abar_im = mag * jnp.cos(lam_im * dt), mag * jnp.sin(lam_im * dt)
    den = lam_re * lam_re + lam_im * lam_im
    nr, ni = abar_re - 1, abar_im
    cr = (nr * lam_re + ni * lam_im) / den
    ci = (ni * lam_re - nr * lam_im) / den
    bb_re = cr[..., None] * b_re - ci[..., None] * b_im
    bb_im = cr[..., None] * b_im + ci[..., None] * b_re
    gpc = CH // S5_GROUP
    eye = jnp.eye(gpc, dtype=F32)

    def in_map(b):
        b = b.reshape(2, N_CH, gpc, S5_STATE, S5_GROUP).transpose(0, 1, 2, 4, 3)
        return (b[:, :, :, :, None, :] * eye[None, None, :, None, :, None]).reshape(2, N_CH, CH, S5_ST)

    def out_map(c):
        c = c.reshape(2, N_CH, gpc, S5_GROUP, S5_STATE).transpose(0, 1, 2, 4, 3)
        return (c[:, :, :, :, None, :] * eye[None, None, :, None, :, None]).reshape(2, N_CH, S5_ST, CH)

    bb = jnp.concatenate([in_map(bb_re), in_map(bb_im)], axis=-1).astype(BF16)
    cc = jnp.concatenate([out_map(c_re), -out_map(c_im)], axis=-2).astype(BF16)
    n_cb = N_CH // nc
    bb = bb.reshape(2, n_cb, nc * CH, 2 * S5_ST)
    cc = cc.reshape(2, n_cb, nc, 2 * S5_ST, CH).transpose(0, 1, 3, 2, 4).reshape(2, n_cb, 2 * S5_ST, nc * CH)

    def per_slot(a):
        a = jnp.broadcast_to(a.reshape(2, n_cb, nc, 1, S5_ST), (2, n_cb, nc, ns, S5_ST))
        return a.reshape(2, n_cb, SUBLANES, S5_ST)

    return bb, cc, per_slot(abar_re), per_slot(abar_im)


def _out_proj_kernel(x_ref, oa_ref, ob_ref, y_ref, wg_ref, bg_ref, wo_ref, mod_ref, g_ref, *refs, route):
    if route:
        r_ref, _, x1_ref, h2_ref, route_ref = refs
    else:
        x1_ref, h2_ref = refs
    z = jax.nn.gelu(y_ref[...])
    gl = jnp.dot(z.astype(BF16), wg_ref[...], preferred_element_type=F32) + bg_ref[...]
    oc = z * jax.nn.sigmoid(gl)
    mix = (jnp.dot(oa_ref[...], wo_ref[:D_ATTN, :], preferred_element_type=F32)
           + jnp.dot(ob_ref[...].astype(BF16), wo_ref[D_ATTN:D_ATTN + D_LRU, :], preferred_element_type=F32)
           + jnp.dot(oc.astype(BF16), wo_ref[D_ATTN + D_LRU:, :], preferred_element_type=F32))
    x1 = x_ref[...] + mod_ref[0, 2:3, :] * mix
    x1_ref[...] = x1
    h2 = _rms_mod(x1, g_ref[...], mod_ref[0, 3:4, :], mod_ref[0, 4:5, :])
    h2_ref[...] = h2.astype(h2_ref.dtype)
    if route:
        hi = h2.astype(BF16)
        lo = (h2 - hi.astype(F32)).astype(BF16)
        pr = (jnp.dot(hi, r_ref[...], preferred_element_type=F32)
              + jnp.dot(lo, r_ref[...], preferred_element_type=F32))
        logits = pr + pltpu.roll(pr, LANES - N_EXPERTS, axis=1)
        lane = lax.broadcasted_iota(jnp.int32, logits.shape, 1)
        lg = jnp.where(lane < N_EXPERTS, logits, -jnp.inf)
        m1 = jnp.max(lg, axis=-1, keepdims=True)
        i1 = jnp.min(jnp.where(lg == m1, lane, LANES), axis=-1, keepdims=True)
        lg2 = jnp.where(lane == i1, -jnp.inf, lg)
        m2 = jnp.max(lg2, axis=-1, keepdims=True)
        i2 = jnp.min(jnp.where(lg2 == m2, lane, LANES), axis=-1, keepdims=True)
        e2 = jnp.exp(m2 - m1)
        den = 1.0 + e2
        route_ref[...] = (jnp.where(lane == 0, i1.astype(F32), 0.0) + jnp.where(lane == 1, i2.astype(F32), 0.0)
                          + jnp.where(lane == 2, 1.0 / den, 0.0) + jnp.where(lane == 3, e2 / den, 0.0))


def _out_proj(x2d, oa, ob, y, w_glu_bf, b_glu, w_out_bf, mod, mod_base, rows_per_mod, g, routing):
    r, d = x2d.shape
    tm = 512
    tiles_per_mod = rows_per_mod // tm
    route = routing is not None
    row = lambda n: pl.BlockSpec((tm, n), lambda i: (i, 0))
    full = lambda a: pl.BlockSpec(a.shape, lambda i: (0,) * a.ndim, pipeline_mode=pl.Buffered(1))
    args = [x2d, oa, ob, y, w_glu_bf, b_glu, w_out_bf, mod, g]
    in_specs = [row(d), row(D_ATTN), row(D_LRU), row(D_S5), full(w_glu_bf), full(b_glu), full(w_out_bf),
                pl.BlockSpec((1, N_MOD, d), lambda i: (mod_base + i // tiles_per_mod, 0, 0)), full(g)]
    aliases = {}
    if route:
        router_pad, shared_rows, row_base = routing
        args += [router_pad, shared_rows]
        in_specs += [full(router_pad), pl.BlockSpec(memory_space=pl.ANY)]
        aliases = {len(args) - 1: 1}
        out_shape = [jax.ShapeDtypeStruct((r, d), F32), jax.ShapeDtypeStruct(shared_rows.shape, F32),
                     jax.ShapeDtypeStruct((r, LANES), F32)]
        out_specs = [row(d), pl.BlockSpec((tm, d), lambda i: (i + row_base // tm, 0)), row(LANES)]
    else:
        out_shape = [jax.ShapeDtypeStruct((r, d), F32), jax.ShapeDtypeStruct((r, d), BF16)]
        out_specs = [row(d), row(d)]
    return pl.pallas_call(
        functools.partial(_out_proj_kernel, route=route),
        grid=(r // tm,),
        in_specs=in_specs,
        out_specs=out_specs,
        out_shape=out_shape,
        input_output_aliases=aliases,
        compiler_params=_cparams(("parallel",), 56),
        name="out_proj",
    )(*args)


def _swiglu_part(h, w1, w3, w2):
    a = jnp.dot(h, w1, preferred_element_type=F32)
    b = jnp.dot(h, w3, preferred_element_type=F32)
    return jnp.dot((jax.nn.silu(a) * b).astype(BF16), w2, preferred_element_type=F32)


def _ffn_kernel(h_ref, x_ref, mod_ref, w1_ref, w3_ref, w2_ref, o_ref):
    f = pl.program_id(1)
    part = _swiglu_part(h_ref[...], w1_ref[...], w3_ref[...], w2_ref[...])

    @pl.when(f == 0)
    def _():
        o_ref[...] = part

    @pl.when(f > 0)
    def _():
        o_ref[...] += part

    @pl.when(f == pl.num_programs(1) - 1)
    def _():
        o_ref[...] = x_ref[...] + mod_ref[0, 5:6, :] * o_ref[...]


def _ffn(h2, x1, mod, mod_base, rows_per_mod, w1, w3, w2):
    r, d = x1.shape
    tm, tf = 1024, 512
    tiles_per_mod = rows_per_mod // tm
    return pl.pallas_call(
        _ffn_kernel,
        grid=(r // tm, w1.shape[1] // tf),
        in_specs=[pl.BlockSpec((tm, d), lambda i, f: (i, 0)),
                  pl.BlockSpec((tm, d), lambda i, f: (i, 0), pipeline_mode=pl.Buffered(1)),
                  pl.BlockSpec((1, N_MOD, d), lambda i, f: (mod_base + i // tiles_per_mod, 0, 0)),
                  pl.BlockSpec((d, tf), lambda i, f: (0, f)),
                  pl.BlockSpec((d, tf), lambda i, f: (0, f)),
                  pl.BlockSpec((tf, d), lambda i, f: (f, 0))],
        out_specs=pl.BlockSpec((tm, d), lambda i, f: (i, 0)),
        out_shape=jax.ShapeDtypeStruct((r, d), F32),
        compiler_params=_cparams(("parallel", "arbitrary"), 60),
        name="ffn",
    )(h2, x1, mod, w1, w3, w2)


MOE_TM = 512
MOE_TF = 1408
MOE_TT = 256


def _route_plan(route):
    t = route.shape[0]
    n_pairs = 2 * t
    n_tiles = n_pairs // MOE_TM + N_EXPERTS
    experts = route[:, :2].astype(jnp.int32).reshape(n_pairs)
    onehot = (experts[:, None] == jnp.arange(N_EXPERTS, dtype=jnp.int32)[None]).astype(jnp.int32)
    csum = jnp.cumsum(onehot, axis=0)
    rank = jnp.sum(onehot * csum, axis=1) - 1
    counts = csum[-1]
    padded = (counts + MOE_TM - 1) // MOE_TM * MOE_TM
    ends = jnp.cumsum(padded)
    pos = jnp.sum(onehot * (ends - padded)[None], axis=1) + rank
    src = jnp.zeros((n_tiles * MOE_TM,), jnp.int32).at[pos].set(jnp.arange(n_pairs, dtype=jnp.int32) // 2)
    tile_start = jnp.arange(n_tiles, dtype=jnp.int32) * MOE_TM
    tile_valid = (tile_start < ends[-1]).astype(jnp.int32)
    tile_expert = jnp.sum((tile_start[:, None] >= ends[None]).astype(jnp.int32), axis=1)
    last_expert = jnp.max(jnp.where(counts > 0, jnp.arange(N_EXPERTS, dtype=jnp.int32), 0))
    tile_expert = jnp.where(tile_valid == 1, tile_expert, last_expert)
    return pos, src, tile_expert, tile_valid


def _gather_rows(idx_of_row, n_rows, src_hbm, dst, sem):
    def row(r, carry):
        pltpu.make_async_copy(src_hbm.at[pl.ds(idx_of_row(r), 1)], dst.at[pl.ds(r, 1)], sem).start()
        return carry
    lax.fori_loop(0, n_rows, row, 0, unroll=8)


def _wait_rows(n_rows, src_hbm, dst, sem):
    pltpu.make_async_copy(src_hbm.at[pl.ds(0, n_rows)], dst, sem).wait()


def _moe_experts_kernel(src_ref, texp_ref, valid_ref, h_hbm, w1_ref, w3_ref, w2_ref, o_ref, xbuf, xbf, sem,
                        *, rows_per_step):
    i = pl.program_id(0)
    f = pl.program_id(1)
    n_i = pl.num_programs(0)

    @pl.when(f == 0)
    def _():
        @pl.when(i == 0)
        def _():
            _gather_rows(lambda r: src_ref[r], MOE_TM, h_hbm, xbuf, sem.at[0])

        _wait_rows(MOE_TM, h_hbm, xbuf, sem.at[0])
        xbf[...] = xbuf[...].astype(BF16)

    nxt = jnp.where(i + 1 < n_i, i + 1, 0)

    def start_next_rows():
        for r in range(rows_per_step):
            row = f * rows_per_step + r
            pltpu.make_async_copy(h_hbm.at[pl.ds(src_ref[nxt * MOE_TM + row], 1)], xbuf.at[pl.ds(row, 1)],
                                  sem.at[0]).start()

    valid = valid_ref[i] == 1

    @pl.when(valid)
    def _():
        part = _swiglu_part(xbf[...], w1_ref[0], w3_ref[0], w2_ref[0])
        start_next_rows()

        @pl.when(f == 0)
        def _():
            o_ref[...] = part

        @pl.when(f > 0)
        def _():
            o_ref[...] += part

    @pl.when(jnp.logical_not(valid))
    def _():
        start_next_rows()

        @pl.when(f == 0)
        def _():
            o_ref[...] = jnp.zeros_like(o_ref)

    @pl.when((i == n_i - 1) & (f == pl.num_programs(1) - 1))
    def _():
        _wait_rows(MOE_TM, h_hbm, xbuf, sem.at[0])


def _moe_experts(h2, src, tile_expert, tile_valid, w1, w3, w2):
    t, d = h2.shape
    n_tiles = tile_expert.shape[0]
    n_f = w1.shape[2] // MOE_TF

    def wmap(i, f, src, texp, valid):
        return (texp[i], 0, jnp.where(valid[i] == 1, f, n_f - 1))

    def w2map(i, f, src, texp, valid):
        return (texp[i], jnp.where(valid[i] == 1, f, n_f - 1), 0)

    grid_spec = pltpu.PrefetchScalarGridSpec(
        num_scalar_prefetch=3,
        grid=(n_tiles, n_f),
        in_specs=[pl.BlockSpec(memory_space=pl.ANY),
                  pl.BlockSpec((1, d, MOE_TF), wmap),
                  pl.BlockSpec((1, d, MOE_TF), wmap),
                  pl.BlockSpec((1, MOE_TF, d), w2map)],
        out_specs=pl.BlockSpec((MOE_TM, d), lambda i, f, *_: (i, 0)),
        scratch_shapes=[pltpu.VMEM((MOE_TM, d), F32), pltpu.VMEM((MOE_TM, d), BF16),
                        pltpu.SemaphoreType.DMA((1,))])
    return pl.pallas_call(
        functools.partial(_moe_experts_kernel, rows_per_step=MOE_TM // n_f),
        grid_spec=grid_spec,
        out_shape=jax.ShapeDtypeStruct((n_tiles * MOE_TM, d), F32),
        compiler_params=_cparams(("arbitrary", "arbitrary"), 62),
        name="moe_experts",
    )(src, tile_expert, tile_valid, h2, w1, w3, w2)


def _moe_combine_kernel(pos_ref, x_ref, mod_ref, route_ref, y_hbm, o_ref, buf, sem, *, tok_base):
    i = pl.program_id(0)
    slot = i % 2

    def gather(tile, slot):
        for k in range(2):
            _gather_rows(lambda r, k=k: pos_ref[2 * (tok_base + tile * MOE_TT + r) + k], MOE_TT, y_hbm,
                         buf.at[slot, k], sem.at[slot])

    @pl.when(i == 0)
    def _():
        gather(0, 0)

    for k in range(2):
        _wait_rows(MOE_TT, y_hbm, buf.at[slot, k], sem.at[slot])

    @pl.when(i + 1 < pl.num_programs(0))
    def _():
        gather(i + 1, 1 - slot)

    mixed = route_ref[:, 2:3] * buf[slot, 0] + route_ref[:, 3:4] * buf[slot, 1]
    o_ref[...] = x_ref[...] + mod_ref[0, 5:6, :] * mixed


def _moe_combine(pos, x1, mod, mod_base, rows_per_mod, route, y_rows, tok_base):
    r, d = x1.shape
    tiles_per_mod = rows_per_mod // MOE_TT
    grid_spec = pltpu.PrefetchScalarGridSpec(
        num_scalar_prefetch=1,
        grid=(r // MOE_TT,),
        in_specs=[pl.BlockSpec((MOE_TT, d), lambda i, *_: (i, 0)),
                  pl.BlockSpec((1, N_MOD, d), lambda i, *_: (mod_base + i // tiles_per_mod, 0, 0)),
                  pl.BlockSpec((MOE_TT, LANES), lambda i, *_: (i, 0)),
                  pl.BlockSpec(memory_space=pl.ANY)],
        out_specs=pl.BlockSpec((MOE_TT, d), lambda i, *_: (i, 0)),
        scratch_shapes=[pltpu.VMEM((2, 2, MOE_TT, d), F32), pltpu.SemaphoreType.DMA((2,))])
    return pl.pallas_call(
        functools.partial(_moe_combine_kernel, tok_base=tok_base),
        grid_spec=grid_spec,
        out_shape=jax.ShapeDtypeStruct((r, d), F32),
        compiler_params=_cparams(("arbitrary",), 32),
        name="moe_combine",
    )(pos, x1, mod, route, y_rows)


def _slot_states(s, ns, nc, width):
    ng = s.shape[0] // ns
    s = s.reshape(ng, ns, 2, N_CH // nc, nc, width).transpose(0, 3, 2, 4, 1, 5)
    return s.reshape(ng, N_CH // nc, 2, SUBLANES, width)


def _unslot_states(s, ns, nc, width):
    ng = s.shape[0]
    s = s.reshape(ng, N_CH // nc, 2, nc, ns, width).transpose(0, 4, 2, 1, 3, 5)
    return s.reshape(ng * ns, 2, N_CH * width)


def _token_mixer(x2d, batch, seq, mod, mod_base, rows_per_mod, p, kv_ctx, lru_h0, s5_h0, prev_kv=None,
                 routing=None):
    is_ctx = kv_ctx is None
    ns = min(batch, SUBLANES)
    nc = SUBLANES // ns
    outs = _in_proj(x2d, p['g_mix'], mod, mod_base, rows_per_mod, p['w_in'], p['q_g'], p['k_g'], p['ones_bd'],
                    seq, is_ctx, prev_kv)
    q, k, v, rec = outs[:4]
    if is_ctx:
        oa = _ctx_attn(q, k, v, seq)
        lru_h0 = jnp.zeros((batch, 2, D_LRU), F32)
        s5_h0 = (jnp.zeros((batch, 2, N_S5_GROUPS * S5_STATE), F32),) * 2
    else:
        oa = _nbr_attn(q, k, v, kv_ctx[0], kv_ctx[1], kv_ctx[2], p['tb'], seq)
    lru_w, lru_b = _rglru_params(p['lru_wa'], p['lru_ba'], p['lru_wi'], p['lru_bi'], nc)
    ob, lru_fin = _rglru(rec, p['conv_w'], p['conv_b'], lru_w, lru_b, p['lru_lam'],
                         _slot_states(lru_h0, ns, nc, CH), seq, ns, nc)
    s5_bb, s5_cc, s5_are, s5_aim = _s5_params(*p['s5'], ns, nc)
    y, fin_re, fin_im = _s5(rec, s5_bb, s5_cc, s5_are, s5_aim, p['d_skip'],
                            _slot_states(s5_h0[0], ns, nc, S5_ST), _slot_states(s5_h0[1], ns, nc, S5_ST),
                            seq, ns, nc)
    res = _out_proj(x2d, oa, ob, y, p['w_glu'], p['b_glu'], p['w_out'], mod, mod_base, rows_per_mod,
                    p['g_ffn'], routing)
    if not is_ctx:
        return res, None
    state = (outs[4], outs[5], _unslot_states(lru_fin, ns, nc, CH),
             _unslot_states(fin_re, ns, nc, S5_ST).reshape(batch, 2, N_S5_GROUPS, S5_STATE),
             _unslot_states(fin_im, ns, nc, S5_ST).reshape(batch, 2, N_S5_GROUPS, S5_STATE))
    return res, state


def kernel(x_prompt, x_sample, c, cache_k, cache_v, state_lru, state_s5_re, state_s5_im, c_ctx, norm_mix_g, norm_ffn_g, w_mod, b_mod, w_in, w_out, q_norm_g, k_norm_g, rpb, lru_conv_w, lru_conv_b, lru_wa, lru_ba, lru_wi, lru_bi, lru_lam, s5_lam_re, s5_lam_im, s5_log_dt, s5_b_re, s5_b_im, s5_c_re, s5_c_im, s5_d, s5_w_glu, s5_b_glu, ffn_w1, ffn_w3, ffn_w2, moe_router, moe_w1, moe_w3, moe_w2):
    batch, seq, d = x_prompt.shape
    dec_batch, dec_seq, _ = x_sample.shape
    depth = w_in.shape[0]
    assert dec_batch + 1 <= SUBLANES and batch % SUBLANES == 0 and dec_batch in (4, 8)

    cvecs = jnp.concatenate([c_ctx[None], c, jnp.zeros((SUBLANES - 1 - dec_batch, d), F32)], axis=0)
    mods = _adaln(cvecs, w_mod, b_mod).reshape(depth, SUBLANES, N_MOD, d)

    heads_per_tile = 512 // HEAD_DIM
    ones_bd = jnp.asarray(np.kron(np.eye(heads_per_tile // 2), np.ones((HEAD_DIM, HEAD_DIM))), BF16)

    xp = x_prompt.reshape(batch * seq, d)
    xs = x_sample.reshape(dec_batch * dec_seq, d)
    n_tok = (batch * seq, dec_batch * dec_seq)
    kv, lrus, s5rs, s5is = None, [], [], []
    for l in range(depth):
        p = {
            'g_mix': norm_mix_g[l][None], 'g_ffn': norm_ffn_g[l][None],
            'w_in': w_in[l].astype(BF16), 'w_out': w_out[l].astype(BF16),
            'q_g': jnp.tile(q_norm_g[l], heads_per_tile)[None], 'k_g': jnp.tile(k_norm_g[l], heads_per_tile)[None],
            'ones_bd': ones_bd, 'tb': _bias_table(rpb[l]),
            'conv_w': lru_conv_w[l], 'conv_b': lru_conv_b[l][None],
            'lru_wa': lru_wa[l], 'lru_ba': lru_ba[l], 'lru_wi': lru_wi[l], 'lru_bi': lru_bi[l], 'lru_lam': lru_lam[l],
            's5': (s5_lam_re[l], s5_lam_im[l], s5_log_dt[l], s5_b_re[l], s5_b_im[l], s5_c_re[l], s5_c_im[l]),
            'd_skip': s5_d[l][None], 'w_glu': s5_w_glu[l].astype(BF16), 'b_glu': s5_b_glu[l][None],
        }
        j = l // 2
        dense = l % 2 == 0
        route_p = route_s = None
        if not dense:
            r_hi = moe_router[j].astype(BF16)
            r_lo = (moe_router[j] - r_hi.astype(F32)).astype(BF16)
            router = jnp.pad(jnp.concatenate([r_hi, r_lo], axis=1), ((0, 0), (0, LANES - 2 * N_EXPERTS)))
            route_p = (router, jnp.zeros((sum(n_tok), d), F32), 0)
        res_p, (k_c, v_c, lru_c, s5r_c, s5i_c) = _token_mixer(
            xp, batch, seq, mods[l], 0, batch * seq, p, None, None, None, prev_kv=kv, routing=route_p)
        kv = (k_c, v_c)
        lrus.append(lru_c)
        s5rs.append(s5r_c)
        s5is.append(s5i_c)
        if not dense:
            route_s = (router, res_p[1], n_tok[0])
        res_s, _ = _token_mixer(
            xs, dec_batch, dec_seq, mods[l], 1, dec_seq, p, (cache_k, cache_v, l), state_lru[:, l],
            (state_s5_re[:, l].reshape(dec_batch, 2, -1), state_s5_im[:, l].reshape(dec_batch, 2, -1)),
            routing=route_s)
        if dense:
            w = (ffn_w1[j].astype(BF16), ffn_w3[j].astype(BF16), ffn_w2[j].astype(BF16))
            xp = _ffn(res_p[1], res_p[0], mods[l], 0, batch * seq, *w)
            xs = _ffn(res_s[1], res_s[0], mods[l], 1, dec_seq, *w)
        else:
            h2 = res_s[1]
            pos, src, tile_expert, tile_valid = _route_plan(jnp.concatenate([res_p[2], res_s[2]], axis=0))
            y_rows = _moe_experts(h2, src, tile_expert, tile_valid,
                                  moe_w1[j].astype(BF16), moe_w3[j].astype(BF16), moe_w2[j].astype(BF16))
            xp = _moe_combine(pos, res_p[0], mods[l], 0, batch * seq, res_p[2], y_rows, 0)
            xs = _moe_combine(pos, res_s[0], mods[l], 1, dec_seq, res_s[2], y_rows, batch * seq)
    return (xp.reshape(batch, seq, d), xs.reshape(dec_batch, dec_seq, d), kv[0], kv[1],
            jnp.stack(lrus, axis=1), jnp.stack(s5rs, axis=1), jnp.stack(s5is, axis=1))
```

```python
import functools

import numpy as np
import jax
import jax.numpy as jnp
from jax import lax
from jax.experimental import pallas as pl
from jax.experimental.pallas import tpu as pltpu

F32 = jnp.float32
BF16 = jnp.bfloat16

D_MODEL = 2048
N_HEADS = 16
HEAD_DIM = 64
D_ATTN = N_HEADS * HEAD_DIM
GRID_W = 64
WIN_R = 8
WIN_C = 16
D_LRU = 512
LRU_BLOCK = 64
CONV_W = 4
LRU_C = 8.0
D_S5 = 512
S5_GROUP = 16
N_S5_GROUPS = 32
S5_STATE = 64
D_REC = 2 * D_LRU + D_S5
D_IN = 3 * D_ATTN + D_REC
N_MOD = 6
N_EXPERTS = 8
EPS = 1e-6
NEG = -1e30

LANES = 128
SUBLANES = 8
MIB = 1024 * 1024

CH = 128
N_CH = D_LRU // CH
S5_ST = (CH // S5_GROUP) * S5_STATE


def _cparams(sem, vmem_mib):
    return pltpu.CompilerParams(dimension_semantics=sem, vmem_limit_bytes=vmem_mib * MIB)


def _adaln_kernel(c_ref, w_ref, b_ref, o_ref):
    s = jax.nn.silu(c_ref[...]).astype(BF16)
    o_ref[0] = jnp.dot(s, w_ref[0].astype(BF16), preferred_element_type=F32) + b_ref[0]


def _adaln(cvecs, w_mod, b_mod):
    depth, d, n = w_mod.shape
    tn = 1024
    return pl.pallas_call(
        _adaln_kernel,
        grid=(depth, n // tn),
        in_specs=[pl.BlockSpec((SUBLANES, d), lambda l, j: (0, 0)),
                  pl.BlockSpec((1, d, tn), lambda l, j: (l, 0, j)),
                  pl.BlockSpec((1, 1, tn), lambda l, j: (l, 0, j))],
        out_specs=pl.BlockSpec((1, SUBLANES, tn), lambda l, j: (l, 0, j)),
        out_shape=jax.ShapeDtypeStruct((depth, SUBLANES, n), F32),
        compiler_params=_cparams(("parallel", "parallel"), 40),
        name="adaln",
    )(cvecs, w_mod, b_mod.reshape(depth, 1, n))


def _rms_mod(x, g, shift, scale):
    xf = x * lax.rsqrt(jnp.mean(x * x, axis=-1, keepdims=True) + EPS)
    return (xf * g) * (1 + scale) + shift


def _in_proj_kernel(x_ref, g_ref, mod_ref, w_ref, qg_ref, kg_ref, ones_ref, *refs, seq, write_cache, n_prev):
    kp_ref = vp_ref = None
    if write_cache and n_prev:
        kp_ref, vp_ref, *refs = refs
    if write_cache:
        q_ref, k_ref, v_ref, rec_ref, kc_ref, vc_ref, h_scr = refs
    else:
        q_ref, k_ref, v_ref, rec_ref, h_scr = refs
    j = pl.program_id(1)
    tm, tn = q_ref.shape
    heads = tn // HEAD_DIM

    @pl.when(j == 0)
    def _():
        h_scr[...] = _rms_mod(x_ref[...], g_ref[...], mod_ref[0, 0:1, :], mod_ref[0, 1:2, :]).astype(BF16)

    y = jnp.dot(h_scr[...], w_ref[...], preferred_element_type=F32)

    def to_cache(c_ref, prev_ref, val):
        if prev_ref is not None:
            c_ref[:, :n_prev] = prev_ref[...]
        for b in range(tm // seq):
            for h in range(heads):
                c_ref[b, n_prev, h, :, :] = val[b * seq:(b + 1) * seq, h * HEAD_DIM:(h + 1) * HEAD_DIM]

    @pl.when(j < 4)
    def _():
        y2 = y * y
        hi = y2.astype(BF16)
        lo = (y2 - hi.astype(F32)).astype(BF16)
        half = ones_ref.shape[0]
        ss = jnp.concatenate(
            [jnp.dot(hi[:, c:c + half], ones_ref[...], preferred_element_type=F32)
             + jnp.dot(lo[:, c:c + half], ones_ref[...], preferred_element_type=F32) for c in range(0, tn, half)],
            axis=1)
        gain = jnp.where(j < 2, qg_ref[...], kg_ref[...])
        yn = (y * lax.rsqrt(ss * (1.0 / HEAD_DIM) + EPS)) * gain

        @pl.when(j < 2)
        def _():
            q_ref[...] = yn.astype(BF16)

        @pl.when(j >= 2)
        def _():
            k_ref[...] = yn.astype(BF16)
            if write_cache:
                to_cache(kc_ref, kp_ref, yn)

    @pl.when((j >= 4) & (j < 6))
    def _():
        v_ref[...] = y.astype(BF16)
        if write_cache:
            to_cache(vc_ref, vp_ref, y)

    @pl.when(j >= 6)
    def _():
        rec_ref[...] = y


def _in_proj(x2d, g, mod, mod_base, rows_per_mod, w_bf, qg, kg, ones_bd, seq, write_cache, prev_kv):
    r, d = x2d.shape
    tm, tn = 512, 512
    n_i = r // tm
    tiles_per_mod = rows_per_mod // tm
    bpt = tm // seq
    hpt = tn // HEAD_DIM
    n_prev = prev_kv[0].shape[1] if prev_kv else 0

    def col(lo, n):
        return lambda i, j: (i, jnp.clip(j - lo, 0, n - 1))

    def cache_spec(lo, n_layers):
        return pl.BlockSpec((bpt, n_layers, hpt, seq, HEAD_DIM), lambda i, j: (i, 0, jnp.clip(j - lo, 0, 1), 0, 0))

    args = [x2d, g, mod, w_bf, qg, kg, ones_bd]
    in_specs = [pl.BlockSpec((tm, d), lambda i, j: (i, 0)),
                pl.BlockSpec((1, d), lambda i, j: (0, 0)),
                pl.BlockSpec((1, N_MOD, d), lambda i, j: (mod_base + i // tiles_per_mod, 0, 0)),
                pl.BlockSpec((d, tn), lambda i, j: (0, j)),
                pl.BlockSpec((1, tn), lambda i, j: (0, 0)),
                pl.BlockSpec((1, tn), lambda i, j: (0, 0)),
                pl.BlockSpec(ones_bd.shape, lambda i, j: (0, 0))]
    out_shape = [jax.ShapeDtypeStruct((r, D_ATTN), BF16)] * 3 + [jax.ShapeDtypeStruct((r, D_REC), F32)]
    out_specs = [pl.BlockSpec((tm, tn), col(0, 2)), pl.BlockSpec((tm, tn), col(2, 2)),
                 pl.BlockSpec((tm, tn), col(4, 2)), pl.BlockSpec((tm, tn), col(6, 3))]
    if write_cache:
        if n_prev:
            args += list(prev_kv)
            in_specs += [cache_spec(2, n_prev), cache_spec(4, n_prev)]
        cshape = jax.ShapeDtypeStruct((r // seq, n_prev + 1, N_HEADS, seq, HEAD_DIM), F32)
        out_shape += [cshape, cshape]
        out_specs += [cache_spec(2, n_prev + 1), cache_spec(4, n_prev + 1)]
    return pl.pallas_call(
        functools.partial(_in_proj_kernel, seq=seq, write_cache=write_cache, n_prev=n_prev),
        grid=(n_i, D_IN // tn),
        in_specs=in_specs,
        out_specs=out_specs,
        out_shape=out_shape,
        scratch_shapes=[pltpu.VMEM((tm, d), BF16)],
        compiler_params=_cparams(("parallel", "arbitrary"), 56),
        name="in_proj",
    )(*args)


def _qk(q, k):
    return lax.dot_general(q, k, (((1,), (1,)), ((), ())), preferred_element_type=F32)


def _head_lanes(n_rows, first):
    lane = lax.broadcasted_iota(jnp.int32, (n_rows, LANES), 1)
    return (lane < HEAD_DIM) if first else (lane >= HEAD_DIM)


def _own(x, first):
    return jnp.where(_head_lanes(x.shape[0], first), x, jnp.zeros_like(x))


def _values_and_ones(v, first):
    ones = jnp.where(_head_lanes(v.shape[0], first), 1.0, 0.0).astype(BF16)
    return jnp.concatenate([_own(v, first), ones], axis=1)


def _exp_rows(parts):
    m = functools.reduce(jnp.maximum, [jnp.max(s, axis=-1, keepdims=True) for s in parts])
    return [jnp.exp(s - m).astype(BF16) for s in parts]


def _ctx_attn_kernel(q_ref, k_ref, v_ref, o_ref):
    scale = HEAD_DIM ** -0.5
    for hp in range(N_HEADS * HEAD_DIM // LANES):
        sl = slice(hp * LANES, (hp + 1) * LANES)
        q, k, v = q_ref[:, sl], k_ref[:, sl], v_ref[:, sl]
        nd = None
        for first in (True, False):
            (e,) = _exp_rows([_qk(_own(q, first), k) * scale])
            part = jnp.dot(e, _values_and_ones(v, first), preferred_element_type=F32)
            nd = part if nd is None else nd + part
        o_ref[:, sl] = (nd[:, :LANES] / nd[:, LANES:]).astype(BF16)


def _ctx_attn(q, k, v, seq):
    r = q.shape[0]
    spec = pl.BlockSpec((seq, D_ATTN), lambda b: (b, 0))
    return pl.pallas_call(
        _ctx_attn_kernel,
        grid=(r // seq,),
        in_specs=[spec, spec, spec],
        out_specs=spec,
        out_shape=jax.ShapeDtypeStruct((r, D_ATTN), BF16),
        compiler_params=_cparams(("parallel",), 32),
        name="ctx_attn",
    )(q, k, v)


def _nbr_plan(seq):
    rows = seq // GRID_W
    wr = min(WIN_R, rows)
    row_start = np.clip(np.arange(rows) - wr // 2, 0, rows - wr)
    rows_per_blk = 256 // GRID_W
    ranges = []
    for qb in range(rows // rows_per_blk):
        rs = row_start[qb * rows_per_blk:(qb + 1) * rows_per_blk]
        lo = int(rs.min()) * GRID_W // LANES * LANES
        hi = -(-(int(rs.max()) + wr) * GRID_W // LANES) * LANES
        ranges.append((lo, hi))
    return rows, wr, row_start, ranges


def _nbr_attn_kernel(q_ref, k_ref, v_ref, kc_ref, vc_ref, tb_ref, o_ref, bias_scr, *, seq):
    rows, wr, row_start, ranges = _nbr_plan(seq)
    scale = HEAD_DIM ** -0.5
    heads = q_ref.shape[1] // HEAD_DIM
    neg = jnp.full((GRID_W, GRID_W), NEG, F32)

    @pl.when(pl.program_id(1) == 0)
    def _():
        for h in range(heads):
            for qr in range(rows):
                for kp in range(rows // 2):
                    blks = []
                    for kr in (2 * kp, 2 * kp + 1):
                        inside = row_start[qr] <= kr < row_start[qr] + wr
                        blks.append(tb_ref[h, kr - qr + WIN_R - 1] if inside else neg)
                    bias_scr[h, qr * GRID_W:(qr + 1) * GRID_W, kp * LANES:(kp + 1) * LANES] = (
                        jnp.concatenate(blks, axis=1))

    kc = jnp.concatenate([kc_ref[0, 0, h] for h in range(heads)], axis=1).astype(BF16)
    vc = jnp.concatenate([vc_ref[0, 0, h] for h in range(heads)], axis=1).astype(BF16)
    for qb, (lo, hi) in enumerate(ranges):
        qs = slice(qb * 256, (qb + 1) * 256)
        q = q_ref[qs, :]
        nd = None
        for h, first in enumerate((True, False)):
            qh = _own(q, first)
            s_loc = _qk(qh, k_ref[lo:hi, :]) * scale + bias_scr[h, qs, lo:hi]
            s_ctx = _qk(qh, kc) * scale
            e_loc, e_ctx = _exp_rows([s_loc, s_ctx])
            part = (jnp.dot(e_loc, _values_and_ones(v_ref[lo:hi, :], first), preferred_element_type=F32)
                    + jnp.dot(e_ctx, _values_and_ones(vc, first), preferred_element_type=F32))
            nd = part if nd is None else nd + part
        o_ref[qs, :] = (nd[:, :LANES] / nd[:, LANES:]).astype(BF16)


def _nbr_attn(q, k, v, cache_k, cache_v, layer, tb, seq):
    r = q.shape[0]
    past = cache_k.shape[3]
    hp = LANES // HEAD_DIM
    spec = pl.BlockSpec((seq, LANES), lambda g, b: (b, g))
    cspec = pl.BlockSpec((1, 1, hp, past, HEAD_DIM), lambda g, b: (b, layer, g, 0, 0))
    return pl.pallas_call(
        functools.partial(_nbr_attn_kernel, seq=seq),
        grid=(N_HEADS // hp, r // seq),
        in_specs=[spec, spec, spec, cspec, cspec,
                  pl.BlockSpec((hp,) + tb.shape[1:], lambda g, b: (g, 0, 0, 0))],
        out_specs=spec,
        out_shape=jax.ShapeDtypeStruct((r, D_ATTN), BF16),
        scratch_shapes=[pltpu.VMEM((hp, seq, seq), F32)],
        compiler_params=_cparams(("parallel", "arbitrary"), 48),
        name="nbr_attn",
    )(q, k, v, cache_k, cache_v, tb)


def _bias_table(rpb_l):
    col = np.arange(GRID_W)
    col_start = np.clip(col - WIN_C // 2, 0, GRID_W - WIN_C)
    col_in = (col[None, :] >= col_start[:, None]) & (col[None, :] < col_start[:, None] + WIN_C)
    dc_idx = np.clip(col[None, :] - col[:, None] + WIN_C - 1, 0, 2 * WIN_C - 2)
    onehot = (dc_idx.reshape(1, -1) == np.arange(2 * WIN_C - 1)[:, None]).astype(np.float32)
    tb = jnp.einsum('hrc,cq->hrq', rpb_l, onehot, precision=lax.Precision.HIGHEST)
    tb = tb.reshape(rpb_l.shape[0], rpb_l.shape[1], GRID_W, GRID_W)
    return jnp.where(col_in[None, None], tb, NEG).astype(F32)


def _expm1(x):
    u = jnp.exp(x)
    um1 = u - 1.0
    near = um1 * x / jnp.where(u == 1.0, 1.0, jnp.log(u))
    return jnp.where(x < -0.5, um1, jnp.where(u == 1.0, x, near))


def _slot0(n_rows, n_cols, ns):
    return (lax.broadcasted_iota(jnp.int32, (n_rows, n_cols), 0) % SUBLANES) < ns


def _per_slot(vec, n_rows, ns, nc):
    if nc == 1:
        return vec
    w = vec.shape[1] // 2
    return jnp.where(_slot0(n_rows, w, ns), vec[:, :w], vec[:, w:])


def _stack_slots(x, ns, nc):
    if nc == 1:
        return x
    m = _slot0(x.shape[0], x.shape[1], ns)
    zero = jnp.zeros_like(x)
    return jnp.concatenate([jnp.where(m, x, zero), jnp.where(m, zero, x)], axis=1)


def _pick_slot(y, ns, nc):
    if nc == 1:
        return y
    w = y.shape[1] // 2
    return jnp.where(_slot0(y.shape[0], w, ns), y[:, :w], y[:, w:])


def _to_time_major(x_ref, t_scr, row0, seq, ns, nc):
    for k in range(nc):
        for s in range(ns):
            t_scr[pl.ds(row0 + k * ns + s, seq, stride=SUBLANES), :] = (
                x_ref[s * seq:(s + 1) * seq, k * CH:(k + 1) * CH])


def _rows(start, n):
    return pl.ds(pl.multiple_of(start, SUBLANES), n)


def _rglru_kernel(x_ref, g_ref, cw_ref, cb_ref, w_ref, b_ref, lam_ref, h0_ref, o_ref, fin_ref,
                  xt_scr, yt_scr, af_scr, bf_scr, ab_scr, bb_scr, *, seq, tc, ns, nc):
    rows = tc * SUBLANES
    n_chunks = seq // tc
    front = (CONV_W // 2) * SUBLANES
    back = (CONV_W - 1 - CONV_W // 2) * SUBLANES

    xt_scr[0:front, :] = jnp.zeros((front, CH), F32)
    xt_scr[front + seq * SUBLANES:front + seq * SUBLANES + back, :] = jnp.zeros((back, CH), F32)
    _to_time_major(x_ref, xt_scr, front, seq, ns, nc)
    yt_scr[...] = jnp.zeros_like(yt_scr)

    def coefficients(r0, d, a_scr, b_scr):
        xc = _per_slot(cb_ref[...], rows, ns, nc)
        for j in range(CONV_W):
            xc = xc + xt_scr[_rows(r0 + j * SUBLANES, rows), :] * _per_slot(cw_ref[j:j + 1, :], rows, ns, nc)
        gates = (jnp.dot(_stack_slots(xc, ns, nc).astype(BF16), w_ref[d, 0], preferred_element_type=F32)
                 + _per_slot(b_ref[d, 0], rows, ns, nc))
        rg = jax.nn.sigmoid(gates[:, :CH])
        ig = jax.nn.sigmoid(gates[:, CH:])
        log_a = -LRU_C * rg * _per_slot(jax.nn.softplus(-lam_ref[d:d + 1, :]), rows, ns, nc)
        a_scr[...] = jnp.exp(log_a)
        b_scr[...] = jnp.sqrt(-_expm1(2 * log_a)) * (ig * xc)

    def chunk(c, carry):
        rf = pl.multiple_of(c * rows, rows)
        rb = pl.multiple_of((n_chunks - 1 - c) * rows, rows)
        coefficients(rf, 0, af_scr, bf_scr)
        coefficients(rb, 1, ab_scr, bb_scr)

        def step(t, carry):
            h_f, h_b = carry
            tf = _rows(t * SUBLANES, SUBLANES)
            tb = _rows((tc - 1 - t) * SUBLANES, SUBLANES)
            h_f = af_scr[tf, :] * h_f + bf_scr[tf, :]
            h_b = ab_scr[tb, :] * h_b + bb_scr[tb, :]
            bf_scr[tf, :] = h_f
            bb_scr[tb, :] = h_b
            return h_f, h_b

        carry = lax.fori_loop(0, tc, step, carry, unroll=8)
        yt_scr[pl.ds(rf, rows), :] += bf_scr[...]
        yt_scr[pl.ds(rb, rows), :] += bb_scr[...]
        return carry

    h_f, h_b = lax.fori_loop(0, n_chunks, chunk, (h0_ref[0, 0, 0], h0_ref[0, 0, 1]))
    fin_ref[0, 0, 0] = h_f
    fin_ref[0, 0, 1] = h_b
    for k in range(nc):
        for s in range(ns):
            blk = (slice(s * seq, (s + 1) * seq), slice(k * CH, (k + 1) * CH))
            o_ref[blk] = yt_scr[pl.ds(k * ns + s, seq, stride=SUBLANES), :] * jax.nn.gelu(g_ref[blk])


def _rglru(rec, conv_w, conv_b, w_g, b_g, lam, h0, seq, ns, nc):
    r = rec.shape[0]
    n_cb = N_CH // nc
    tc = 128
    wide = nc * CH
    n_pad = (seq + CONV_W - 1) * SUBLANES
    state = pl.BlockSpec((1, 1, 2, SUBLANES, CH), lambda g, c: (g, c, 0, 0, 0))
    return pl.pallas_call(
        functools.partial(_rglru_kernel, seq=seq, tc=tc, ns=ns, nc=nc),
        grid=(r // (ns * seq), n_cb),
        in_specs=[pl.BlockSpec((ns * seq, wide), lambda g, c: (g, c)),
                  pl.BlockSpec((ns * seq, wide), lambda g, c: (g, n_cb + c)),
                  pl.BlockSpec((CONV_W, wide), lambda g, c: (0, c)),
                  pl.BlockSpec((1, wide), lambda g, c: (0, c)),
                  pl.BlockSpec((2, 1, wide, 2 * CH), lambda g, c: (0, c, 0, 0)),
                  pl.BlockSpec((2, 1, 1, 2 * wide), lambda g, c: (0, c, 0, 0)),
                  pl.BlockSpec((2, wide), lambda g, c: (0, c)),
                  state],
        out_specs=[pl.BlockSpec((ns * seq, wide), lambda g, c: (g, c)), state],
        out_shape=[jax.ShapeDtypeStruct((r, D_LRU), F32),
                   jax.ShapeDtypeStruct((r // (ns * seq), n_cb, 2, SUBLANES, CH), F32)],
        scratch_shapes=[pltpu.VMEM((n_pad, CH), F32), pltpu.VMEM((seq * SUBLANES, CH), F32)]
        + [pltpu.VMEM((tc * SUBLANES, CH), F32)] * 4,
        compiler_params=_cparams(("parallel", "parallel"), 48),
        name="rglru",
    )(rec, rec, conv_w, conv_b, w_g, b_g, lam, h0)


def _rglru_params(wa, ba, wi, bi, nc):
    bpc = CH // LRU_BLOCK
    n_cb = N_CH // nc
    eye = jnp.eye(bpc, dtype=F32)

    def dense(w):
        w = w.reshape(2, N_CH, bpc, LRU_BLOCK, LRU_BLOCK)
        return (w[:, :, :, :, None, :] * eye[None, None, :, None, :, None]).reshape(2, N_CH, CH, CH)

    w_g = jnp.concatenate([dense(wa), dense(wi)], axis=-1).reshape(2, n_cb, nc * CH, 2 * CH).astype(BF16)
    b_g = jnp.concatenate([ba.reshape(2, N_CH, CH), bi.reshape(2, N_CH, CH)], axis=-1)
    return w_g, b_g.reshape(2, n_cb, 1, nc * 2 * CH)


def _s5_kernel(u_ref, bb_ref, cc_ref, ar_ref, ai_ref, d_ref, s0r_ref, s0i_ref, y_ref, fr_ref, fi_ref,
               ut_scr, yt_scr, sf_scr, sb_scr, *, seq, tc, ns, nc):
    rows = tc * SUBLANES
    n_chunks = seq // tc
    _to_time_major(u_ref, ut_scr, 0, seq, ns, nc)
    yt_scr[...] = jnp.zeros_like(yt_scr)

    def chunk(c, carry):
        rf = pl.multiple_of(c * rows, rows)
        rb = pl.multiple_of((n_chunks - 1 - c) * rows, rows)
        for d, r0, st_scr in ((0, rf, sf_scr), (1, rb, sb_scr)):
            u = _stack_slots(ut_scr[pl.ds(r0, rows), :], ns, nc)
            st_scr[...] = jnp.dot(u.astype(BF16), bb_ref[d, 0], preferred_element_type=F32)

        def update(st_scr, rr, d, s_re, s_im):
            a_re = ar_ref[d, 0]
            a_im = ai_ref[d, 0]
            n_re = a_re * s_re - a_im * s_im + st_scr[rr, :S5_ST]
            n_im = a_re * s_im + a_im * s_re + st_scr[rr, S5_ST:]
            st_scr[rr, :S5_ST] = n_re
            st_scr[rr, S5_ST:] = n_im
            return n_re, n_im

        def step(t, carry):
            f_re, f_im, b_re, b_im = carry
            f_re, f_im = update(sf_scr, _rows(t * SUBLANES, SUBLANES), 0, f_re, f_im)
            b_re, b_im = update(sb_scr, _rows((tc - 1 - t) * SUBLANES, SUBLANES), 1, b_re, b_im)
            return f_re, f_im, b_re, b_im

        carry = lax.fori_loop(0, tc, step, carry, unroll=2)
        for d, r0, st_scr in ((0, rf, sf_scr), (1, rb, sb_scr)):
            y = jnp.dot(st_scr[...].astype(BF16), cc_ref[d, 0], preferred_element_type=F32)
            yt_scr[pl.ds(r0, rows), :] += _pick_slot(y, ns, nc)
        return carry

    f_re, f_im, b_re, b_im = lax.fori_loop(
        0, n_chunks, chunk, (s0r_ref[0, 0, 0], s0i_ref[0, 0, 0], s0r_ref[0, 0, 1], s0i_ref[0, 0, 1]))
    fr_ref[0, 0, 0] = f_re
    fi_ref[0, 0, 0] = f_im
    fr_ref[0, 0, 1] = b_re
    fi_ref[0, 0, 1] = b_im
    for k in range(nc):
        for s in range(ns):
            blk = (slice(s * seq, (s + 1) * seq), slice(k * CH, (k + 1) * CH))
            y_ref[blk] = (d_ref[:, k * CH:(k + 1) * CH] * u_ref[blk]
                          + yt_scr[pl.ds(k * ns + s, seq, stride=SUBLANES), :])


def _s5(rec, bb, cc, a_re, a_im, d_skip, s0_re, s0_im, seq, ns, nc):
    r = rec.shape[0]
    n_cb = N_CH // nc
    tc = 128
    wide = nc * CH
    state = pl.BlockSpec((1, 1, 2, SUBLANES, S5_ST), lambda g, c: (g, c, 0, 0, 0))
    aspec = pl.BlockSpec((2, 1, SUBLANES, S5_ST), lambda g, c: (0, c, 0, 0))
    st_shape = jax.ShapeDtypeStruct((r // (ns * seq), n_cb, 2, SUBLANES, S5_ST), F32)
    return pl.pallas_call(
        functools.partial(_s5_kernel, seq=seq, tc=tc, ns=ns, nc=nc),
        grid=(r // (ns * seq), n_cb),
        in_specs=[pl.BlockSpec((ns * seq, wide), lambda g, c: (g, 2 * n_cb + c)),
                  pl.BlockSpec((2, 1, wide, 2 * S5_ST), lambda g, c: (0, c, 0, 0)),
                  pl.BlockSpec((2, 1, 2 * S5_ST, wide), lambda g, c: (0, c, 0, 0)),
                  aspec, aspec,
                  pl.BlockSpec((1, wide), lambda g, c: (0, c)),
                  state, state],
        out_specs=[pl.BlockSpec((ns * seq, wide), lambda g, c: (g, c)), state, state],
        out_shape=[jax.ShapeDtypeStruct((r, D_S5), F32), st_shape, st_shape],
        scratch_shapes=[pltpu.VMEM((seq * SUBLANES, CH), F32), pltpu.VMEM((seq * SUBLANES, CH), F32),
                        pltpu.VMEM((tc * SUBLANES, 2 * S5_ST), F32), pltpu.VMEM((tc * SUBLANES, 2 * S5_ST), F32)],
        compiler_params=_cparams(("parallel", "parallel"), 48),
        name="s5",
    )(rec, bb, cc, a_re, a_im, d_skip, s0_re, s0_im)


def _s5_params(lam_re, lam_im, log_dt, b_re, b_im, c_re, c_im, ns, nc):
    dt = jnp.exp(log_dt)[..., None]
    mag = jnp.exp(lam_re * dt)
    abar_re, abar_im = mag * jnp.cos(lam_im * dt), mag * jnp.sin(lam_im * dt)
    den = lam_re * lam_re + lam_im * lam_im
    nr, ni = abar_re - 1, abar_im
    cr = (nr * lam_re + ni * lam_im) / den
    ci = (ni * lam_re - nr * lam_im) / den
    bb_re = cr[..., None] * b_re - ci[..., None] * b_im
    bb_im = cr[..., None] * b_im + ci[..., None] * b_re
    gpc = CH // S5_GROUP
    eye = jnp.eye(gpc, dtype=F32)

    def in_map(b):
        b = b.reshape(2, N_CH, gpc, S5_STATE, S5_GROUP).transpose(0, 1, 2, 4, 3)
        return (b[:, :, :, :, None, :] * eye[None, None, :, None, :, None]).reshape(2, N_CH, CH, S5_ST)

    def out_map(c):
        c = c.reshape(2, N_CH, gpc, S5_GROUP, S5_STATE).transpose(0, 1, 2, 4, 3)
        return (c[:, :, :, :, None, :] * eye[None, None, :, None, :, None]).reshape(2, N_CH, S5_ST, CH)

    bb = jnp.concatenate([in_map(bb_re), in_map(bb_im)], axis=-1).astype(BF16)
    cc = jnp.concatenate([out_map(c_re), -out_map(c_im)], axis=-2).astype(BF16)
    n_cb = N_CH // nc
    bb = bb.reshape(2, n_cb, nc * CH, 2 * S5_ST)
    cc = cc.reshape(2, n_cb, nc, 2 * S5_ST, CH).transpose(0, 1, 3, 2, 4).reshape(2, n_cb, 2 * S5_ST, nc * CH)

    def per_slot(a):
        a = jnp.broadcast_to(a.reshape(2, n_cb, nc, 1, S5_ST), (2, n_cb, nc, ns, S5_ST))
        return a.reshape(2, n_cb, SUBLANES, S5_ST)

    return bb, cc, per_slot(abar_re), per_slot(abar_im)


def _out_proj_kernel(x_ref, oa_ref, ob_ref, y_ref, wg_ref, bg_ref, wo_ref, mod_ref, g_ref, *refs, route):
    if route:
        r_ref, _, x1_ref, h2_ref, route_ref = refs
    else:
        x1_ref, h2_ref = refs
    z = jax.nn.gelu(y_ref[...])
    gl = jnp.dot(z.astype(BF16), wg_ref[...], preferred_element_type=F32) + bg_ref[...]
    oc = z * jax.nn.sigmoid(gl)
    mix = (jnp.dot(oa_ref[...], wo_ref[:D_ATTN, :], preferred_element_type=F32)
           + jnp.dot(ob_ref[...].astype(BF16), wo_ref[D_ATTN:D_ATTN + D_LRU, :], preferred_element_type=F32)
           + jnp.dot(oc.astype(BF16), wo_ref[D_ATTN + D_LRU:, :], preferred_element_type=F32))
    x1 = x_ref[...] + mod_ref[0, 2:3, :] * mix
    x1_ref[...] = x1
    h2 = _rms_mod(x1, g_ref[...], mod_ref[0, 3:4, :], mod_ref[0, 4:5, :])
    h2_ref[...] = h2.astype(h2_ref.dtype)
    if route:
        hi = h2.astype(BF16)
        lo = (h2 - hi.astype(F32)).astype(BF16)
        pr = (jnp.dot(hi, r_ref[...], preferred_element_type=F32)
              + jnp.dot(lo, r_ref[...], preferred_element_type=F32))
        logits = pr + pltpu.roll(pr, LANES - N_EXPERTS, axis=1)
        lane = lax.broadcasted_iota(jnp.int32, logits.shape, 1)
        lg = jnp.where(lane < N_EXPERTS, logits, -jnp.inf)
        m1 = jnp.max(lg, axis=-1, keepdims=True)
        i1 = jnp.min(jnp.where(lg == m1, lane, LANES), axis=-1, keepdims=True)
        lg2 = jnp.where(lane == i1, -jnp.inf, lg)
        m2 = jnp.max(lg2, axis=-1, keepdims=True)
        i2 = jnp.min(jnp.where(lg2 == m2, lane, LANES), axis=-1, keepdims=True)
        e2 = jnp.exp(m2 - m1)
        den = 1.0 + e2
        route_ref[...] = (jnp.where(lane == 0, i1.astype(F32), 0.0) + jnp.where(lane == 1, i2.astype(F32), 0.0)
                          + jnp.where(lane == 2, 1.0 / den, 0.0) + jnp.where(lane == 3, e2 / den, 0.0))


def _out_proj(x2d, oa, ob, y, w_glu_bf, b_glu, w_out_bf, mod, mod_base, rows_per_mod, g, routing):
    r, d = x2d.shape
    tm = 512
    tiles_per_mod = rows_per_mod // tm
    route = routing is not None
    row = lambda n: pl.BlockSpec((tm, n), lambda i: (i, 0))
    full = lambda a: pl.BlockSpec(a.shape, lambda i: (0,) * a.ndim, pipeline_mode=pl.Buffered(1))
    args = [x2d, oa, ob, y, w_glu_bf, b_glu, w_out_bf, mod, g]
    in_specs = [row(d), row(D_ATTN), row(D_LRU), row(D_S5), full(w_glu_bf), full(b_glu), full(w_out_bf),
                pl.BlockSpec((1, N_MOD, d), lambda i: (mod_base + i // tiles_per_mod, 0, 0)), full(g)]
    aliases = {}
    if route:
        router_pad, shared_rows, row_base = routing
        args += [router_pad, shared_rows]
        in_specs += [full(router_pad), pl.BlockSpec(memory_space=pl.ANY)]
        aliases = {len(args) - 1: 1}
        out_shape = [jax.ShapeDtypeStruct((r, d), F32), jax.ShapeDtypeStruct(shared_rows.shape, F32),
                     jax.ShapeDtypeStruct((r, LANES), F32)]
        out_specs = [row(d), pl.BlockSpec((tm, d), lambda i: (i + row_base // tm, 0)), row(LANES)]
    else:
        out_shape = [jax.ShapeDtypeStruct((r, d), F32), jax.ShapeDtypeStruct((r, d), BF16)]
        out_specs = [row(d), row(d)]
    return pl.pallas_call(
        functools.partial(_out_proj_kernel, route=route),
        grid=(r // tm,),
        in_specs=in_specs,
        out_specs=out_specs,
        out_shape=out_shape,
        input_output_aliases=aliases,
        compiler_params=_cparams(("parallel",), 56),
        name="out_proj",
    )(*args)


def _swiglu_part(h, w1, w3, w2):
    a = jnp.dot(h, w1, preferred_element_type=F32)
    b = jnp.dot(h, w3, preferred_element_type=F32)
    return jnp.dot((jax.nn.silu(a) * b).astype(BF16), w2, preferred_element_type=F32)


def _ffn_kernel(h_ref, x_ref, mod_ref, w1_ref, w3_ref, w2_ref, o_ref):
    f = pl.program_id(1)
    part = _swiglu_part(h_ref[...], w1_ref[...], w3_ref[...], w2_ref[...])

    @pl.when(f == 0)
    def _():
        o_ref[...] = part

    @pl.when(f > 0)
    def _():
        o_ref[...] += part

    @pl.when(f == pl.num_programs(1) - 1)
    def _():
        o_ref[...] = x_ref[...] + mod_ref[0, 5:6, :] * o_ref[...]


def _ffn(h2, x1, mod, mod_base, rows_per_mod, w1, w3, w2):
    r, d = x1.shape
    tm, tf = 1024, 512
    tiles_per_mod = rows_per_mod // tm
    return pl.pallas_call(
        _ffn_kernel,
        grid=(r // tm, w1.shape[1] // tf),
        in_specs=[pl.BlockSpec((tm, d), lambda i, f: (i, 0)),
                  pl.BlockSpec((tm, d), lambda i, f: (i, 0), pipeline_mode=pl.Buffered(1)),
                  pl.BlockSpec((1, N_MOD, d), lambda i, f: (mod_base + i // tiles_per_mod, 0, 0)),
                  pl.BlockSpec((d, tf), lambda i, f: (0, f)),
                  pl.BlockSpec((d, tf), lambda i, f: (0, f)),
                  pl.BlockSpec((tf, d), lambda i, f: (f, 0))],
        out_specs=pl.BlockSpec((tm, d), lambda i, f: (i, 0)),
        out_shape=jax.ShapeDtypeStruct((r, d), F32),
        compiler_params=_cparams(("parallel", "arbitrary"), 60),
        name="ffn",
    )(h2, x1, mod, w1, w3, w2)


MOE_TM = 512
MOE_TF = 1408
MOE_TT = 256


def _route_plan(route):
    t = route.shape[0]
    n_pairs = 2 * t
    n_tiles = n_pairs // MOE_TM + N_EXPERTS
    experts = route[:, :2].astype(jnp.int32).reshape(n_pairs)
    onehot = (experts[:, None] == jnp.arange(N_EXPERTS, dtype=jnp.int32)[None]).astype(jnp.int32)
    csum = jnp.cumsum(onehot, axis=0)
    rank = jnp.sum(onehot * csum, axis=1) - 1
    counts = csum[-1]
    padded = (counts + MOE_TM - 1) // MOE_TM * MOE_TM
    ends = jnp.cumsum(padded)
    pos = jnp.sum(onehot * (ends - padded)[None], axis=1) + rank
    src = jnp.zeros((n_tiles * MOE_TM,), jnp.int32).at[pos].set(jnp.arange(n_pairs, dtype=jnp.int32) // 2)
    tile_start = jnp.arange(n_tiles, dtype=jnp.int32) * MOE_TM
    tile_valid = (tile_start < ends[-1]).astype(jnp.int32)
    tile_expert = jnp.sum((tile_start[:, None] >= ends[None]).astype(jnp.int32), axis=1)
    last_expert = jnp.max(jnp.where(counts > 0, jnp.arange(N_EXPERTS, dtype=jnp.int32), 0))
    tile_expert = jnp.where(tile_valid == 1, tile_expert, last_expert)
    return pos, src, tile_expert, tile_valid


def _gather_rows(idx_of_row, n_rows, src_hbm, dst, sem):
    def row(r, carry):
        pltpu.make_async_copy(src_hbm.at[pl.ds(idx_of_row(r), 1)], dst.at[pl.ds(r, 1)], sem).start()
        return carry
    lax.fori_loop(0, n_rows, row, 0, unroll=8)


def _wait_rows(n_rows, src_hbm, dst, sem):
    pltpu.make_async_copy(src_hbm.at[pl.ds(0, n_rows)], dst, sem).wait()


def _moe_experts_kernel(src_ref, texp_ref, valid_ref, h_hbm, w1_ref, w3_ref, w2_ref, o_ref, xbuf, xbf, sem,
                        *, rows_per_step):
    i = pl.program_id(0)
    f = pl.program_id(1)
    n_i = pl.num_programs(0)

    @pl.when(f == 0)
    def _():
        @pl.when(i == 0)
        def _():
            _gather_rows(lambda r: src_ref[r], MOE_TM, h_hbm, xbuf, sem.at[0])

        _wait_rows(MOE_TM, h_hbm, xbuf, sem.at[0])
        xbf[...] = xbuf[...].astype(BF16)

    nxt = jnp.where(i + 1 < n_i, i + 1, 0)

    def start_next_rows():
        for r in range(rows_per_step):
            row = f * rows_per_step + r
            pltpu.make_async_copy(h_hbm.at[pl.ds(src_ref[nxt * MOE_TM + row], 1)], xbuf.at[pl.ds(row, 1)],
                                  sem.at[0]).start()

    valid = valid_ref[i] == 1

    @pl.when(valid)
    def _():
        part = _swiglu_part(xbf[...], w1_ref[0], w3_ref[0], w2_ref[0])
        start_next_rows()

        @pl.when(f == 0)
        def _():
            o_ref[...] = part

        @pl.when(f > 0)
        def _():
            o_ref[...] += part

    @pl.when(jnp.logical_not(valid))
    def _():
        start_next_rows()

        @pl.when(f == 0)
        def _():
            o_ref[...] = jnp.zeros_like(o_ref)

    @pl.when((i == n_i - 1) & (f == pl.num_programs(1) - 1))
    def _():
        _wait_rows(MOE_TM, h_hbm, xbuf, sem.at[0])


def _moe_experts(h2, src, tile_expert, tile_valid, w1, w3, w2):
    t, d = h2.shape
    n_tiles = tile_expert.shape[0]
    n_f = w1.shape[2] // MOE_TF

    def wmap(i, f, src, texp, valid):
        return (texp[i], 0, jnp.where(valid[i] == 1, f, n_f - 1))

    def w2map(i, f, src, texp, valid):
        return (texp[i], jnp.where(valid[i] == 1, f, n_f - 1), 0)

    grid_spec = pltpu.PrefetchScalarGridSpec(
        num_scalar_prefetch=3,
        grid=(n_tiles, n_f),
        in_specs=[pl.BlockSpec(memory_space=pl.ANY),
                  pl.BlockSpec((1, d, MOE_TF), wmap),
                  pl.BlockSpec((1, d, MOE_TF), wmap),
                  pl.BlockSpec((1, MOE_TF, d), w2map)],
        out_specs=pl.BlockSpec((MOE_TM, d), lambda i, f, *_: (i, 0)),
        scratch_shapes=[pltpu.VMEM((MOE_TM, d), F32), pltpu.VMEM((MOE_TM, d), BF16),
                        pltpu.SemaphoreType.DMA((1,))])
    return pl.pallas_call(
        functools.partial(_moe_experts_kernel, rows_per_step=MOE_TM // n_f),
        grid_spec=grid_spec,
        out_shape=jax.ShapeDtypeStruct((n_tiles * MOE_TM, d), F32),
        compiler_params=_cparams(("arbitrary", "arbitrary"), 62),
        name="moe_experts",
    )(src, tile_expert, tile_valid, h2, w1, w3, w2)


def _moe_combine_kernel(pos_ref, x_ref, mod_ref, route_ref, y_hbm, o_ref, buf, sem, *, tok_base):
    i = pl.program_id(0)
    slot = i % 2

    def gather(tile, slot):
        for k in range(2):
            _gather_rows(lambda r, k=k: pos_ref[2 * (tok_base + tile * MOE_TT + r) + k], MOE_TT, y_hbm,
                         buf.at[slot, k], sem.at[slot])

    @pl.when(i == 0)
    def _():
        gather(0, 0)

    for k in range(2):
        _wait_rows(MOE_TT, y_hbm, buf.at[slot, k], sem.at[slot])

    @pl.when(i + 1 < pl.num_programs(0))
    def _():
        gather(i + 1, 1 - slot)

    mixed = route_ref[:, 2:3] * buf[slot, 0] + route_ref[:, 3:4] * buf[slot, 1]
    o_ref[...] = x_ref[...] + mod_ref[0, 5:6, :] * mixed


def _moe_combine(pos, x1, mod, mod_base, rows_per_mod, route, y_rows, tok_base):
    r, d = x1.shape
    tiles_per_mod = rows_per_mod // MOE_TT
    grid_spec = pltpu.PrefetchScalarGridSpec(
        num_scalar_prefetch=1,
        grid=(r // MOE_TT,),
        in_specs=[pl.BlockSpec((MOE_TT, d), lambda i, *_: (i, 0)),
                  pl.BlockSpec((1, N_MOD, d), lambda i, *_: (mod_base + i // tiles_per_mod, 0, 0)),
                  pl.BlockSpec((MOE_TT, LANES), lambda i, *_: (i, 0)),
                  pl.BlockSpec(memory_space=pl.ANY)],
        out_specs=pl.BlockSpec((MOE_TT, d), lambda i, *_: (i, 0)),
        scratch_shapes=[pltpu.VMEM((2, 2, MOE_TT, d), F32), pltpu.SemaphoreType.DMA((2,))])
    return pl.pallas_call(
        functools.partial(_moe_combine_kernel, tok_base=tok_base),
        grid_spec=grid_spec,
        out_shape=jax.ShapeDtypeStruct((r, d), F32),
        compiler_params=_cparams(("arbitrary",), 32),
        name="moe_combine",
    )(pos, x1, mod, route, y_rows)


def _slot_states(s, ns, nc, width):
    ng = s.shape[0] // ns
    s = s.reshape(ng, ns, 2, N_CH // nc, nc, width).transpose(0, 3, 2, 4, 1, 5)
    return s.reshape(ng, N_CH // nc, 2, SUBLANES, width)


def _unslot_states(s, ns, nc, width):
    ng = s.shape[0]
    s = s.reshape(ng, N_CH // nc, 2, nc, ns, width).transpose(0, 4, 2, 1, 3, 5)
    return s.reshape(ng * ns, 2, N_CH * width)


def _token_mixer(x2d, batch, seq, mod, mod_base, rows_per_mod, p, kv_ctx, lru_h0, s5_h0, prev_kv=None,
                 routing=None):
    is_ctx = kv_ctx is None
    ns = min(batch, SUBLANES)
    nc = SUBLANES // ns
    outs = _in_proj(x2d, p['g_mix'], mod, mod_base, rows_per_mod, p['w_in'], p['q_g'], p['k_g'], p['ones_bd'],
                    seq, is_ctx, prev_kv)
    q, k, v, rec = outs[:4]
    if is_ctx:
        oa = _ctx_attn(q, k, v, seq)
        lru_h0 = jnp.zeros((batch, 2, D_LRU), F32)
        s5_h0 = (jnp.zeros((batch, 2, N_S5_GROUPS * S5_STATE), F32),) * 2
    else:
        oa = _nbr_attn(q, k, v, kv_ctx[0], kv_ctx[1], kv_ctx[2], p['tb'], seq)
    lru_w, lru_b = _rglru_params(p['lru_wa'], p['lru_ba'], p['lru_wi'], p['lru_bi'], nc)
    ob, lru_fin = _rglru(rec, p['conv_w'], p['conv_b'], lru_w, lru_b, p['lru_lam'],
                         _slot_states(lru_h0, ns, nc, CH), seq, ns, nc)
    s5_bb, s5_cc, s5_are, s5_aim = _s5_params(*p['s5'], ns, nc)
    y, fin_re, fin_im = _s5(rec, s5_bb, s5_cc, s5_are, s5_aim, p['d_skip'],
                            _slot_states(s5_h0[0], ns, nc, S5_ST), _slot_states(s5_h0[1], ns, nc, S5_ST),
                            seq, ns, nc)
    res = _out_proj(x2d, oa, ob, y, p['w_glu'], p['b_glu'], p['w_out'], mod, mod_base, rows_per_mod,
                    p['g_ffn'], routing)
    if not is_ctx:
        return res, None
    state = (outs[4], outs[5], _unslot_states(lru_fin, ns, nc, CH),
             _unslot_states(fin_re, ns, nc, S5_ST).reshape(batch, 2, N_S5_GROUPS, S5_STATE),
             _unslot_states(fin_im, ns, nc, S5_ST).reshape(batch, 2, N_S5_GROUPS, S5_STATE))
    return res, state


def kernel(x_prompt, x_sample, c, cache_k, cache_v, state_lru, state_s5_re, state_s5_im, c_ctx, norm_mix_g, norm_ffn_g, w_mod, b_mod, w_in, w_out, q_norm_g, k_norm_g, rpb, lru_conv_w, lru_conv_b, lru_wa, lru_ba, lru_wi, lru_bi, lru_lam, s5_lam_re, s5_lam_im, s5_log_dt, s5_b_re, s5_b_im, s5_c_re, s5_c_im, s5_d, s5_w_glu, s5_b_glu, ffn_w1, ffn_w3, ffn_w2, moe_router, moe_w1, moe_w3, moe_w2):
    batch, seq, d = x_prompt.shape
    dec_batch, dec_seq, _ = x_sample.shape
    depth = w_in.shape[0]
    assert dec_batch + 1 <= SUBLANES and batch % SUBLANES == 0 and dec_batch in (4, 8)

    cvecs = jnp.concatenate([c_ctx[None], c, jnp.zeros((SUBLANES - 1 - dec_batch, d), F32)], axis=0)
    mods = _adaln(cvecs, w_mod, b_mod).reshape(depth, SUBLANES, N_MOD, d)

    heads_per_tile = 512 // HEAD_DIM
    ones_bd = jnp.asarray(np.kron(np.eye(heads_per_tile // 2), np.ones((HEAD_DIM, HEAD_DIM))), BF16)

    xp = x_prompt.reshape(batch * seq, d)
    xs = x_sample.reshape(dec_batch * dec_seq, d)
    n_tok = (batch * seq, dec_batch * dec_seq)
    kv, lrus, s5rs, s5is = None, [], [], []
    for l in range(depth):
        p = {
            'g_mix': norm_mix_g[l][None], 'g_ffn': norm_ffn_g[l][None],
            'w_in': w_in[l].astype(BF16), 'w_out': w_out[l].astype(BF16),
            'q_g': jnp.tile(q_norm_g[l], heads_per_tile)[None], 'k_g': jnp.tile(k_norm_g[l], heads_per_tile)[None],
            'ones_bd': ones_bd, 'tb': _bias_table(rpb[l]),
            'conv_w': lru_conv_w[l], 'conv_b': lru_conv_b[l][None],
            'lru_wa': lru_wa[l], 'lru_ba': lru_ba[l], 'lru_wi': lru_wi[l], 'lru_bi': lru_bi[l], 'lru_lam': lru_lam[l],
            's5': (s5_lam_re[l], s5_lam_im[l], s5_log_dt[l], s5_b_re[l], s5_b_im[l], s5_c_re[l], s5_c_im[l]),
            'd_skip': s5_d[l][None], 'w_glu': s5_w_glu[l].astype(BF16), 'b_glu': s5_b_glu[l][None],
        }
        j = l // 2
        dense = l % 2 == 0
        route_p = route_s = None
        if not dense:
            r_hi = moe_router[j].astype(BF16)
            r_lo = (moe_router[j] - r_hi.astype(F32)).astype(BF16)
            router = jnp.pad(jnp.concatenate([r_hi, r_lo], axis=1), ((0, 0), (0, LANES - 2 * N_EXPERTS)))
            route_p = (router, jnp.zeros((sum(n_tok), d), F32), 0)
        res_p, (k_c, v_c, lru_c, s5r_c, s5i_c) = _token_mixer(
            xp, batch, seq, mods[l], 0, batch * seq, p, None, None, None, prev_kv=kv, routing=route_p)
        kv = (k_c, v_c)
        lrus.append(lru_c)
        s5rs.append(s5r_c)
        s5is.append(s5i_c)
        if not dense:
            route_s = (router, res_p[1], n_tok[0])
        res_s, _ = _token_mixer(
            xs, dec_batch, dec_seq, mods[l], 1, dec_seq, p, (cache_k, cache_v, l), state_lru[:, l],
            (state_s5_re[:, l].reshape(dec_batch, 2, -1), state_s5_im[:, l].reshape(dec_batch, 2, -1)),
            routing=route_s)
        if dense:
            w = (ffn_w1[j].astype(BF16), ffn_w3[j].astype(BF16), ffn_w2[j].astype(BF16))
            xp = _ffn(res_p[1], res_p[0], mods[l], 0, batch * seq, *w)
            xs = _ffn(res_s[1], res_s[0], mods[l], 1, dec_seq, *w)
        else:
            h2 = res_s[1]
            pos, src, tile_expert, tile_valid = _route_plan(jnp.concatenate([res_p[2], res_s[2]], axis=0))
            y_rows = _moe_experts(h2, src, tile_expert, tile_valid,
                                  moe_w1[j].astype(BF16), moe_w3[j].astype(BF16), moe_w2[j].astype(BF16))
            xp = _moe_combine(pos, res_p[0], mods[l], 0, batch * seq, res_p[2], y_rows, 0)
            xs = _moe_combine(pos, res_s[0], mods[l], 1, dec_seq, res_s[2], y_rows, batch * seq)
    return (xp.reshape(batch, seq, d), xs.reshape(dec_batch, dec_seq, d), kv[0], kv[1],
            jnp.stack(lrus, axis=1), jnp.stack(s5rs, axis=1), jnp.stack(s5is, axis=1))
```

```python
import functools

import numpy as np
import jax
import jax.numpy as jnp
from jax import lax
from jax.experimental import pallas as pl
from jax.experimental.pallas import tpu as pltpu

F32 = jnp.float32
BF16 = jnp.bfloat16

D_MODEL = 2048
N_HEADS = 16
HEAD_DIM = 64
D_ATTN = N_HEADS * HEAD_DIM
GRID_W = 64
WIN_R = 8
WIN_C = 16
D_LRU = 512
LRU_BLOCK = 64
CONV_W = 4
LRU_C = 8.0
D_S5 = 512
S5_GROUP = 16
N_S5_GROUPS = 32
S5_STATE = 64
D_REC = 2 * D_LRU + D_S5
D_IN = 3 * D_ATTN + D_REC
N_MOD = 6
N_EXPERTS = 8
EPS = 1e-6
NEG = -1e30

LANES = 128
SUBLANES = 8
MIB = 1024 * 1024

CH = 128
N_CH = D_LRU // CH
S5_ST = (CH // S5_GROUP) * S5_STATE


def _cparams(sem, vmem_mib):
    return pltpu.CompilerParams(dimension_semantics=sem, vmem_limit_bytes=vmem_mib * MIB)


def _adaln_kernel(c_ref, w_ref, b_ref, o_ref):
    s = jax.nn.silu(c_ref[...]).astype(BF16)
    o_ref[0] = jnp.dot(s, w_ref[0].astype(BF16), preferred_element_type=F32) + b_ref[0]


def _adaln(cvecs, w_mod, b_mod):
    depth, d, n = w_mod.shape
    tn = 1024
    return pl.pallas_call(
        _adaln_kernel,
        grid=(depth, n // tn),
        in_specs=[pl.BlockSpec((SUBLANES, d), lambda l, j: (0, 0)),
                  pl.BlockSpec((1, d, tn), lambda l, j: (l, 0, j)),
                  pl.BlockSpec((1, 1, tn), lambda l, j: (l, 0, j))],
        out_specs=pl.BlockSpec((1, SUBLANES, tn), lambda l, j: (l, 0, j)),
        out_shape=jax.ShapeDtypeStruct((depth, SUBLANES, n), F32),
        compiler_params=_cparams(("parallel", "parallel"), 40),
        name="adaln",
    )(cvecs, w_mod, b_mod.reshape(depth, 1, n))


def _rms_mod(x, g, shift, scale):
    xf = x * lax.rsqrt(jnp.mean(x * x, axis=-1, keepdims=True) + EPS)
    return (xf * g) * (1 + scale) + shift


def _in_proj_kernel(x_ref, g_ref, mod_ref, w_ref, qg_ref, kg_ref, ones_ref, *refs, seq, write_cache, n_prev):
    kp_ref = vp_ref = None
    if write_cache and n_prev:
        kp_ref, vp_ref, *refs = refs
    if write_cache:
        q_ref, k_ref, v_ref, rec_ref, kc_ref, vc_ref, h_scr = refs
    else:
        q_ref, k_ref, v_ref, rec_ref, h_scr = refs
    j = pl.program_id(1)
    tm, tn = q_ref.shape
    heads = tn // HEAD_DIM

    @pl.when(j == 0)
    def _():
        h_scr[...] = _rms_mod(x_ref[...], g_ref[...], mod_ref[0, 0:1, :], mod_ref[0, 1:2, :]).astype(BF16)

    y = jnp.dot(h_scr[...], w_ref[...], preferred_element_type=F32)

    def to_cache(c_ref, prev_ref, val):
        if prev_ref is not None:
            c_ref[:, :n_prev] = prev_ref[...]
        for b in range(tm // seq):
            for h in range(heads):
                c_ref[b, n_prev, h, :, :] = val[b * seq:(b + 1) * seq, h * HEAD_DIM:(h + 1) * HEAD_DIM]

    @pl.when(j < 4)
    def _():
        y2 = y * y
        hi = y2.astype(BF16)
        lo = (y2 - hi.astype(F32)).astype(BF16)
        half = ones_ref.shape[0]
        ss = jnp.concatenate(
            [jnp.dot(hi[:, c:c + half], ones_ref[...], preferred_element_type=F32)
             + jnp.dot(lo[:, c:c + half], ones_ref[...], preferred_element_type=F32) for c in range(0, tn, half)],
            axis=1)
        gain = jnp.where(j < 2, qg_ref[...], kg_ref[...])
        yn = (y * lax.rsqrt(ss * (1.0 / HEAD_DIM) + EPS)) * gain

        @pl.when(j < 2)
        def _():
            q_ref[...] = yn.astype(BF16)

        @pl.when(j >= 2)
        def _():
            k_ref[...] = yn.astype(BF16)
            if write_cache:
                to_cache(kc_ref, kp_ref, yn)

    @pl.when((j >= 4) & (j < 6))
    def _():
        v_ref[...] = y.astype(BF16)
        if write_cache:
            to_cache(vc_ref, vp_ref, y)

    @pl.when(j >= 6)
    def _():
        rec_ref[...] = y


def _in_proj(x2d, g, mod, mod_base, rows_per_mod, w_bf, qg, kg, ones_bd, seq, write_cache, prev_kv):
    r, d = x2d.shape
    tm, tn = 512, 512
    n_i = r // tm
    tiles_per_mod = rows_per_mod // tm
    bpt = tm // seq
    hpt = tn // HEAD_DIM
    n_prev = prev_kv[0].shape[1] if prev_kv else 0

    def col(lo, n):
        return lambda i, j: (i, jnp.clip(j - lo, 0, n - 1))

    def cache_spec(lo, n_layers):
        return pl.BlockSpec((bpt, n_layers, hpt, seq, HEAD_DIM), lambda i, j: (i, 0, jnp.clip(j - lo, 0, 1), 0, 0))

    args = [x2d, g, mod, w_bf, qg, kg, ones_bd]
    in_specs = [pl.BlockSpec((tm, d), lambda i, j: (i, 0)),
                pl.BlockSpec((1, d), lambda i, j: (0, 0)),
                pl.BlockSpec((1, N_MOD, d), lambda i, j: (mod_base + i // tiles_per_mod, 0, 0)),
                pl.BlockSpec((d, tn), lambda i, j: (0, j)),
                pl.BlockSpec((1, tn), lambda i, j: (0, 0)),
                pl.BlockSpec((1, tn), lambda i, j: (0, 0)),
                pl.BlockSpec(ones_bd.shape, lambda i, j: (0, 0))]
    out_shape = [jax.ShapeDtypeStruct((r, D_ATTN), BF16)] * 3 + [jax.ShapeDtypeStruct((r, D_REC), F32)]
    out_specs = [pl.BlockSpec((tm, tn), col(0, 2)), pl.BlockSpec((tm, tn), col(2, 2)),
                 pl.BlockSpec((tm, tn), col(4, 2)), pl.BlockSpec((tm, tn), col(6, 3))]
    if write_cache:
        if n_prev:
            args += list(prev_kv)
            in_specs += [cache_spec(2, n_prev), cache_spec(4, n_prev)]
        cshape = jax.ShapeDtypeStruct((r // seq, n_prev + 1, N_HEADS, seq, HEAD_DIM), F32)
        out_shape += [cshape, cshape]
        out_specs += [cache_spec(2, n_prev + 1), cache_spec(4, n_prev + 1)]
    return pl.pallas_call(
        functools.partial(_in_proj_kernel, seq=seq, write_cache=write_cache, n_prev=n_prev),
        grid=(n_i, D_IN // tn),
        in_specs=in_specs,
        out_specs=out_specs,
        out_shape=out_shape,
        scratch_shapes=[pltpu.VMEM((tm, d), BF16)],
        compiler_params=_cparams(("parallel", "arbitrary"), 56),
        name="in_proj",
    )(*args)


def _qk(q, k):
    return lax.dot_general(q, k, (((1,), (1,)), ((), ())), preferred_element_type=F32)


def _head_lanes(n_rows, first):
    lane = lax.broadcasted_iota(jnp.int32, (n_rows, LANES), 1)
    return (lane < HEAD_DIM) if first else (lane >= HEAD_DIM)


def _own(x, first):
    return jnp.where(_head_lanes(x.shape[0], first), x, jnp.zeros_like(x))


def _values_and_ones(v, first):
    ones = jnp.where(_head_lanes(v.shape[0], first), 1.0, 0.0).astype(BF16)
    return jnp.concatenate([_own(v, first), ones], axis=1)


def _exp_rows(parts):
    m = functools.reduce(jnp.maximum, [jnp.max(s, axis=-1, keepdims=True) for s in parts])
    return [jnp.exp(s - m).astype(BF16) for s in parts]


def _ctx_attn_kernel(q_ref, k_ref, v_ref, o_ref):
    scale = HEAD_DIM ** -0.5
    for hp in range(N_HEADS * HEAD_DIM // LANES):
        sl = slice(hp * LANES, (hp + 1) * LANES)
        q, k, v = q_ref[:, sl], k_ref[:, sl], v_ref[:, sl]
        nd = None
        for first in (True, False):
            (e,) = _exp_rows([_qk(_own(q, first), k) * scale])
            part = jnp.dot(e, _values_and_ones(v, first), preferred_element_type=F32)
            nd = part if nd is None else nd + part
        o_ref[:, sl] = (nd[:, :LANES] / nd[:, LANES:]).astype(BF16)


def _ctx_attn(q, k, v, seq):
    r = q.shape[0]
    spec = pl.BlockSpec((seq, D_ATTN), lambda b: (b, 0))
    return pl.pallas_call(
        _ctx_attn_kernel,
        grid=(r // seq,),
        in_specs=[spec, spec, spec],
        out_specs=spec,
        out_shape=jax.ShapeDtypeStruct((r, D_ATTN), BF16),
        compiler_params=_cparams(("parallel",), 32),
        name="ctx_attn",
    )(q, k, v)


def _nbr_plan(seq):
    rows = seq // GRID_W
    wr = min(WIN_R, rows)
    row_start = np.clip(np.arange(rows) - wr // 2, 0, rows - wr)
    rows_per_blk = 256 // GRID_W
    ranges = []
    for qb in range(rows // rows_per_blk):
        rs = row_start[qb * rows_per_blk:(qb + 1) * rows_per_blk]
        lo = int(rs.min()) * GRID_W // LANES * LANES
        hi = -(-(int(rs.max()) + wr) * GRID_W // LANES) * LANES
        ranges.append((lo, hi))
    return rows, wr, row_start, ranges


def _nbr_attn_kernel(q_ref, k_ref, v_ref, kc_ref, vc_ref, tb_ref, o_ref, bias_scr, *, seq):
    rows, wr, row_start, ranges = _nbr_plan(seq)
    scale = HEAD_DIM ** -0.5
    heads = q_ref.shape[1] // HEAD_DIM
    neg = jnp.full((GRID_W, GRID_W), NEG, F32)

    @pl.when(pl.program_id(1) == 0)
    def _():
        for h in range(heads):
            for qr in range(rows):
                for kp in range(rows // 2):
                    blks = []
                    for kr in (2 * kp, 2 * kp + 1):
                        inside = row_start[qr] <= kr < row_start[qr] + wr
                        blks.append(tb_ref[h, kr - qr + WIN_R - 1] if inside else neg)
                    bias_scr[h, qr * GRID_W:(qr + 1) * GRID_W, kp * LANES:(kp + 1) * LANES] = (
                        jnp.concatenate(blks, axis=1))

    kc = jnp.concatenate([kc_ref[0, 0, h] for h in range(heads)], axis=1).astype(BF16)
    vc = jnp.concatenate([vc_ref[0, 0, h] for h in range(heads)], axis=1).astype(BF16)
    for qb, (lo, hi) in enumerate(ranges):
        qs = slice(qb * 256, (qb + 1) * 256)
        q = q_ref[qs, :]
        nd = None
        for h, first in enumerate((True, False)):
            qh = _own(q, first)
            s_loc = _qk(qh, k_ref[lo:hi, :]) * scale + bias_scr[h, qs, lo:hi]
            s_ctx = _qk(qh, kc) * scale
            e_loc, e_ctx = _exp_rows([s_loc, s_ctx])
            part = (jnp.dot(e_loc, _values_and_ones(v_ref[lo:hi, :], first), preferred_element_type=F32)
                    + jnp.dot(e_ctx, _values_and_ones(vc, first), preferred_element_type=F32))
            nd = part if nd is None else nd + part
        o_ref[qs, :] = (nd[:, :LANES] / nd[:, LANES:]).astype(BF16)


def _nbr_attn(q, k, v, cache_k, cache_v, layer, tb, seq):
    r = q.shape[0]
    past = cache_k.shape[3]
    hp = LANES // HEAD_DIM
    spec = pl.BlockSpec((seq, LANES), lambda g, b: (b, g))
    cspec = pl.BlockSpec((1, 1, hp, past, HEAD_DIM), lambda g, b: (b, layer, g, 0, 0))
    return pl.pallas_call(
        functools.partial(_nbr_attn_kernel, seq=seq),
        grid=(N_HEADS // hp, r // seq),
        in_specs=[spec, spec, spec, cspec, cspec,
                  pl.BlockSpec((hp,) + tb.shape[1:], lambda g, b: (g, 0, 0, 0))],
        out_specs=spec,
        out_shape=jax.ShapeDtypeStruct((r, D_ATTN), BF16),
        scratch_shapes=[pltpu.VMEM((hp, seq, seq), F32)],
        compiler_params=_cparams(("parallel", "arbitrary"), 48),
        name="nbr_attn",
    )(q, k, v, cache_k, cache_v, tb)


def _bias_table(rpb_l):
    col = np.arange(GRID_W)
    col_start = np.clip(col - WIN_C // 2, 0, GRID_W - WIN_C)
    col_in = (col[None, :] >= col_start[:, None]) & (col[None, :] < col_start[:, None] + WIN_C)
    dc_idx = np.clip(col[None, :] - col[:, None] + WIN_C - 1, 0, 2 * WIN_C - 2)
    onehot = (dc_idx.reshape(1, -1) == np.arange(2 * WIN_C - 1)[:, None]).astype(np.float32)
    tb = jnp.einsum('hrc,cq->hrq', rpb_l, onehot, precision=lax.Precision.HIGHEST)
    tb = tb.reshape(rpb_l.shape[0], rpb_l.shape[1], GRID_W, GRID_W)
    return jnp.where(col_in[None, None], tb, NEG).astype(F32)


def _expm1(x):
    u = jnp.exp(x)
    um1 = u - 1.0
    near = um1 * x / jnp.where(u == 1.0, 1.0, jnp.log(u))
    return jnp.where(x < -0.5, um1, jnp.where(u == 1.0, x, near))


def _slot0(n_rows, n_cols, ns):
    return (lax.broadcasted_iota(jnp.int32, (n_rows, n_cols), 0) % SUBLANES) < ns


def _per_slot(vec, n_rows, ns, nc):
    if nc == 1:
        return vec
    w = vec.shape[1] // 2
    return jnp.where(_slot0(n_rows, w, ns), vec[:, :w], vec[:, w:])


def _stack_slots(x, ns, nc):
    if nc == 1:
        return x
    m = _slot0(x.shape[0], x.shape[1], ns)
    zero = jnp.zeros_like(x)
    return jnp.concatenate([jnp.where(m, x, zero), jnp.where(m, zero, x)], axis=1)


def _pick_slot(y, ns, nc):
    if nc == 1:
        return y
    w = y.shape[1] // 2
    return jnp.where(_slot0(y.shape[0], w, ns), y[:, :w], y[:, w:])


def _to_time_major(x_ref, t_scr, row0, seq, ns, nc):
    for k in range(nc):
        for s in range(ns):
            t_scr[pl.ds(row0 + k * ns + s, seq, stride=SUBLANES), :] = (
                x_ref[s * seq:(s + 1) * seq, k * CH:(k + 1) * CH])


def _rows(start, n):
    return pl.ds(pl.multiple_of(start, SUBLANES), n)


def _rglru_kernel(x_ref, g_ref, cw_ref, cb_ref, w_ref, b_ref, lam_ref, h0_ref, o_ref, fin_ref,
                  xt_scr, yt_scr, af_scr, bf_scr, ab_scr, bb_scr, *, seq, tc, ns, nc):
    rows = tc * SUBLANES
    n_chunks = seq // tc
    front = (CONV_W // 2) * SUBLANES
    back = (CONV_W - 1 - CONV_W // 2) * SUBLANES

    xt_scr[0:front, :] = jnp.zeros((front, CH), F32)
    xt_scr[front + seq * SUBLANES:front + seq * SUBLANES + back, :] = jnp.zeros((back, CH), F32)
    _to_time_major(x_ref, xt_scr, front, seq, ns, nc)
    yt_scr[...] = jnp.zeros_like(yt_scr)

    def coefficients(r0, d, a_scr, b_scr):
        xc = _per_slot(cb_ref[...], rows, ns, nc)
        for j in range(CONV_W):
            xc = xc + xt_scr[_rows(r0 + j * SUBLANES, rows), :] * _per_slot(cw_ref[j:j + 1, :], rows, ns, nc)
        gates = (jnp.dot(_stack_slots(xc, ns, nc).astype(BF16), w_ref[d, 0], preferred_element_type=F32)
                 + _per_slot(b_ref[d, 0], rows, ns, nc))
        rg = jax.nn.sigmoid(gates[:, :CH])
        ig = jax.nn.sigmoid(gates[:, CH:])
        log_a = -LRU_C * rg * _per_slot(jax.nn.softplus(-lam_ref[d:d + 1, :]), rows, ns, nc)
        a_scr[...] = jnp.exp(log_a)
        b_scr[...] = jnp.sqrt(-_expm1(2 * log_a)) * (ig * xc)

    def chunk(c, carry):
        rf = pl.multiple_of(c * rows, rows)
        rb = pl.multiple_of((n_chunks - 1 - c) * rows, rows)
        coefficients(rf, 0, af_scr, bf_scr)
        coefficients(rb, 1, ab_scr, bb_scr)

        def step(t, carry):
            h_f, h_b = carry
            tf = _rows(t * SUBLANES, SUBLANES)
            tb = _rows((tc - 1 - t) * SUBLANES, SUBLANES)
            h_f = af_scr[tf, :] * h_f + bf_scr[tf, :]
            h_b = ab_scr[tb, :] * h_b + bb_scr[tb, :]
            bf_scr[tf, :] = h_f
            bb_scr[tb, :] = h_b
            return h_f, h_b

        carry = lax.fori_loop(0, tc, step, carry, unroll=8)
        yt_scr[pl.ds(rf, rows), :] += bf_scr[...]
        yt_scr[pl.ds(rb, rows), :] += bb_scr[...]
        return carry

    h_f, h_b = lax.fori_loop(0, n_chunks, chunk, (h0_ref[0, 0, 0], h0_ref[0, 0, 1]))
    fin_ref[0, 0, 0] = h_f
    fin_ref[0, 0, 1] = h_b
    for k in range(nc):
        for s in range(ns):
            blk = (slice(s * seq, (s + 1) * seq), slice(k * CH, (k + 1) * CH))
            o_ref[blk] = yt_scr[pl.ds(k * ns + s, seq, stride=SUBLANES), :] * jax.nn.gelu(g_ref[blk])


def _rglru(rec, conv_w, conv_b, w_g, b_g, lam, h0, seq, ns, nc):
    r = rec.shape[0]
    n_cb = N_CH // nc
    tc = 128
    wide = nc * CH
    n_pad = (seq + CONV_W - 1) * SUBLANES
    state = pl.BlockSpec((1, 1, 2, SUBLANES, CH), lambda g, c: (g, c, 0, 0, 0))
    return pl.pallas_call(
        functools.partial(_rglru_kernel, seq=seq, tc=tc, ns=ns, nc=nc),
        grid=(r // (ns * seq), n_cb),
        in_specs=[pl.BlockSpec((ns * seq, wide), lambda g, c: (g, c)),
                  pl.BlockSpec((ns * seq, wide), lambda g, c: (g, n_cb + c)),
                  pl.BlockSpec((CONV_W, wide), lambda g, c: (0, c)),
                  pl.BlockSpec((1, wide), lambda g, c: (0, c)),
                  pl.BlockSpec((2, 1, wide, 2 * CH), lambda g, c: (0, c, 0, 0)),
                  pl.BlockSpec((2, 1, 1, 2 * wide), lambda g, c: (0, c, 0, 0)),
                  pl.BlockSpec((2, wide), lambda g, c: (0, c)),
                  state],
        out_specs=[pl.BlockSpec((ns * seq, wide), lambda g, c: (g, c)), state],
        out_shape=[jax.ShapeDtypeStruct((r, D_LRU), F32),
                   jax.ShapeDtypeStruct((r // (ns * seq), n_cb, 2, SUBLANES, CH), F32)],
        scratch_shapes=[pltpu.VMEM((n_pad, CH), F32), pltpu.VMEM((seq * SUBLANES, CH), F32)]
        + [pltpu.VMEM((tc * SUBLANES, CH), F32)] * 4,
        compiler_params=_cparams(("parallel", "parallel"), 48),
        name="rglru",
    )(rec, rec, conv_w, conv_b, w_g, b_g, lam, h0)


def _rglru_params(wa, ba, wi, bi, nc):
    bpc = CH // LRU_BLOCK
    n_cb = N_CH // nc
    eye = jnp.eye(bpc, dtype=F32)

    def dense(w):
        w = w.reshape(2, N_CH, bpc, LRU_BLOCK, LRU_BLOCK)
        return (w[:, :, :, :, None, :] * eye[None, None, :, None, :, None]).reshape(2, N_CH, CH, CH)

    w_g = jnp.concatenate([dense(wa), dense(wi)], axis=-1).reshape(2, n_cb, nc * CH, 2 * CH).astype(BF16)
    b_g = jnp.concatenate([ba.reshape(2, N_CH, CH), bi.reshape(2, N_CH, CH)], axis=-1)
    return w_g, b_g.reshape(2, n_cb, 1, nc * 2 * CH)


def _s5_kernel(u_ref, bb_ref, cc_ref, ar_ref, ai_ref, d_ref, s0r_ref, s0i_ref, y_ref, fr_ref, fi_ref,
               ut_scr, yt_scr, sf_scr, sb_scr, *, seq, tc, ns, nc):
    rows = tc * SUBLANES
    n_chunks = seq // tc
    _to_time_major(u_ref, ut_scr, 0, seq, ns, nc)
    yt_scr[...] = jnp.zeros_like(yt_scr)

    def chunk(c, carry):
        rf = pl.multiple_of(c * rows, rows)
        rb = pl.multiple_of((n_chunks - 1 - c) * rows, rows)
        for d, r0, st_scr in ((0, rf, sf_scr), (1, rb, sb_scr)):
            u = _stack_slots(ut_scr[pl.ds(r0, rows), :], ns, nc)
            st_scr[...] = jnp.dot(u.astype(BF16), bb_ref[d, 0], preferred_element_type=F32)

        def update(st_scr, rr, d, s_re, s_im):
            a_re = ar_ref[d, 0]
            a_im = ai_ref[d, 0]
            n_re = a_re * s_re - a_im * s_im + st_scr[rr, :S5_ST]
            n_im = a_re * s_im + a_im * s_re + st_scr[rr, S5_ST:]
            st_scr[rr, :S5_ST] = n_re
            st_scr[rr, S5_ST:] = n_im
            return n_re, n_im

        def step(t, carry):
            f_re, f_im, b_re, b_im = carry
            f_re, f_im = update(sf_scr, _rows(t * SUBLANES, SUBLANES), 0, f_re, f_im)
            b_re, b_im = update(sb_scr, _rows((tc - 1 - t) * SUBLANES, SUBLANES), 1, b_re, b_im)
            return f_re, f_im, b_re, b_im

        carry = lax.fori_loop(0, tc, step, carry, unroll=2)
        for d, r0, st_scr in ((0, rf, sf_scr), (1, rb, sb_scr)):
            y = jnp.dot(st_scr[...].astype(BF16), cc_ref[d, 0], preferred_element_type=F32)
            yt_scr[pl.ds(r0, rows), :] += _pick_slot(y, ns, nc)
        return carry

    f_re, f_im, b_re, b_im = lax.fori_loop(
        0, n_chunks, chunk, (s0r_ref[0, 0, 0], s0i_ref[0, 0, 0], s0r_ref[0, 0, 1], s0i_ref[0, 0, 1]))
    fr_ref[0, 0, 0] = f_re
    fi_ref[0, 0, 0] = f_im
    fr_ref[0, 0, 1] = b_re
    fi_ref[0, 0, 1] = b_im
    for k in range(nc):
        for s in range(ns):
            blk = (slice(s * seq, (s + 1) * seq), slice(k * CH, (k + 1) * CH))
            y_ref[blk] = (d_ref[:, k * CH:(k + 1) * CH] * u_ref[blk]
                          + yt_scr[pl.ds(k * ns + s, seq, stride=SUBLANES), :])


def _s5(rec, bb, cc, a_re, a_im, d_skip, s0_re, s0_im, seq, ns, nc):
    r = rec.shape[0]
    n_cb = N_CH // nc
    tc = 128
    wide = nc * CH
    state = pl.BlockSpec((1, 1, 2, SUBLANES, S5_ST), lambda g, c: (g, c, 0, 0, 0))
    aspec = pl.BlockSpec((2, 1, SUBLANES, S5_ST), lambda g, c: (0, c, 0, 0))
    st_shape = jax.ShapeDtypeStruct((r // (ns * seq), n_cb, 2, SUBLANES, S5_ST), F32)
    return pl.pallas_call(
        functools.partial(_s5_kernel, seq=seq, tc=tc, ns=ns, nc=nc),
        grid=(r // (ns * seq), n_cb),
        in_specs=[pl.BlockSpec((ns * seq, wide), lambda g, c: (g, 2 * n_cb + c)),
                  pl.BlockSpec((2, 1, wide, 2 * S5_ST), lambda g, c: (0, c, 0, 0)),
                  pl.BlockSpec((2, 1, 2 * S5_ST, wide), lambda g, c: (0, c, 0, 0)),
                  aspec, aspec,
                  pl.BlockSpec((1, wide), lambda g, c: (0, c)),
                  state, state],
        out_specs=[pl.BlockSpec((ns * seq, wide), lambda g, c: (g, c)), state, state],
        out_shape=[jax.ShapeDtypeStruct((r, D_S5), F32), st_shape, st_shape],
        scratch_shapes=[pltpu.VMEM((seq * SUBLANES, CH), F32), pltpu.VMEM((seq * SUBLANES, CH), F32),
                        pltpu.VMEM((tc * SUBLANES, 2 * S5_ST), F32), pltpu.VMEM((tc * SUBLANES, 2 * S5_ST), F32)],
        compiler_params=_cparams(("parallel", "parallel"), 48),
        name="s5",
    )(rec, bb, cc, a_re, a_im, d_skip, s0_re, s0_im)


def _s5_params(lam_re, lam_im, log_dt, b_re, b_im, c_re, c_im, ns, nc):
    dt = jnp.exp(log_dt)[..., None]
    mag = jnp.exp(lam_re * dt)
    abar_re, abar_im = mag * jnp.cos(lam_im * dt), mag * jnp.sin(lam_im * dt)
    den = lam_re * lam_re + lam_im * lam_im
    nr, ni = abar_re - 1, abar_im
    cr = (nr * lam_re + ni * lam_im) / den
    ci = (ni * lam_re - nr * lam_im) / den
    bb_re = cr[..., None] * b_re - ci[..., None] * b_im
    bb_im = cr[..., None] * b_im + ci[..., None] * b_re
    gpc = CH // S5_GROUP
    eye = jnp.eye(gpc, dtype=F32)

    def in_map(b):
        b = b.reshape(2, N_CH, gpc, S5_STATE, S5_GROUP).transpose(0, 1, 2, 4, 3)
        return (b[:, :, :, :, None, :] * eye[None, None, :, None, :, None]).reshape(2, N_CH, CH, S5_ST)

    def out_map(c):
        c = c.reshape(2, N_CH, gpc, S5_GROUP, S5_STATE).transpose(0, 1, 2, 4, 3)
        return (c[:, :, :, :, None, :] * eye[None, None, :, None, :, None]).reshape(2, N_CH, S5_ST, CH)

    bb = jnp.concatenate([in_map(bb_re), in_map(bb_im)], axis=-1).astype(BF16)
    cc = jnp.concatenate([out_map(c_re), -out_map(c_im)], axis=-2).astype(BF16)
    n_cb = N_CH // nc
    bb = bb.reshape(2, n_cb, nc * CH, 2 * S5_ST)
    cc = cc.reshape(2, n_cb, nc, 2 * S5_ST, CH).transpose(0, 1, 3, 2, 4).reshape(2, n_cb, 2 * S5_ST, nc * CH)

    def per_slot(a):
        a = jnp.broadcast_to(a.reshape(2, n_cb, nc, 1, S5_ST), (2, n_cb, nc, ns, S5_ST))
        return a.reshape(2, n_cb, SUBLANES, S5_ST)

    return bb, cc, per_slot(abar_re), per_slot(abar_im)


def _out_proj_kernel(x_ref, oa_ref, ob_ref, y_ref, wg_ref, bg_ref, wo_ref, mod_ref, g_ref, *refs, route):
    if route:
        r_ref, _, x1_ref, h2_ref, route_ref = refs
    else:
        x1_ref, h2_ref = refs
    z = jax.nn.gelu(y_ref[...])
    gl = jnp.dot(z.astype(BF16), wg_ref[...], preferred_element_type=F32) + bg_ref[...]
    oc = z * jax.nn.sigmoid(gl)
    mix = (jnp.dot(oa_ref[...], wo_ref[:D_ATTN, :], preferred_element_type=F32)
           + jnp.dot(ob_ref[...].astype(BF16), wo_ref[D_ATTN:D_ATTN + D_LRU, :], preferred_element_type=F32)
           + jnp.dot(oc.astype(BF16), wo_ref[D_ATTN + D_LRU:, :], preferred_element_type=F32))
    x1 = x_ref[...] + mod_ref[0, 2:3, :] * mix
    x1_ref[...] = x1
    h2 = _rms_mod(x1, g_ref[...], mod_ref[0, 3:4, :], mod_ref[0, 4:5, :])
    h2_ref[...] = h2.astype(h2_ref.dtype)
    if route:
        hi = h2.astype(BF16)
        lo = (h2 - hi.astype(F32)).astype(BF16)
        pr = (jnp.dot(hi, r_ref[...], preferred_element_type=F32)
              + jnp.dot(lo, r_ref[...], preferred_element_type=F32))
        logits = pr + pltpu.roll(pr, LANES - N_EXPERTS, axis=1)
        lane = lax.broadcasted_iota(jnp.int32, logits.shape, 1)
        lg = jnp.where(lane < N_EXPERTS, logits, -jnp.inf)
        m1 = jnp.max(lg, axis=-1, keepdims=True)
        i1 = jnp.min(jnp.where(lg == m1, lane, LANES), axis=-1, keepdims=True)
        lg2 = jnp.where(lane == i1, -jnp.inf, lg)
        m2 = jnp.max(lg2, axis=-1, keepdims=True)
        i2 = jnp.min(jnp.where(lg2 == m2, lane, LANES), axis=-1, keepdims=True)
        e2 = jnp.exp(m2 - m1)
        den = 1.0 + e2
        route_ref[...] = (jnp.where(lane == 0, i1.astype(F32), 0.0) + jnp.where(lane == 1, i2.astype(F32), 0.0)
                          + jnp.where(lane == 2, 1.0 / den, 0.0) + jnp.where(lane == 3, e2 / den, 0.0))


def _out_proj(x2d, oa, ob, y, w_glu_bf, b_glu, w_out_bf, mod, mod_base, rows_per_mod, g, routing):
    r, d = x2d.shape
    tm = 512
    tiles_per_mod = rows_per_mod // tm
    route = routing is not None
    row = lambda n: pl.BlockSpec((tm, n), lambda i: (i, 0))
    full = lambda a: pl.BlockSpec(a.shape, lambda i: (0,) * a.ndim, pipeline_mode=pl.Buffered(1))
    args = [x2d, oa, ob, y, w_glu_bf, b_glu, w_out_bf, mod, g]
    in_specs = [row(d), row(D_ATTN), row(D_LRU), row(D_S5), full(w_glu_bf), full(b_glu), full(w_out_bf),
                pl.BlockSpec((1, N_MOD, d), lambda i: (mod_base + i // tiles_per_mod, 0, 0)), full(g)]
    aliases = {}
    if route:
        router_pad, shared_rows, row_base = routing
        args += [router_pad, shared_rows]
        in_specs += [full(router_pad), pl.BlockSpec(memory_space=pl.ANY)]
        aliases = {len(args) - 1: 1}
        out_shape = [jax.ShapeDtypeStruct((r, d), F32), jax.ShapeDtypeStruct(shared_rows.shape, F32),
                     jax.ShapeDtypeStruct((r, LANES), F32)]
        out_specs = [row(d), pl.BlockSpec((tm, d), lambda i: (i + row_base // tm, 0)), row(LANES)]
    else:
        out_shape = [jax.ShapeDtypeStruct((r, d), F32), jax.ShapeDtypeStruct((r, d), BF16)]
        out_specs = [row(d), row(d)]
    return pl.pallas_call(
        functools.partial(_out_proj_kernel, route=route),
        grid=(r // tm,),
        in_specs=in_specs,
        out_specs=out_specs,
        out_shape=out_shape,
        input_output_aliases=aliases,
        compiler_params=_cparams(("parallel",), 56),
        name="out_proj",
    )(*args)


def _swiglu_part(h, w1, w3, w2):
    a = jnp.dot(h, w1, preferred_element_type=F32)
    b = jnp.dot(h, w3, preferred_element_type=F32)
    return jnp.dot((jax.nn.silu(a) * b).astype(BF16), w2, preferred_element_type=F32)


def _ffn_kernel(h_ref, x_ref, mod_ref, w1_ref, w3_ref, w2_ref, o_ref):
    f = pl.program_id(1)
    part = _swiglu_part(h_ref[...], w1_ref[...], w3_ref[...], w2_ref[...])

    @pl.when(f == 0)
    def _():
        o_ref[...] = part

    @pl.when(f > 0)
    def _():
        o_ref[...] += part

    @pl.when(f == pl.num_programs(1) - 1)
    def _():
        o_ref[...] = x_ref[...] + mod_ref[0, 5:6, :] * o_ref[...]


def _ffn(h2, x1, mod, mod_base, rows_per_mod, w1, w3, w2):
    r, d = x1.shape
    tm, tf = 1024, 512
    tiles_per_mod = rows_per_mod // tm
    return pl.pallas_call(
        _ffn_kernel,
        grid=(r // tm, w1.shape[1] // tf),
        in_specs=[pl.BlockSpec((tm, d), lambda i, f: (i, 0)),
                  pl.BlockSpec((tm, d), lambda i, f: (i, 0), pipeline_mode=pl.Buffered(1)),
                  pl.BlockSpec((1, N_MOD, d), lambda i, f: (mod_base + i // tiles_per_mod, 0, 0)),
                  pl.BlockSpec((d, tf), lambda i, f: (0, f)),
                  pl.BlockSpec((d, tf), lambda i, f: (0, f)),
                  pl.BlockSpec((tf, d), lambda i, f: (f, 0))],
        out_specs=pl.BlockSpec((tm, d), lambda i, f: (i, 0)),
        out_shape=jax.ShapeDtypeStruct((r, d), F32),
        compiler_params=_cparams(("parallel", "arbitrary"), 60),
        name="ffn",
    )(h2, x1, mod, w1, w3, w2)


MOE_TM = 512
MOE_TF = 1408
MOE_TT = 256


def _route_plan(route):
    t = route.shape[0]
    n_pairs = 2 * t
    n_tiles = n_pairs // MOE_TM + N_EXPERTS
    experts = route[:, :2].astype(jnp.int32).reshape(n_pairs)
    onehot = (experts[:, None] == jnp.arange(N_EXPERTS, dtype=jnp.int32)[None]).astype(jnp.int32)
    csum = jnp.cumsum(onehot, axis=0)
    rank = jnp.sum(onehot * csum, axis=1) - 1
    counts = csum[-1]
    padded = (counts + MOE_TM - 1) // MOE_TM * MOE_TM
    ends = jnp.cumsum(padded)
    pos = jnp.sum(onehot * (ends - padded)[None], axis=1) + rank
    src = jnp.zeros((n_tiles * MOE_TM,), jnp.int32).at[pos].set(
        jnp.arange(n_pairs, dtype=jnp.int32) // 2, unique_indices=True, mode='promise_in_bounds')
    tile_start = jnp.arange(n_tiles, dtype=jnp.int32) * MOE_TM
    tile_valid = (tile_start < ends[-1]).astype(jnp.int32)
    tile_expert = jnp.sum((tile_start[:, None] >= ends[None]).astype(jnp.int32), axis=1)
    last_expert = jnp.max(jnp.where(counts > 0, jnp.arange(N_EXPERTS, dtype=jnp.int32), 0))
    tile_expert = jnp.where(tile_valid == 1, tile_expert, last_expert)
    return pos, src, tile_expert, tile_valid


def _gather_rows(idx_of_row, n_rows, src_hbm, dst, sem):
    def row(r, carry):
        pltpu.make_async_copy(src_hbm.at[pl.ds(idx_of_row(r), 1)], dst.at[pl.ds(r, 1)], sem).start()
        return carry
    lax.fori_loop(0, n_rows, row, 0, unroll=8)


def _wait_rows(n_rows, src_hbm, dst, sem):
    pltpu.make_async_copy(src_hbm.at[pl.ds(0, n_rows)], dst, sem).wait()


def _moe_experts_kernel(src_ref, texp_ref, valid_ref, h_hbm, w1_ref, w3_ref, w2_ref, o_ref, xbuf, xbf, sem,
                        *, rows_per_step):
    i = pl.program_id(0)
    f = pl.program_id(1)
    n_i = pl.num_programs(0)

    @pl.when(f == 0)
    def _():
        @pl.when(i == 0)
        def _():
            _gather_rows(lambda r: src_ref[r], MOE_TM, h_hbm, xbuf, sem.at[0])

        _wait_rows(MOE_TM, h_hbm, xbuf, sem.at[0])
        xbf[...] = xbuf[...].astype(BF16)

    nxt = jnp.where(i + 1 < n_i, i + 1, 0)

    def start_next_rows():
        for r in range(rows_per_step):
            row = f * rows_per_step + r
            pltpu.make_async_copy(h_hbm.at[pl.ds(src_ref[nxt * MOE_TM + row], 1)], xbuf.at[pl.ds(row, 1)],
                                  sem.at[0]).start()

    valid = valid_ref[i] == 1

    @pl.when(valid)
    def _():
        part = _swiglu_part(xbf[...], w1_ref[0], w3_ref[0], w2_ref[0])
        start_next_rows()

        @pl.when(f == 0)
        def _():
            o_ref[...] = part

        @pl.when(f > 0)
        def _():
            o_ref[...] += part

    @pl.when(jnp.logical_not(valid))
    def _():
        start_next_rows()

        @pl.when(f == 0)
        def _():
            o_ref[...] = jnp.zeros_like(o_ref)

    @pl.when((i == n_i - 1) & (f == pl.num_programs(1) - 1))
    def _():
        _wait_rows(MOE_TM, h_hbm, xbuf, sem.at[0])


def _moe_experts(h2, src, tile_expert, tile_valid, w1, w3, w2):
    t, d = h2.shape
    n_tiles = tile_expert.shape[0]
    n_f = w1.shape[2] // MOE_TF

    def wmap(i, f, src, texp, valid):
        return (texp[i], 0, jnp.where(valid[i] == 1, f, n_f - 1))

    def w2map(i, f, src, texp, valid):
        return (texp[i], jnp.where(valid[i] == 1, f, n_f - 1), 0)

    grid_spec = pltpu.PrefetchScalarGridSpec(
        num_scalar_prefetch=3,
        grid=(n_tiles, n_f),
        in_specs=[pl.BlockSpec(memory_space=pl.ANY),
                  pl.BlockSpec((1, d, MOE_TF), wmap),
                  pl.BlockSpec((1, d, MOE_TF), wmap),
                  pl.BlockSpec((1, MOE_TF, d), w2map)],
        out_specs=pl.BlockSpec((MOE_TM, d), lambda i, f, *_: (i, 0)),
        scratch_shapes=[pltpu.VMEM((MOE_TM, d), F32), pltpu.VMEM((MOE_TM, d), BF16),
                        pltpu.SemaphoreType.DMA((1,))])
    return pl.pallas_call(
        functools.partial(_moe_experts_kernel, rows_per_step=MOE_TM // n_f),
        grid_spec=grid_spec,
        out_shape=jax.ShapeDtypeStruct((n_tiles * MOE_TM, d), F32),
        compiler_params=_cparams(("arbitrary", "arbitrary"), 62),
        name="moe_experts",
    )(src, tile_expert, tile_valid, h2, w1, w3, w2)


def _moe_combine_kernel(pos_ref, x_ref, mod_ref, route_ref, y_hbm, o_ref, buf, sem, *, tok_base):
    i = pl.program_id(0)
    slot = i % 2

    def gather(tile, slot):
        for k in range(2):
            _gather_rows(lambda r, k=k: pos_ref[2 * (tok_base + tile * MOE_TT + r) + k], MOE_TT, y_hbm,
                         buf.at[slot, k], sem.at[slot])

    @pl.when(i == 0)
    def _():
        gather(0, 0)

    for k in range(2):
        _wait_rows(MOE_TT, y_hbm, buf.at[slot, k], sem.at[slot])

    @pl.when(i + 1 < pl.num_programs(0))
    def _():
        gather(i + 1, 1 - slot)

    mixed = route_ref[:, 2:3] * buf[slot, 0] + route_ref[:, 3:4] * buf[slot, 1]
    o_ref[...] = x_ref[...] + mod_ref[0, 5:6, :] * mixed


def _moe_combine(pos, x1, mod, mod_base, rows_per_mod, route, y_rows, tok_base):
    r, d = x1.shape
    tiles_per_mod = rows_per_mod // MOE_TT
    grid_spec = pltpu.PrefetchScalarGridSpec(
        num_scalar_prefetch=1,
        grid=(r // MOE_TT,),
        in_specs=[pl.BlockSpec((MOE_TT, d), lambda i, *_: (i, 0)),
                  pl.BlockSpec((1, N_MOD, d), lambda i, *_: (mod_base + i // tiles_per_mod, 0, 0)),
                  pl.BlockSpec((MOE_TT, LANES), lambda i, *_: (i, 0)),
                  pl.BlockSpec(memory_space=pl.ANY)],
        out_specs=pl.BlockSpec((MOE_TT, d), lambda i, *_: (i, 0)),
        scratch_shapes=[pltpu.VMEM((2, 2, MOE_TT, d), F32), pltpu.SemaphoreType.DMA((2,))])
    return pl.pallas_call(
        functools.partial(_moe_combine_kernel, tok_base=tok_base),
        grid_spec=grid_spec,
        out_shape=jax.ShapeDtypeStruct((r, d), F32),
        compiler_params=_cparams(("arbitrary",), 32),
        name="moe_combine",
    )(pos, x1, mod, route, y_rows)


def _slot_states(s, ns, nc, width):
    ng = s.shape[0] // ns
    s = s.reshape(ng, ns, 2, N_CH // nc, nc, width).transpose(0, 3, 2, 4, 1, 5)
    return s.reshape(ng, N_CH // nc, 2, SUBLANES, width)


def _unslot_states(s, ns, nc, width):
    ng = s.shape[0]
    s = s.reshape(ng, N_CH // nc, 2, nc, ns, width).transpose(0, 4, 2, 1, 3, 5)
    return s.reshape(ng * ns, 2, N_CH * width)


def _token_mixer(x2d, batch, seq, mod, mod_base, rows_per_mod, p, kv_ctx, lru_h0, s5_h0, prev_kv=None,
                 routing=None):
    is_ctx = kv_ctx is None
    ns = min(batch, SUBLANES)
    nc = SUBLANES // ns
    outs = _in_proj(x2d, p['g_mix'], mod, mod_base, rows_per_mod, p['w_in'], p['q_g'], p['k_g'], p['ones_bd'],
                    seq, is_ctx, prev_kv)
    q, k, v, rec = outs[:4]
    if is_ctx:
        oa = _ctx_attn(q, k, v, seq)
        lru_h0 = jnp.zeros((batch, 2, D_LRU), F32)
        s5_h0 = (jnp.zeros((batch, 2, N_S5_GROUPS * S5_STATE), F32),) * 2
    else:
        oa = _nbr_attn(q, k, v, kv_ctx[0], kv_ctx[1], kv_ctx[2], p['tb'], seq)
    lru_w, lru_b = _rglru_params(p['lru_wa'], p['lru_ba'], p['lru_wi'], p['lru_bi'], nc)
    ob, lru_fin = _rglru(rec, p['conv_w'], p['conv_b'], lru_w, lru_b, p['lru_lam'],
                         _slot_states(lru_h0, ns, nc, CH), seq, ns, nc)
    s5_bb, s5_cc, s5_are, s5_aim = _s5_params(*p['s5'], ns, nc)
    y, fin_re, fin_im = _s5(rec, s5_bb, s5_cc, s5_are, s5_aim, p['d_skip'],
                            _slot_states(s5_h0[0], ns, nc, S5_ST), _slot_states(s5_h0[1], ns, nc, S5_ST),
                            seq, ns, nc)
    res = _out_proj(x2d, oa, ob, y, p['w_glu'], p['b_glu'], p['w_out'], mod, mod_base, rows_per_mod,
                    p['g_ffn'], routing)
    if not is_ctx:
        return res, None
    state = (outs[4], outs[5], _unslot_states(lru_fin, ns, nc, CH),
             _unslot_states(fin_re, ns, nc, S5_ST).reshape(batch, 2, N_S5_GROUPS, S5_STATE),
             _unslot_states(fin_im, ns, nc, S5_ST).reshape(batch, 2, N_S5_GROUPS, S5_STATE))
    return res, state


def kernel(x_prompt, x_sample, c, cache_k, cache_v, state_lru, state_s5_re, state_s5_im, c_ctx, norm_mix_g, norm_ffn_g, w_mod, b_mod, w_in, w_out, q_norm_g, k_norm_g, rpb, lru_conv_w, lru_conv_b, lru_wa, lru_ba, lru_wi, lru_bi, lru_lam, s5_lam_re, s5_lam_im, s5_log_dt, s5_b_re, s5_b_im, s5_c_re, s5_c_im, s5_d, s5_w_glu, s5_b_glu, ffn_w1, ffn_w3, ffn_w2, moe_router, moe_w1, moe_w3, moe_w2):
    batch, seq, d = x_prompt.shape
    dec_batch, dec_seq, _ = x_sample.shape
    depth = w_in.shape[0]
    assert dec_batch + 1 <= SUBLANES and batch % SUBLANES == 0 and dec_batch in (4, 8)

    cvecs = jnp.concatenate([c_ctx[None], c, jnp.zeros((SUBLANES - 1 - dec_batch, d), F32)], axis=0)
    mods = _adaln(cvecs, w_mod, b_mod).reshape(depth, SUBLANES, N_MOD, d)

    heads_per_tile = 512 // HEAD_DIM
    ones_bd = jnp.asarray(np.kron(np.eye(heads_per_tile // 2), np.ones((HEAD_DIM, HEAD_DIM))), BF16)

    xp = x_prompt.reshape(batch * seq, d)
    xs = x_sample.reshape(dec_batch * dec_seq, d)
    n_tok = (batch * seq, dec_batch * dec_seq)
    kv, lrus, s5rs, s5is = None, [], [], []
    for l in range(depth):
        p = {
            'g_mix': norm_mix_g[l][None], 'g_ffn': norm_ffn_g[l][None],
            'w_in': w_in[l].astype(BF16), 'w_out': w_out[l].astype(BF16),
            'q_g': jnp.tile(q_norm_g[l], heads_per_tile)[None], 'k_g': jnp.tile(k_norm_g[l], heads_per_tile)[None],
            'ones_bd': ones_bd, 'tb': _bias_table(rpb[l]),
            'conv_w': lru_conv_w[l], 'conv_b': lru_conv_b[l][None],
            'lru_wa': lru_wa[l], 'lru_ba': lru_ba[l], 'lru_wi': lru_wi[l], 'lru_bi': lru_bi[l], 'lru_lam': lru_lam[l],
            's5': (s5_lam_re[l], s5_lam_im[l], s5_log_dt[l], s5_b_re[l], s5_b_im[l], s5_c_re[l], s5_c_im[l]),
            'd_skip': s5_d[l][None], 'w_glu': s5_w_glu[l].astype(BF16), 'b_glu': s5_b_glu[l][None],
        }
        j = l // 2
        dense = l % 2 == 0
        route_p = route_s = None
        if not dense:
            r_hi = moe_router[j].astype(BF16)
            r_lo = (moe_router[j] - r_hi.astype(F32)).astype(BF16)
            router = jnp.pad(jnp.concatenate([r_hi, r_lo], axis=1), ((0, 0), (0, LANES - 2 * N_EXPERTS)))
            route_p = (router, jnp.zeros((sum(n_tok), d), F32), 0)
        res_p, (k_c, v_c, lru_c, s5r_c, s5i_c) = _token_mixer(
            xp, batch, seq, mods[l], 0, batch * seq, p, None, None, None, prev_kv=kv, routing=route_p)
        kv = (k_c, v_c)
        lrus.append(lru_c)
        s5rs.append(s5r_c)
        s5is.append(s5i_c)
        if not dense:
            route_s = (router, res_p[1], n_tok[0])
        res_s, _ = _token_mixer(
            xs, dec_batch, dec_seq, mods[l], 1, dec_seq, p, (cache_k, cache_v, l), state_lru[:, l],
            (state_s5_re[:, l].reshape(dec_batch, 2, -1), state_s5_im[:, l].reshape(dec_batch, 2, -1)),
            routing=route_s)
        if dense:
            w = (ffn_w1[j].astype(BF16), ffn_w3[j].astype(BF16), ffn_w2[j].astype(BF16))
            xp = _ffn(res_p[1], res_p[0], mods[l], 0, batch * seq, *w)
            xs = _ffn(res_s[1], res_s[0], mods[l], 1, dec_seq, *w)
        else:
            h2 = res_s[1]
            pos, src, tile_expert, tile_valid = _route_plan(jnp.concatenate([res_p[2], res_s[2]], axis=0))
            y_rows = _moe_experts(h2, src, tile_expert, tile_valid,
                                  moe_w1[j].astype(BF16), moe_w3[j].astype(BF16), moe_w2[j].astype(BF16))
            xp = _moe_combine(pos, res_p[0], mods[l], 0, batch * seq, res_p[2], y_rows, 0)
            xs = _moe_combine(pos, res_s[0], mods[l], 1, dec_seq, res_s[2], y_rows, batch * seq)
    return (xp.reshape(batch, seq, d), xs.reshape(dec_batch, dec_seq, d), kv[0], kv[1],
            jnp.stack(lrus, axis=1), jnp.stack(s5rs, axis=1), jnp.stack(s5is, axis=1))
```

```python
import functools

import numpy as np
import jax
import jax.numpy as jnp
from jax import lax
from jax.experimental import pallas as pl
from jax.experimental.pallas import tpu as pltpu

F32 = jnp.float32
BF16 = jnp.bfloat16

D_MODEL = 2048
N_HEADS = 16
HEAD_DIM = 64
D_ATTN = N_HEADS * HEAD_DIM
GRID_W = 64
WIN_R = 8
WIN_C = 16
D_LRU = 512
LRU_BLOCK = 64
CONV_W = 4
LRU_C = 8.0
D_S5 = 512
S5_GROUP = 16
N_S5_GROUPS = 32
S5_STATE = 64
D_REC = 2 * D_LRU + D_S5
D_IN = 3 * D_ATTN + D_REC
N_MOD = 6
N_EXPERTS = 8
EPS = 1e-6
NEG = -1e30

LANES = 128
SUBLANES = 8
MIB = 1024 * 1024

CH = 128
N_CH = D_LRU // CH
S5_ST = (CH // S5_GROUP) * S5_STATE


def _cparams(sem, vmem_mib):
    return pltpu.CompilerParams(dimension_semantics=sem, vmem_limit_bytes=vmem_mib * MIB)


def _adaln_kernel(c_ref, w_ref, b_ref, o_ref):
    s = jax.nn.silu(c_ref[...]).astype(BF16)
    o_ref[0] = jnp.dot(s, w_ref[0].astype(BF16), preferred_element_type=F32) + b_ref[0]


def _adaln(cvecs, w_mod, b_mod):
    depth, d, n = w_mod.shape
    tn = 1024
    return pl.pallas_call(
        _adaln_kernel,
        grid=(depth, n // tn),
        in_specs=[pl.BlockSpec((SUBLANES, d), lambda l, j: (0, 0)),
                  pl.BlockSpec((1, d, tn), lambda l, j: (l, 0, j)),
                  pl.BlockSpec((1, 1, tn), lambda l, j: (l, 0, j))],
        out_specs=pl.BlockSpec((1, SUBLANES, tn), lambda l, j: (l, 0, j)),
        out_shape=jax.ShapeDtypeStruct((depth, SUBLANES, n), F32),
        compiler_params=_cparams(("parallel", "parallel"), 40),
        name="adaln",
    )(cvecs, w_mod, b_mod.reshape(depth, 1, n))


def _rms_mod(x, g, shift, scale):
    xf = x * lax.rsqrt(jnp.mean(x * x, axis=-1, keepdims=True) + EPS)
    return (xf * g) * (1 + scale) + shift


def _in_proj_kernel(x_ref, g_ref, mod_ref, w_ref, qg_ref, kg_ref, ones_ref, *refs, seq, write_cache, n_prev):
    kp_ref = vp_ref = None
    if write_cache and n_prev:
        kp_ref, vp_ref, *refs = refs
    if write_cache:
        q_ref, k_ref, v_ref, rec_ref, kc_ref, vc_ref, h_scr = refs
    else:
        q_ref, k_ref, v_ref, rec_ref, h_scr = refs
    j = pl.program_id(1)
    tm, tn = q_ref.shape
    heads = tn // HEAD_DIM

    @pl.when(j == 0)
    def _():
        h_scr[...] = _rms_mod(x_ref[...], g_ref[...], mod_ref[0, 0:1, :], mod_ref[0, 1:2, :]).astype(BF16)

    y = jnp.dot(h_scr[...], w_ref[...], preferred_element_type=F32)

    def to_cache(c_ref, prev_ref, val):
        if prev_ref is not None:
            c_ref[:, :n_prev] = prev_ref[...]
        for b in range(tm // seq):
            for h in range(heads):
                c_ref[b, n_prev, h, :, :] = val[b * seq:(b + 1) * seq, h * HEAD_DIM:(h + 1) * HEAD_DIM]

    @pl.when(j < 4)
    def _():
        y2 = y * y
        hi = y2.astype(BF16)
        lo = (y2 - hi.astype(F32)).astype(BF16)
        half = ones_ref.shape[0]
        ss = jnp.concatenate(
            [jnp.dot(hi[:, c:c + half], ones_ref[...], preferred_element_type=F32)
             + jnp.dot(lo[:, c:c + half], ones_ref[...], preferred_element_type=F32) for c in range(0, tn, half)],
            axis=1)
        gain = jnp.where(j < 2, qg_ref[...], kg_ref[...])
        yn = (y * lax.rsqrt(ss * (1.0 / HEAD_DIM) + EPS)) * gain

        @pl.when(j < 2)
        def _():
            q_ref[...] = yn.astype(BF16)

        @pl.when(j >= 2)
        def _():
            k_ref[...] = yn.astype(BF16)
            if write_cache:
                to_cache(kc_ref, kp_ref, yn)

    @pl.when((j >= 4) & (j < 6))
    def _():
        v_ref[...] = y.astype(BF16)
        if write_cache:
            to_cache(vc_ref, vp_ref, y)

    @pl.when(j >= 6)
    def _():
        rec_ref[...] = y


def _in_proj(x2d, g, mod, mod_base, rows_per_mod, w_bf, qg, kg, ones_bd, seq, write_cache, prev_kv):
    r, d = x2d.shape
    tm, tn = 512, 512
    n_i = r // tm
    tiles_per_mod = rows_per_mod // tm
    bpt = tm // seq
    hpt = tn // HEAD_DIM
    n_prev = prev_kv[0].shape[1] if prev_kv else 0

    def col(lo, n):
        return lambda i, j: (i, jnp.clip(j - lo, 0, n - 1))

    def cache_spec(lo, n_layers):
        return pl.BlockSpec((bpt, n_layers, hpt, seq, HEAD_DIM), lambda i, j: (i, 0, jnp.clip(j - lo, 0, 1), 0, 0))

    args = [x2d, g, mod, w_bf, qg, kg, ones_bd]
    in_specs = [pl.BlockSpec((tm, d), lambda i, j: (i, 0)),
                pl.BlockSpec((1, d), lambda i, j: (0, 0)),
                pl.BlockSpec((1, N_MOD, d), lambda i, j: (mod_base + i // tiles_per_mod, 0, 0)),
                pl.BlockSpec((d, tn), lambda i, j: (0, j)),
                pl.BlockSpec((1, tn), lambda i, j: (0, 0)),
                pl.BlockSpec((1, tn), lambda i, j: (0, 0)),
                pl.BlockSpec(ones_bd.shape, lambda i, j: (0, 0))]
    out_shape = [jax.ShapeDtypeStruct((r, D_ATTN), BF16)] * 3 + [jax.ShapeDtypeStruct((r, D_REC), F32)]
    out_specs = [pl.BlockSpec((tm, tn), col(0, 2)), pl.BlockSpec((tm, tn), col(2, 2)),
                 pl.BlockSpec((tm, tn), col(4, 2)), pl.BlockSpec((tm, tn), col(6, 3))]
    if write_cache:
        if n_prev:
            args += list(prev_kv)
            in_specs += [cache_spec(2, n_prev), cache_spec(4, n_prev)]
        cshape = jax.ShapeDtypeStruct((r // seq, n_prev + 1, N_HEADS, seq, HEAD_DIM), F32)
        out_shape += [cshape, cshape]
        out_specs += [cache_spec(2, n_prev + 1), cache_spec(4, n_prev + 1)]
    return pl.pallas_call(
        functools.partial(_in_proj_kernel, seq=seq, write_cache=write_cache, n_prev=n_prev),
        grid=(n_i, D_IN // tn),
        in_specs=in_specs,
        out_specs=out_specs,
        out_shape=out_shape,
        scratch_shapes=[pltpu.VMEM((tm, d), BF16)],
        compiler_params=_cparams(("parallel", "arbitrary"), 56),
        name="in_proj",
    )(*args)


def _qk(q, k):
    return lax.dot_general(q, k, (((1,), (1,)), ((), ())), preferred_element_type=F32)


def _head_lanes(n_rows, first):
    lane = lax.broadcasted_iota(jnp.int32, (n_rows, LANES), 1)
    return (lane < HEAD_DIM) if first else (lane >= HEAD_DIM)


def _own(x, first):
    return jnp.where(_head_lanes(x.shape[0], first), x, jnp.zeros_like(x))


def _values_and_ones(v, first):
    ones = jnp.where(_head_lanes(v.shape[0], first), 1.0, 0.0).astype(BF16)
    return jnp.concatenate([_own(v, first), ones], axis=1)


def _exp_rows(parts):
    m = functools.reduce(jnp.maximum, [jnp.max(s, axis=-1, keepdims=True) for s in parts])
    return [jnp.exp(s - m).astype(BF16) for s in parts]


def _ctx_attn_kernel(q_ref, k_ref, v_ref, o_ref):
    scale = HEAD_DIM ** -0.5
    for hp in range(N_HEADS * HEAD_DIM // LANES):
        sl = slice(hp * LANES, (hp + 1) * LANES)
        q, k, v = q_ref[:, sl], k_ref[:, sl], v_ref[:, sl]
        nd = None
        for first in (True, False):
            (e,) = _exp_rows([_qk(_own(q, first), k) * scale])
            part = jnp.dot(e, _values_and_ones(v, first), preferred_element_type=F32)
            nd = part if nd is None else nd + part
        o_ref[:, sl] = (nd[:, :LANES] / nd[:, LANES:]).astype(BF16)


def _ctx_attn(q, k, v, seq):
    r = q.shape[0]
    spec = pl.BlockSpec((seq, D_ATTN), lambda b: (b, 0))
    return pl.pallas_call(
        _ctx_attn_kernel,
        grid=(r // seq,),
        in_specs=[spec, spec, spec],
        out_specs=spec,
        out_shape=jax.ShapeDtypeStruct((r, D_ATTN), BF16),
        compiler_params=_cparams(("parallel",), 32),
        name="ctx_attn",
    )(q, k, v)


def _nbr_plan(seq):
    rows = seq // GRID_W
    wr = min(WIN_R, rows)
    row_start = np.clip(np.arange(rows) - wr // 2, 0, rows - wr)
    rows_per_blk = 256 // GRID_W
    ranges = []
    for qb in range(rows // rows_per_blk):
        rs = row_start[qb * rows_per_blk:(qb + 1) * rows_per_blk]
        lo = int(rs.min()) * GRID_W // LANES * LANES
        hi = -(-(int(rs.max()) + wr) * GRID_W // LANES) * LANES
        ranges.append((lo, hi))
    return rows, wr, row_start, ranges


def _nbr_attn_kernel(q_ref, k_ref, v_ref, kc_ref, vc_ref, tb_ref, o_ref, bias_scr, *, seq):
    rows, wr, row_start, ranges = _nbr_plan(seq)
    scale = HEAD_DIM ** -0.5
    heads = q_ref.shape[1] // HEAD_DIM
    neg = jnp.full((GRID_W, GRID_W), NEG, F32)

    @pl.when(pl.program_id(1) == 0)
    def _():
        for h in range(heads):
            for qr in range(rows):
                for kp in range(rows // 2):
                    blks = []
                    for kr in (2 * kp, 2 * kp + 1):
                        inside = row_start[qr] <= kr < row_start[qr] + wr
                        blks.append(tb_ref[h, kr - qr + WIN_R - 1] if inside else neg)
                    bias_scr[h, qr * GRID_W:(qr + 1) * GRID_W, kp * LANES:(kp + 1) * LANES] = (
                        jnp.concatenate(blks, axis=1))

    kc = jnp.concatenate([kc_ref[0, 0, h] for h in range(heads)], axis=1).astype(BF16)
    vc = jnp.concatenate([vc_ref[0, 0, h] for h in range(heads)], axis=1).astype(BF16)
    for qb, (lo, hi) in enumerate(ranges):
        qs = slice(qb * 256, (qb + 1) * 256)
        q = q_ref[qs, :]
        nd = None
        for h, first in enumerate((True, False)):
            qh = _own(q, first)
            s_loc = _qk(qh, k_ref[lo:hi, :]) * scale + bias_scr[h, qs, lo:hi]
            s_ctx = _qk(qh, kc) * scale
            e_loc, e_ctx = _exp_rows([s_loc, s_ctx])
            part = (jnp.dot(e_loc, _values_and_ones(v_ref[lo:hi, :], first), preferred_element_type=F32)
                    + jnp.dot(e_ctx, _values_and_ones(vc, first), preferred_element_type=F32))
            nd = part if nd is None else nd + part
        o_ref[qs, :] = (nd[:, :LANES] / nd[:, LANES:]).astype(BF16)


def _nbr_attn(q, k, v, cache_k, cache_v, layer, tb, seq):
    r = q.shape[0]
    past = cache_k.shape[3]
    hp = LANES // HEAD_DIM
    spec = pl.BlockSpec((seq, LANES), lambda g, b: (b, g))
    cspec = pl.BlockSpec((1, 1, hp, past, HEAD_DIM), lambda g, b: (b, layer, g, 0, 0))
    return pl.pallas_call(
        functools.partial(_nbr_attn_kernel, seq=seq),
        grid=(N_HEADS // hp, r // seq),
        in_specs=[spec, spec, spec, cspec, cspec,
                  pl.BlockSpec((hp,) + tb.shape[1:], lambda g, b: (g, 0, 0, 0))],
        out_specs=spec,
        out_shape=jax.ShapeDtypeStruct((r, D_ATTN), BF16),
        scratch_shapes=[pltpu.VMEM((hp, seq, seq), F32)],
        compiler_params=_cparams(("parallel", "arbitrary"), 48),
        name="nbr_attn",
    )(q, k, v, cache_k, cache_v, tb)


def _bias_table(rpb_l):
    col = np.arange(GRID_W)
    col_start = np.clip(col - WIN_C // 2, 0, GRID_W - WIN_C)
    col_in = (col[None, :] >= col_start[:, None]) & (col[None, :] < col_start[:, None] + WIN_C)
    dc_idx = np.clip(col[None, :] - col[:, None] + WIN_C - 1, 0, 2 * WIN_C - 2)
    onehot = (dc_idx.reshape(1, -1) == np.arange(2 * WIN_C - 1)[:, None]).astype(np.float32)
    tb = jnp.einsum('hrc,cq->hrq', rpb_l, onehot, precision=lax.Precision.HIGHEST)
    tb = tb.reshape(rpb_l.shape[0], rpb_l.shape[1], GRID_W, GRID_W)
    return jnp.where(col_in[None, None], tb, NEG).astype(F32)


def _expm1(x):
    u = jnp.exp(x)
    um1 = u - 1.0
    near = um1 * x / jnp.where(u == 1.0, 1.0, jnp.log(u))
    return jnp.where(x < -0.5, um1, jnp.where(u == 1.0, x, near))


def _slot0(n_rows, n_cols, ns):
    return (lax.broadcasted_iota(jnp.int32, (n_rows, n_cols), 0) % SUBLANES) < ns


def _per_slot(vec, n_rows, ns, nc):
    if nc == 1:
        return vec
    w = vec.shape[1] // 2
    return jnp.where(_slot0(n_rows, w, ns), vec[:, :w], vec[:, w:])


def _stack_slots(x, ns, nc):
    if nc == 1:
        return x
    m = _slot0(x.shape[0], x.shape[1], ns)
    zero = jnp.zeros_like(x)
    return jnp.concatenate([jnp.where(m, x, zero), jnp.where(m, zero, x)], axis=1)


def _pick_slot(y, ns, nc):
    if nc == 1:
        return y
    w = y.shape[1] // 2
    return jnp.where(_slot0(y.shape[0], w, ns), y[:, :w], y[:, w:])


def _to_time_major(x_ref, t_scr, row0, seq, ns, nc):
    for k in range(nc):
        for s in range(ns):
            t_scr[pl.ds(row0 + k * ns + s, seq, stride=SUBLANES), :] = (
                x_ref[s * seq:(s + 1) * seq, k * CH:(k + 1) * CH])


def _rows(start, n):
    return pl.ds(pl.multiple_of(start, SUBLANES), n)


def _rglru_kernel(x_ref, g_ref, cw_ref, cb_ref, w_ref, b_ref, lam_ref, h0_ref, o_ref, fin_ref,
                  xt_scr, yt_scr, af_scr, bf_scr, ab_scr, bb_scr, *, seq, tc, ns, nc):
    rows = tc * SUBLANES
    n_chunks = seq // tc
    front = (CONV_W // 2) * SUBLANES
    back = (CONV_W - 1 - CONV_W // 2) * SUBLANES

    xt_scr[0:front, :] = jnp.zeros((front, CH), F32)
    xt_scr[front + seq * SUBLANES:front + seq * SUBLANES + back, :] = jnp.zeros((back, CH), F32)
    _to_time_major(x_ref, xt_scr, front, seq, ns, nc)
    yt_scr[...] = jnp.zeros_like(yt_scr)

    def coefficients(r0, d, a_scr, b_scr):
        xc = _per_slot(cb_ref[...], rows, ns, nc)
        for j in range(CONV_W):
            xc = xc + xt_scr[_rows(r0 + j * SUBLANES, rows), :] * _per_slot(cw_ref[j:j + 1, :], rows, ns, nc)
        gates = (jnp.dot(_stack_slots(xc, ns, nc).astype(BF16), w_ref[d, 0], preferred_element_type=F32)
                 + _per_slot(b_ref[d, 0], rows, ns, nc))
        rg = jax.nn.sigmoid(gates[:, :CH])
        ig = jax.nn.sigmoid(gates[:, CH:])
        log_a = -LRU_C * rg * _per_slot(jax.nn.softplus(-lam_ref[d:d + 1, :]), rows, ns, nc)
        a_scr[...] = jnp.exp(log_a)
        b_scr[...] = jnp.sqrt(-_expm1(2 * log_a)) * (ig * xc)

    def chunk(c, carry):
        rf = pl.multiple_of(c * rows, rows)
        rb = pl.multiple_of((n_chunks - 1 - c) * rows, rows)
        coefficients(rf, 0, af_scr, bf_scr)
        coefficients(rb, 1, ab_scr, bb_scr)

        def step(t, carry):
            h_f, h_b = carry
            tf = _rows(t * SUBLANES, SUBLANES)
            tb = _rows((tc - 1 - t) * SUBLANES, SUBLANES)
            h_f = af_scr[tf, :] * h_f + bf_scr[tf, :]
            h_b = ab_scr[tb, :] * h_b + bb_scr[tb, :]
            bf_scr[tf, :] = h_f
            bb_scr[tb, :] = h_b
            return h_f, h_b

        carry = lax.fori_loop(0, tc, step, carry, unroll=8)
        yt_scr[pl.ds(rf, rows), :] += bf_scr[...]
        yt_scr[pl.ds(rb, rows), :] += bb_scr[...]
        return carry

    h_f, h_b = lax.fori_loop(0, n_chunks, chunk, (h0_ref[0, 0, 0], h0_ref[0, 0, 1]))
    fin_ref[0, 0, 0] = h_f
    fin_ref[0, 0, 1] = h_b
    for k in range(nc):
        for s in range(ns):
            blk = (slice(s * seq, (s + 1) * seq), slice(k * CH, (k + 1) * CH))
            o_ref[blk] = yt_scr[pl.ds(k * ns + s, seq, stride=SUBLANES), :] * jax.nn.gelu(g_ref[blk])


def _rglru(rec, conv_w, conv_b, w_g, b_g, lam, h0, seq, ns, nc):
    r = rec.shape[0]
    n_cb = N_CH // nc
    tc = 128
    wide = nc * CH
    n_pad = (seq + CONV_W - 1) * SUBLANES
    state = pl.BlockSpec((1, 1, 2, SUBLANES, CH), lambda g, c: (g, c, 0, 0, 0))
    return pl.pallas_call(
        functools.partial(_rglru_kernel, seq=seq, tc=tc, ns=ns, nc=nc),
        grid=(r // (ns * seq), n_cb),
        in_specs=[pl.BlockSpec((ns * seq, wide), lambda g, c: (g, c)),
                  pl.BlockSpec((ns * seq, wide), lambda g, c: (g, n_cb + c)),
                  pl.BlockSpec((CONV_W, wide), lambda g, c: (0, c)),
                  pl.BlockSpec((1, wide), lambda g, c: (0, c)),
                  pl.BlockSpec((2, 1, wide, 2 * CH), lambda g, c: (0, c, 0, 0)),
                  pl.BlockSpec((2, 1, 1, 2 * wide), lambda g, c: (0, c, 0, 0)),
                  pl.BlockSpec((2, wide), lambda g, c: (0, c)),
                  state],
        out_specs=[pl.BlockSpec((ns * seq, wide), lambda g, c: (g, c)), state],
        out_shape=[jax.ShapeDtypeStruct((r, D_LRU), F32),
                   jax.ShapeDtypeStruct((r // (ns * seq), n_cb, 2, SUBLANES, CH), F32)],
        scratch_shapes=[pltpu.VMEM((n_pad, CH), F32), pltpu.VMEM((seq * SUBLANES, CH), F32)]
        + [pltpu.VMEM((tc * SUBLANES, CH), F32)] * 4,
        compiler_params=_cparams(("parallel", "parallel"), 48),
        name="rglru",
    )(rec, rec, conv_w, conv_b, w_g, b_g, lam, h0)


def _rglru_params(wa, ba, wi, bi, nc):
    bpc = CH // LRU_BLOCK
    n_cb = N_CH // nc
    eye = jnp.eye(bpc, dtype=F32)

    def dense(w):
        w = w.reshape(2, N_CH, bpc, LRU_BLOCK, LRU_BLOCK)
        return (w[:, :, :, :, None, :] * eye[None, None, :, None, :, None]).reshape(2, N_CH, CH, CH)

    w_g = jnp.concatenate([dense(wa), dense(wi)], axis=-1).reshape(2, n_cb, nc * CH, 2 * CH).astype(BF16)
    b_g = jnp.concatenate([ba.reshape(2, N_CH, CH), bi.reshape(2, N_CH, CH)], axis=-1)
    return w_g, b_g.reshape(2, n_cb, 1, nc * 2 * CH)


def _s5_kernel(u_ref, bb_ref, cc_ref, ar_ref, ai_ref, d_ref, s0r_ref, s0i_ref, y_ref, fr_ref, fi_ref,
               ut_scr, yt_scr, sf_scr, sb_scr, *, seq, tc, ns, nc):
    rows = tc * SUBLANES
    n_chunks = seq // tc
    _to_time_major(u_ref, ut_scr, 0, seq, ns, nc)
    yt_scr[...] = jnp.zeros_like(yt_scr)

    def chunk(c, carry):
        rf = pl.multiple_of(c * rows, rows)
        rb = pl.multiple_of((n_chunks - 1 - c) * rows, rows)
        for d, r0, st_scr in ((0, rf, sf_scr), (1, rb, sb_scr)):
            u = _stack_slots(ut_scr[pl.ds(r0, rows), :], ns, nc)
            st_scr[...] = jnp.dot(u.astype(BF16), bb_ref[d, 0], preferred_element_type=F32)

        def update(st_scr, rr, d, s_re, s_im):
            a_re = ar_ref[d, 0]
            a_im = ai_ref[d, 0]
            n_re = a_re * s_re - a_im * s_im + st_scr[rr, :S5_ST]
            n_im = a_re * s_im + a_im * s_re + st_scr[rr, S5_ST:]
            st_scr[rr, :S5_ST] = n_re
            st_scr[rr, S5_ST:] = n_im
            return n_re, n_im

        def step(t, carry):
            f_re, f_im, b_re, b_im = carry
            f_re, f_im = update(sf_scr, _rows(t * SUBLANES, SUBLANES), 0, f_re, f_im)
            b_re, b_im = update(sb_scr, _rows((tc - 1 - t) * SUBLANES, SUBLANES), 1, b_re, b_im)
            return f_re, f_im, b_re, b_im

        carry = lax.fori_loop(0, tc, step, carry, unroll=2)
        for d, r0, st_scr in ((0, rf, sf_scr), (1, rb, sb_scr)):
            y = jnp.dot(st_scr[...].astype(BF16), cc_ref[d, 0], preferred_element_type=F32)
            yt_scr[pl.ds(r0, rows), :] += _pick_slot(y, ns, nc)
        return carry

    f_re, f_im, b_re, b_im = lax.fori_loop(
        0, n_chunks, chunk, (s0r_ref[0, 0, 0], s0i_ref[0, 0, 0], s0r_ref[0, 0, 1], s0i_ref[0, 0, 1]))
    fr_ref[0, 0, 0] = f_re
    fi_ref[0, 0, 0] = f_im
    fr_ref[0, 0, 1] = b_re
    fi_ref[0, 0, 1] = b_im
    for k in range(nc):
        for s in range(ns):
            blk = (slice(s * seq, (s + 1) * seq), slice(k * CH, (k + 1) * CH))
            y_ref[blk] = (d_ref[:, k * CH:(k + 1) * CH] * u_ref[blk]
                          + yt_scr[pl.ds(k * ns + s, seq, stride=SUBLANES), :])


def _s5(rec, bb, cc, a_re, a_im, d_skip, s0_re, s0_im, seq, ns, nc):
    r = rec.shape[0]
    n_cb = N_CH // nc
    tc = 128
    wide = nc * CH
    state = pl.BlockSpec((1, 1, 2, SUBLANES, S5_ST), lambda g, c: (g, c, 0, 0, 0))
    aspec = pl.BlockSpec((2, 1, SUBLANES, S5_ST), lambda g, c: (0, c, 0, 0))
    st_shape = jax.ShapeDtypeStruct((r // (ns * seq), n_cb, 2, SUBLANES, S5_ST), F32)
    return pl.pallas_call(
        functools.partial(_s5_kernel, seq=seq, tc=tc, ns=ns, nc=nc),
        grid=(r // (ns * seq), n_cb),
        in_specs=[pl.BlockSpec((ns * seq, wide), lambda g, c: (g, 2 * n_cb + c)),
                  pl.BlockSpec((2, 1, wide, 2 * S5_ST), lambda g, c: (0, c, 0, 0)),
                  pl.BlockSpec((2, 1, 2 * S5_ST, wide), lambda g, c: (0, c, 0, 0)),
                  aspec, aspec,
                  pl.BlockSpec((1, wide), lambda g, c: (0, c)),
                  state, state],
        out_specs=[pl.BlockSpec((ns * seq, wide), lambda g, c: (g, c)), state, state],
        out_shape=[jax.ShapeDtypeStruct((r, D_S5), F32), st_shape, st_shape],
        scratch_shapes=[pltpu.VMEM((seq * SUBLANES, CH), F32), pltpu.VMEM((seq * SUBLANES, CH), F32),
                        pltpu.VMEM((tc * SUBLANES, 2 * S5_ST), F32), pltpu.VMEM((tc * SUBLANES, 2 * S5_ST), F32)],
        compiler_params=_cparams(("parallel", "parallel"), 48),
        name="s5",
    )(rec, bb, cc, a_re, a_im, d_skip, s0_re, s0_im)


def _s5_params(lam_re, lam_im, log_dt, b_re, b_im, c_re, c_im, ns, nc):
    dt = jnp.exp(log_dt)[..., None]
    mag = jnp.exp(lam_re * dt)
    abar_re, abar_im = mag * jnp.cos(lam_im * dt), mag * jnp.sin(lam_im * dt)
    den = lam_re * lam_re + lam_im * lam_im
    nr, ni = abar_re - 1, abar_im
    cr = (nr * lam_re + ni * lam_im) / den
    ci = (ni * lam_re - nr * lam_im) / den
    bb_re = cr[..., None] * b_re - ci[..., None] * b_im
    bb_im = cr[..., None] * b_im + ci[..., None] * b_re
    gpc = CH // S5_GROUP
    eye = jnp.eye(gpc, dtype=F32)

    def in_map(b):
        b = b.reshape(2, N_CH, gpc, S5_STATE, S5_GROUP).transpose(0, 1, 2, 4, 3)
        return (b[:, :, :, :, None, :] * eye[None, None, :, None, :, None]).reshape(2, N_CH, CH, S5_ST)

    def out_map(c):
        c = c.reshape(2, N_CH, gpc, S5_GROUP, S5_STATE).transpose(0, 1, 2, 4, 3)
        return (c[:, :, :, :, None, :] * eye[None, None, :, None, :, None]).reshape(2, N_CH, S5_ST, CH)

    bb = jnp.concatenate([in_map(bb_re), in_map(bb_im)], axis=-1).astype(BF16)
    cc = jnp.concatenate([out_map(c_re), -out_map(c_im)], axis=-2).astype(BF16)
    n_cb = N_CH // nc
    bb = bb.reshape(2, n_cb, nc * CH, 2 * S5_ST)
    cc = cc.reshape(2, n_cb, nc, 2 * S5_ST, CH).transpose(0, 1, 3, 2, 4).reshape(2, n_cb, 2 * S5_ST, nc * CH)

    def per_slot(a):
        a = jnp.broadcast_to(a.reshape(2, n_cb, nc, 1, S5_ST), (2, n_cb, nc, ns, S5_ST))
        return a.reshape(2, n_cb, SUBLANES, S5_ST)

    return bb, cc, per_slot(abar_re), per_slot(abar_im)


def _out_proj_kernel(x_ref, oa_ref, ob_ref, y_ref, wg_ref, bg_ref, wo_ref, mod_ref, g_ref, *refs, route):
    if route:
        r_ref, _, x1_ref, h2_ref, route_ref = refs
    else:
        x1_ref, h2_ref = refs
    z = jax.nn.gelu(y_ref[...])
    gl = jnp.dot(z.astype(BF16), wg_ref[...], preferred_element_type=F32) + bg_ref[...]
    oc = z * jax.nn.sigmoid(gl)
    mix = (jnp.dot(oa_ref[...], wo_ref[:D_ATTN, :], preferred_element_type=F32)
           + jnp.dot(ob_ref[...].astype(BF16), wo_ref[D_ATTN:D_ATTN + D_LRU, :], preferred_element_type=F32)
           + jnp.dot(oc.astype(BF16), wo_ref[D_ATTN + D_LRU:, :], preferred_element_type=F32))
    x1 = x_ref[...] + mod_ref[0, 2:3, :] * mix
    x1_ref[...] = x1
    h2 = _rms_mod(x1, g_ref[...], mod_ref[0, 3:4, :], mod_ref[0, 4:5, :])
    h2_ref[...] = h2.astype(h2_ref.dtype)
    if route:
        hi = h2.astype(BF16)
        lo = (h2 - hi.astype(F32)).astype(BF16)
        pr = (jnp.dot(hi, r_ref[...], preferred_element_type=F32)
              + jnp.dot(lo, r_ref[...], preferred_element_type=F32))
        logits = pr + pltpu.roll(pr, LANES - N_EXPERTS, axis=1)
        lane = lax.broadcasted_iota(jnp.int32, logits.shape, 1)
        lg = jnp.where(lane < N_EXPERTS, logits, -jnp.inf)
        m1 = jnp.max(lg, axis=-1, keepdims=True)
        i1 = jnp.min(jnp.where(lg == m1, lane, LANES), axis=-1, keepdims=True)
        lg2 = jnp.where(lane == i1, -jnp.inf, lg)
        m2 = jnp.max(lg2, axis=-1, keepdims=True)
        i2 = jnp.min(jnp.where(lg2 == m2, lane, LANES), axis=-1, keepdims=True)
        e2 = jnp.exp(m2 - m1)
        den = 1.0 + e2
        route_ref[...] = (jnp.where(lane == 0, i1.astype(F32), 0.0) + jnp.where(lane == 1, i2.astype(F32), 0.0)
                          + jnp.where(lane == 2, 1.0 / den, 0.0) + jnp.where(lane == 3, e2 / den, 0.0))


def _out_proj(x2d, oa, ob, y, w_glu_bf, b_glu, w_out_bf, mod, mod_base, rows_per_mod, g, routing):
    r, d = x2d.shape
    tm = 512
    tiles_per_mod = rows_per_mod // tm
    route = routing is not None
    row = lambda n: pl.BlockSpec((tm, n), lambda i: (i, 0))
    full = lambda a: pl.BlockSpec(a.shape, lambda i: (0,) * a.ndim, pipeline_mode=pl.Buffered(1))
    args = [x2d, oa, ob, y, w_glu_bf, b_glu, w_out_bf, mod, g]
    in_specs = [row(d), row(D_ATTN), row(D_LRU), row(D_S5), full(w_glu_bf), full(b_glu), full(w_out_bf),
                pl.BlockSpec((1, N_MOD, d), lambda i: (mod_base + i // tiles_per_mod, 0, 0)), full(g)]
    aliases = {}
    if route:
        router_pad, shared_rows, row_base = routing
        args += [router_pad, shared_rows]
        in_specs += [full(router_pad), pl.BlockSpec(memory_space=pl.ANY)]
        aliases = {len(args) - 1: 1}
        out_shape = [jax.ShapeDtypeStruct((r, d), F32), jax.ShapeDtypeStruct(shared_rows.shape, F32),
                     jax.ShapeDtypeStruct((r, LANES), F32)]
        out_specs = [row(d), pl.BlockSpec((tm, d), lambda i: (i + row_base // tm, 0)), row(LANES)]
    else:
        out_shape = [jax.ShapeDtypeStruct((r, d), F32), jax.ShapeDtypeStruct((r, d), BF16)]
        out_specs = [row(d), row(d)]
    return pl.pallas_call(
        functools.partial(_out_proj_kernel, route=route),
        grid=(r // tm,),
        in_specs=in_specs,
        out_specs=out_specs,
        out_shape=out_shape,
        input_output_aliases=aliases,
        compiler_params=_cparams(("parallel",), 56),
        name="out_proj",
    )(*args)


def _swiglu_part(h, w1, w3, w2):
    a = jnp.dot(h, w1, preferred_element_type=F32)
    b = jnp.dot(h, w3, preferred_element_type=F32)
    return jnp.dot((jax.nn.silu(a) * b).astype(BF16), w2, preferred_element_type=F32)


def _ffn_kernel(h_ref, x_ref, mod_ref, w1_ref, w3_ref, w2_ref, o_ref):
    f = pl.program_id(1)
    part = _swiglu_part(h_ref[...], w1_ref[...], w3_ref[...], w2_ref[...])

    @pl.when(f == 0)
    def _():
        o_ref[...] = part

    @pl.when(f > 0)
    def _():
        o_ref[...] += part

    @pl.when(f == pl.num_programs(1) - 1)
    def _():
        o_ref[...] = x_ref[...] + mod_ref[0, 5:6, :] * o_ref[...]


def _ffn(h2, x1, mod, mod_base, rows_per_mod, w1, w3, w2):
    r, d = x1.shape
    tm, tf = 1024, 512
    tiles_per_mod = rows_per_mod // tm
    return pl.pallas_call(
        _ffn_kernel,
        grid=(r // tm, w1.shape[1] // tf),
        in_specs=[pl.BlockSpec((tm, d), lambda i, f: (i, 0)),
                  pl.BlockSpec((tm, d), lambda i, f: (i, 0), pipeline_mode=pl.Buffered(1)),
                  pl.BlockSpec((1, N_MOD, d), lambda i, f: (mod_base + i // tiles_per_mod, 0, 0)),
                  pl.BlockSpec((d, tf), lambda i, f: (0, f)),
                  pl.BlockSpec((d, tf), lambda i, f: (0, f)),
                  pl.BlockSpec((tf, d), lambda i, f: (f, 0))],
        out_specs=pl.BlockSpec((tm, d), lambda i, f: (i, 0)),
        out_shape=jax.ShapeDtypeStruct((r, d), F32),
        compiler_params=_cparams(("parallel", "arbitrary"), 60),
        name="ffn",
    )(h2, x1, mod, w1, w3, w2)


MOE_TM = 512
MOE_TF = 1408
MOE_TT = 256


def _route_plan(route):
    t = route.shape[0]
    n_pairs = 2 * t
    n_tiles = n_pairs // MOE_TM + N_EXPERTS
    experts = route[:, :2].astype(jnp.int32).reshape(n_pairs)
    onehot = (experts[:, None] == jnp.arange(N_EXPERTS, dtype=jnp.int32)[None]).astype(jnp.int32)
    csum = jnp.cumsum(onehot, axis=0)
    rank = jnp.sum(onehot * csum, axis=1) - 1
    counts = csum[-1]
    padded = (counts + MOE_TM - 1) // MOE_TM * MOE_TM
    ends = jnp.cumsum(padded)
    pos = jnp.sum(onehot * (ends - padded)[None], axis=1) + rank
    src = jnp.zeros((n_tiles * MOE_TM,), jnp.int32).at[pos].set(
        jnp.arange(n_pairs, dtype=jnp.int32) // 2, unique_indices=True, mode='promise_in_bounds')
    tile_start = jnp.arange(n_tiles, dtype=jnp.int32) * MOE_TM
    tile_valid = (tile_start < ends[-1]).astype(jnp.int32)
    tile_expert = jnp.sum((tile_start[:, None] >= ends[None]).astype(jnp.int32), axis=1)
    last_expert = jnp.max(jnp.where(counts > 0, jnp.arange(N_EXPERTS, dtype=jnp.int32), 0))
    tile_expert = jnp.where(tile_valid == 1, tile_expert, last_expert)
    return pos, src, tile_expert, tile_valid


def _gather_rows(idx_of_row, n_rows, src_hbm, dst, sem):
    def row(r, carry):
        pltpu.make_async_copy(src_hbm.at[pl.ds(idx_of_row(r), 1)], dst.at[pl.ds(r, 1)], sem).start()
        return carry
    lax.fori_loop(0, n_rows, row, 0, unroll=8)


def _wait_rows(n_rows, src_hbm, dst, sem):
    pltpu.make_async_copy(src_hbm.at[pl.ds(0, n_rows)], dst, sem).wait()


def _moe_experts_kernel(src_ref, texp_ref, valid_ref, h_hbm, w1_ref, w3_ref, w2_ref, o_ref, xbuf, xbf, sem,
                        *, rows_per_step):
    i = pl.program_id(0)
    f = pl.program_id(1)
    n_i = pl.num_programs(0)

    @pl.when(f == 0)
    def _():
        @pl.when(i == 0)
        def _():
            _gather_rows(lambda r: src_ref[r], MOE_TM, h_hbm, xbuf, sem.at[0])

        _wait_rows(MOE_TM, h_hbm, xbuf, sem.at[0])
        xbf[...] = xbuf[...].astype(BF16)

    nxt = jnp.where(i + 1 < n_i, i + 1, 0)

    def start_next_rows():
        for r in range(rows_per_step):
            row = f * rows_per_step + r
            pltpu.make_async_copy(h_hbm.at[pl.ds(src_ref[nxt * MOE_TM + row], 1)], xbuf.at[pl.ds(row, 1)],
                                  sem.at[0]).start()

    valid = valid_ref[i] == 1

    @pl.when(valid)
    def _():
        part = _swiglu_part(xbf[...], w1_ref[0], w3_ref[0], w2_ref[0])
        start_next_rows()

        @pl.when(f == 0)
        def _():
            o_ref[...] = part

        @pl.when(f > 0)
        def _():
            o_ref[...] += part

    @pl.when(jnp.logical_not(valid))
    def _():
        start_next_rows()

        @pl.when(f == 0)
        def _():
            o_ref[...] = jnp.zeros_like(o_ref)

    @pl.when((i == n_i - 1) & (f == pl.num_programs(1) - 1))
    def _():
        _wait_rows(MOE_TM, h_hbm, xbuf, sem.at[0])


def _moe_experts(h2, src, tile_expert, tile_valid, w1, w3, w2):
    t, d = h2.shape
    n_tiles = tile_expert.shape[0]
    n_f = w1.shape[2] // MOE_TF

    def hidden_tile(i, f, valid):
        odd = i % 2 == 1
        return jnp.where(valid[i] == 1, jnp.where(odd, n_f - 1 - f, f), jnp.where(odd, n_f - 1, 0))

    def wmap(i, f, src, texp, valid):
        return (texp[i], 0, hidden_tile(i, f, valid))

    def w2map(i, f, src, texp, valid):
        return (texp[i], hidden_tile(i, f, valid), 0)

    grid_spec = pltpu.PrefetchScalarGridSpec(
        num_scalar_prefetch=3,
        grid=(n_tiles, n_f),
        in_specs=[pl.BlockSpec(memory_space=pl.ANY),
                  pl.BlockSpec((1, d, MOE_TF), wmap),
                  pl.BlockSpec((1, d, MOE_TF), wmap),
                  pl.BlockSpec((1, MOE_TF, d), w2map)],
        out_specs=pl.BlockSpec((MOE_TM, d), lambda i, f, *_: (i, 0)),
        scratch_shapes=[pltpu.VMEM((MOE_TM, d), F32), pltpu.VMEM((MOE_TM, d), BF16),
                        pltpu.SemaphoreType.DMA((1,))])
    return pl.pallas_call(
        functools.partial(_moe_experts_kernel, rows_per_step=MOE_TM // n_f),
        grid_spec=grid_spec,
        out_shape=jax.ShapeDtypeStruct((n_tiles * MOE_TM, d), F32),
        compiler_params=_cparams(("arbitrary", "arbitrary"), 62),
        name="moe_experts",
    )(src, tile_expert, tile_valid, h2, w1, w3, w2)


def _moe_combine_kernel(pos_ref, x_ref, mod_ref, route_ref, y_hbm, o_ref, buf, sem, *, tok_base):
    i = pl.program_id(0)
    slot = i % 2

    def gather(tile, slot):
        for k in range(2):
            _gather_rows(lambda r, k=k: pos_ref[2 * (tok_base + tile * MOE_TT + r) + k], MOE_TT, y_hbm,
                         buf.at[slot, k], sem.at[slot])

    @pl.when(i == 0)
    def _():
        gather(0, 0)

    for k in range(2):
        _wait_rows(MOE_TT, y_hbm, buf.at[slot, k], sem.at[slot])

    @pl.when(i + 1 < pl.num_programs(0))
    def _():
        gather(i + 1, 1 - slot)

    mixed = route_ref[:, 2:3] * buf[slot, 0] + route_ref[:, 3:4] * buf[slot, 1]
    o_ref[...] = x_ref[...] + mod_ref[0, 5:6, :] * mixed


def _moe_combine(pos, x1, mod, mod_base, rows_per_mod, route, y_rows, tok_base):
    r, d = x1.shape
    tiles_per_mod = rows_per_mod // MOE_TT
    grid_spec = pltpu.PrefetchScalarGridSpec(
        num_scalar_prefetch=1,
        grid=(r // MOE_TT,),
        in_specs=[pl.BlockSpec((MOE_TT, d), lambda i, *_: (i, 0)),
                  pl.BlockSpec((1, N_MOD, d), lambda i, *_: (mod_base + i // tiles_per_mod, 0, 0)),
                  pl.BlockSpec((MOE_TT, LANES), lambda i, *_: (i, 0)),
                  pl.BlockSpec(memory_space=pl.ANY)],
        out_specs=pl.BlockSpec((MOE_TT, d), lambda i, *_: (i, 0)),
        scratch_shapes=[pltpu.VMEM((2, 2, MOE_TT, d), F32), pltpu.SemaphoreType.DMA((2,))])
    return pl.pallas_call(
        functools.partial(_moe_combine_kernel, tok_base=tok_base),
        grid_spec=grid_spec,
        out_shape=jax.ShapeDtypeStruct((r, d), F32),
        compiler_params=_cparams(("arbitrary",), 32),
        name="moe_combine",
    )(pos, x1, mod, route, y_rows)


def _slot_states(s, ns, nc, width):
    ng = s.shape[0] // ns
    s = s.reshape(ng, ns, 2, N_CH // nc, nc, width).transpose(0, 3, 2, 4, 1, 5)
    return s.reshape(ng, N_CH // nc, 2, SUBLANES, width)


def _unslot_states(s, ns, nc, width):
    ng = s.shape[0]
    s = s.reshape(ng, N_CH // nc, 2, nc, ns, width).transpose(0, 4, 2, 1, 3, 5)
    return s.reshape(ng * ns, 2, N_CH * width)


def _token_mixer(x2d, batch, seq, mod, mod_base, rows_per_mod, p, kv_ctx, lru_h0, s5_h0, prev_kv=None,
                 routing=None):
    is_ctx = kv_ctx is None
    ns = min(batch, SUBLANES)
    nc = SUBLANES // ns
    outs = _in_proj(x2d, p['g_mix'], mod, mod_base, rows_per_mod, p['w_in'], p['q_g'], p['k_g'], p['ones_bd'],
                    seq, is_ctx, prev_kv)
    q, k, v, rec = outs[:4]
    if is_ctx:
        oa = _ctx_attn(q, k, v, seq)
        lru_h0 = jnp.zeros((batch, 2, D_LRU), F32)
        s5_h0 = (jnp.zeros((batch, 2, N_S5_GROUPS * S5_STATE), F32),) * 2
    else:
        oa = _nbr_attn(q, k, v, kv_ctx[0], kv_ctx[1], kv_ctx[2], p['tb'], seq)
    lru_w, lru_b = _rglru_params(p['lru_wa'], p['lru_ba'], p['lru_wi'], p['lru_bi'], nc)
    ob, lru_fin = _rglru(rec, p['conv_w'], p['conv_b'], lru_w, lru_b, p['lru_lam'],
                         _slot_states(lru_h0, ns, nc, CH), seq, ns, nc)
    s5_bb, s5_cc, s5_are, s5_aim = _s5_params(*p['s5'], ns, nc)
    y, fin_re, fin_im = _s5(rec, s5_bb, s5_cc, s5_are, s5_aim, p['d_skip'],
                            _slot_states(s5_h0[0], ns, nc, S5_ST), _slot_states(s5_h0[1], ns, nc, S5_ST),
                            seq, ns, nc)
    res = _out_proj(x2d, oa, ob, y, p['w_glu'], p['b_glu'], p['w_out'], mod, mod_base, rows_per_mod,
                    p['g_ffn'], routing)
    if not is_ctx:
        return res, None
    state = (outs[4], outs[5], _unslot_states(lru_fin, ns, nc, CH),
             _unslot_states(fin_re, ns, nc, S5_ST).reshape(batch, 2, N_S5_GROUPS, S5_STATE),
             _unslot_states(fin_im, ns, nc, S5_ST).reshape(batch, 2, N_S5_GROUPS, S5_STATE))
    return res, state


def kernel(x_prompt, x_sample, c, cache_k, cache_v, state_lru, state_s5_re, state_s5_im, c_ctx, norm_mix_g, norm_ffn_g, w_mod, b_mod, w_in, w_out, q_norm_g, k_norm_g, rpb, lru_conv_w, lru_conv_b, lru_wa, lru_ba, lru_wi, lru_bi, lru_lam, s5_lam_re, s5_lam_im, s5_log_dt, s5_b_re, s5_b_im, s5_c_re, s5_c_im, s5_d, s5_w_glu, s5_b_glu, ffn_w1, ffn_w3, ffn_w2, moe_router, moe_w1, moe_w3, moe_w2):
    batch, seq, d = x_prompt.shape
    dec_batch, dec_seq, _ = x_sample.shape
    depth = w_in.shape[0]
    assert dec_batch + 1 <= SUBLANES and batch % SUBLANES == 0 and dec_batch in (4, 8)

    cvecs = jnp.concatenate([c_ctx[None], c, jnp.zeros((SUBLANES - 1 - dec_batch, d), F32)], axis=0)
    mods = _adaln(cvecs, w_mod, b_mod).reshape(depth, SUBLANES, N_MOD, d)

    heads_per_tile = 512 // HEAD_DIM
    ones_bd = jnp.asarray(np.kron(np.eye(heads_per_tile // 2), np.ones((HEAD_DIM, HEAD_DIM))), BF16)

    xp = x_prompt.reshape(batch * seq, d)
    xs = x_sample.reshape(dec_batch * dec_seq, d)
    n_tok = (batch * seq, dec_batch * dec_seq)
    kv, lrus, s5rs, s5is = None, [], [], []
    for l in range(depth):
        p = {
            'g_mix': norm_mix_g[l][None], 'g_ffn': norm_ffn_g[l][None],
            'w_in': w_in[l].astype(BF16), 'w_out': w_out[l].astype(BF16),
            'q_g': jnp.tile(q_norm_g[l], heads_per_tile)[None], 'k_g': jnp.tile(k_norm_g[l], heads_per_tile)[None],
            'ones_bd': ones_bd, 'tb': _bias_table(rpb[l]),
            'conv_w': lru_conv_w[l], 'conv_b': lru_conv_b[l][None],
            'lru_wa': lru_wa[l], 'lru_ba': lru_ba[l], 'lru_wi': lru_wi[l], 'lru_bi': lru_bi[l], 'lru_lam': lru_lam[l],
            's5': (s5_lam_re[l], s5_lam_im[l], s5_log_dt[l], s5_b_re[l], s5_b_im[l], s5_c_re[l], s5_c_im[l]),
            'd_skip': s5_d[l][None], 'w_glu': s5_w_glu[l].astype(BF16), 'b_glu': s5_b_glu[l][None],
        }
        j = l // 2
        dense = l % 2 == 0
        route_p = route_s = None
        if not dense:
            r_hi = moe_router[j].astype(BF16)
            r_lo = (moe_router[j] - r_hi.astype(F32)).astype(BF16)
            router = jnp.pad(jnp.concatenate([r_hi, r_lo], axis=1), ((0, 0), (0, LANES - 2 * N_EXPERTS)))
            route_p = (router, jnp.zeros((sum(n_tok), d), F32), 0)
        res_p, (k_c, v_c, lru_c, s5r_c, s5i_c) = _token_mixer(
            xp, batch, seq, mods[l], 0, batch * seq, p, None, None, None, prev_kv=kv, routing=route_p)
        kv = (k_c, v_c)
        lrus.append(lru_c)
        s5rs.append(s5r_c)
        s5is.append(s5i_c)
        if not dense:
            route_s = (router, res_p[1], n_tok[0])
        res_s, _ = _token_mixer(
            xs, dec_batch, dec_seq, mods[l], 1, dec_seq, p, (cache_k, cache_v, l), state_lru[:, l],
            (state_s5_re[:, l].reshape(dec_batch, 2, -1), state_s5_im[:, l].reshape(dec_batch, 2, -1)),
            routing=route_s)
        if dense:
            w = (ffn_w1[j].astype(BF16), ffn_w3[j].astype(BF16), ffn_w2[j].astype(BF16))
            xp = _ffn(res_p[1], res_p[0], mods[l], 0, batch * seq, *w)
            xs = _ffn(res_s[1], res_s[0], mods[l], 1, dec_seq, *w)
        else:
            h2 = res_s[1]
            pos, src, tile_expert, tile_valid = _route_plan(jnp.concatenate([res_p[2], res_s[2]], axis=0))
            y_rows = _moe_experts(h2, src, tile_expert, tile_valid,
                                  moe_w1[j].astype(BF16), moe_w3[j].astype(BF16), moe_w2[j].astype(BF16))
            xp = _moe_combine(pos, res_p[0], mods[l], 0, batch * seq, res_p[2], y_rows, 0)
            xs = _moe_combine(pos, res_s[0], mods[l], 1, dec_seq, res_s[2], y_rows, batch * seq)
    return (xp.reshape(batch, seq, d), xs.reshape(dec_batch, dec_seq, d), kv[0], kv[1],
            jnp.stack(lrus, axis=1), jnp.stack(s5rs, axis=1), jnp.stack(s5is, axis=1))
```
